```python
import math
import jax, jax.numpy as jnp
from jax import lax
import numpy as np


D_MODEL = 4096
BATCH = 32
SEQ = 256
DEPTH = 2
DEC_BATCH = 2
DEC_SEQ = 1024
PAST_LEN = 256

GRID_W = 64
N_HEADS = 16
HEAD_DK = 128
HEAD_DV = 128
W_QK = N_HEADS * HEAD_DK
W_DN = N_HEADS * HEAD_DV
W_SC = 2048
D_FF = 11008
CHUNK = 64
CONV_W = 3
EPS = 1e-6
N_IN = 2 * W_QK + 2 * W_DN + 4 * N_HEADS + 3 * W_SC + 2 * D_MODEL

kernel_name = 'hybrid_deltanet_shortconv_prefix_dit_step'


def _rmsnorm(x, g):
    xf = x.astype(jnp.float32)
    y = xf * lax.rsqrt(jnp.mean(xf * xf, axis=-1, keepdims=True) + EPS)
    return (y * g.astype(jnp.float32)).astype(x.dtype)


def _l2norm(x):
    return x * lax.rsqrt(jnp.sum(x * x, axis=-1, keepdims=True) + EPS)


def _dwconv3(x, w, axis):
    n = x.shape[axis]
    pad = [(0, 0)] * x.ndim
    pad[axis] = (1, 1)
    xp = jnp.pad(x, pad)
    taps = [lax.slice_in_dim(xp, i, i + n, axis=axis) for i in range(CONV_W)]
    return taps[0] * w[0] + taps[1] * w[1] + taps[2] * w[2]


def _seq_conv(x, w):
    return _dwconv3(x, w, 1)


def _row_conv(x, w):
    b, n, ch = x.shape
    rows = n // GRID_W
    return _dwconv3(x.reshape(b, rows, GRID_W, ch), w, 2).reshape(b, n, ch)


def _col_conv(x, w):
    b, n, ch = x.shape
    rows = n // GRID_W
    return _dwconv3(x.reshape(b, rows, GRID_W, ch), w, 1).reshape(b, n, ch)


def _in_splits():
    sizes = (2 * W_QK + W_DN, W_DN, 4 * N_HEADS, W_SC, W_SC, W_SC)
    out, acc = [], 0
    for s in sizes:
        acc += s
        out.append(acc)
    return out


def _adaln(cvec, w, b):
    return (jax.nn.silu(cvec) @ w + b)[:, None, :]


def _gated_delta_chunked(q, k, v, beta, g, s0):
    b, n, h, dk = q.shape
    dv = v.shape[-1]
    nc = n // CHUNK

    def blk(t):
        return t.reshape(b, nc, CHUNK, h, -1).transpose(0, 3, 1, 2, 4)

    def blk_s(t):
        return t.reshape(b, nc, CHUNK, h).transpose(0, 3, 1, 2)

    q, k, v = blk(q), blk(k), blk(v)
    beta = blk_s(beta)
    gc = jnp.cumsum(blk_s(g), axis=-1)
    tril = jnp.tril(jnp.ones((CHUNK, CHUNK), dtype=bool))
    strict = jnp.tril(jnp.ones((CHUNK, CHUNK), dtype=bool), -1)
    diff = gc[..., :, None] - gc[..., None, :]
    decay = jnp.where(tril, jnp.exp(jnp.where(tril, diff, 0.0)), 0.0)
    kk = jnp.einsum('bhncd,bhnmd->bhncm', k, k)
    a_mat = jnp.where(strict, kk * decay * beta[..., :, None], 0.0)
    rhs = jnp.concatenate([v * beta[..., None], k * (beta * jnp.exp(gc))[..., None]], axis=-1)
    sol = lax.linalg.triangular_solve(a_mat + jnp.eye(CHUNK, dtype=a_mat.dtype), rhs,
                                      left_side=True, lower=True, unit_diagonal=True)
    u, w = sol[..., :dv], sol[..., dv:]
    intra = jnp.where(tril, jnp.einsum('bhncd,bhnmd->bhncm', q, k) * decay, 0.0)
    g_last = gc[..., -1]
    k_dec = k * jnp.exp(g_last[..., None] - gc)[..., None]
    q_dec = q * jnp.exp(gc)[..., None]

    def step(s, xs):
        q_c, k_c, u_c, w_c, a_c, gl = xs
        v_new = u_c - jnp.einsum('bhck,bhkv->bhcv', w_c, s)
        o = jnp.einsum('bhck,bhkv->bhcv', q_c, s) + jnp.einsum('bhcm,bhmv->bhcv', a_c, v_new)
        s = s * jnp.exp(gl)[..., None, None] + jnp.einsum('bhck,bhcv->bhkv', k_c, v_new)
        return s, o

    xs = tuple(jnp.moveaxis(t, 2, 0) for t in (q_dec, k_dec, u, w, intra, g_last))
    s_fin, o = lax.scan(step, s0, xs)
    o = o.transpose(1, 0, 3, 2, 4).reshape(b, n, h, dv)
    return o, s_fin


def _layer(x, mod, s0, p, conv_mix, conv_ffn):
    sh_a, sc_a, gt_a, sh_f, sc_f, gt_f = jnp.split(mod, 6, axis=-1)
    b, n, _ = x.shape
    f32 = jnp.float32
    h = _rmsnorm(x, p['norm1_g']) * (1.0 + sc_a) + sh_a
    proj = h @ p['w_in']
    qkv, z, bg, xs, bs, cs, gates = jnp.split(proj, _in_splits(), axis=-1)
    qkv = jax.nn.silu(conv_mix(qkv, p['conv_qkv']))
    q, k, v = jnp.split(qkv, [W_QK, 2 * W_QK], axis=-1)
    q = _l2norm(q.reshape(b, n, N_HEADS, HEAD_DK).astype(f32)) * (HEAD_DK ** -0.5)
    k = _l2norm(k.reshape(b, n, N_HEADS, HEAD_DK).astype(f32))
    v = v.reshape(b, n, N_HEADS, HEAD_DV).astype(f32)
    bg = bg.astype(f32).reshape(b, n, 4, N_HEADS)
    beta = jax.nn.sigmoid(bg[:, :, 0:2])
    g = -jnp.exp(p['a_log'].astype(f32)) * jax.nn.softplus(bg[:, :, 2:4] + p['dt_bias'].astype(f32))
    s0 = s0.astype(f32)
    o_f, s_f = _gated_delta_chunked(q, k, v, beta[:, :, 0], g[:, :, 0], s0[:, 0])
    rev = lambda t: jnp.flip(t, axis=1)
    o_b, s_b = _gated_delta_chunked(rev(q), rev(k), rev(v), rev(beta[:, :, 1]), rev(g[:, :, 1]), s0[:, 1])
    o = o_f + rev(o_b)
    o = _rmsnorm(o, p['onorm_g']) * jax.nn.silu(z.reshape(b, n, N_HEADS, HEAD_DV).astype(f32))
    o = o.reshape(b, n, W_DN).astype(x.dtype)
    y_sc = bs * conv_mix(cs * xs, p['conv_sc'])
    g_a, g_b = jnp.split(jax.nn.sigmoid(gates), 2, axis=-1)
    merged = g_a * (o @ p['w_pa']) + g_b * (y_sc @ p['w_pb'])
    x = x + gt_a * (merged @ p['w_o'])
    h = _rmsnorm(x, p['norm2_g']) * (1.0 + sc_f) + sh_f
    up_g, up_v = jnp.split(h @ p['w_up'], 2, axis=-1)
    f = jax.nn.silu(conv_ffn(up_g, p['conv_ff'])) * up_v
    x = x + gt_f * (f @ p['w_down'])
    return x, jnp.stack([s_f, s_b], axis=1)


def setup_inputs(seed: int = 0) -> dict:
    key = jax.random.key(seed)
    ks = jax.random.split(key, 24)
    f32 = jnp.float32
    L, H, D = DEPTH, N_HEADS, D_MODEL

    def nrm(k, shape, scale):
        return jax.random.normal(k, shape, f32) * scale

    dt = jnp.exp(jax.random.uniform(ks[12], (L, 2, H), f32, math.log(1e-3), math.log(1e-1)))
    return {
        'x_prompt': nrm(ks[0], (BATCH, SEQ, D), 1.0),
        'x_sample': nrm(ks[1], (DEC_BATCH, DEC_SEQ, D), 1.0),
        'state_dn': nrm(ks[2], (DEC_BATCH, DEPTH, 2, H, HEAD_DK, HEAD_DV), HEAD_DK ** -0.5),
        'c': nrm(ks[3], (DEC_BATCH, D), 1.0),
        'c_ctx': nrm(ks[4], (D,), 1.0),
        'norm1_g': 1.0 + nrm(ks[5], (L, D), 0.01),
        'norm2_g': 1.0 + nrm(ks[6], (L, D), 0.01),
        'w_ada': nrm(ks[7], (L, D, 6 * D), 0.5 * D ** -0.5),
        'b_ada': nrm(ks[8], (L, 6 * D), 0.01),
        'w_in': nrm(ks[9], (L, D, N_IN), D ** -0.5),
        'conv_qkv': nrm(ks[10], (L, CONV_W, 2 * W_QK + W_DN), CONV_W ** -0.5),
        'a_log': jnp.log(jax.random.uniform(ks[11], (L, 2, H), f32, 1.0, 16.0)),
        'dt_bias': dt + jnp.log(-jnp.expm1(-dt)),
        'onorm_g': 1.0 + nrm(ks[13], (L, HEAD_DV), 0.01),
        'conv_sc': nrm(ks[14], (L, CONV_W, W_SC), CONV_W ** -0.5),
        'w_pa': nrm(ks[15], (L, W_DN, D), W_DN ** -0.5),
        'w_pb': nrm(ks[16], (L, W_SC, D), W_SC ** -0.5),
        'w_o': nrm(ks[17], (L, D, D), D ** -0.5),
        'w_up': nrm(ks[18], (L, D, 2 * D_FF), D ** -0.5),
        'conv_ff': nrm(ks[19], (L, CONV_W, D_FF), CONV_W ** -0.5),
        'w_down': nrm(ks[20], (L, D_FF, D), D_FF ** -0.5),
        'final_g': 1.0 + nrm(ks[21], (D,), 0.01),
    }


def reference(x_prompt, x_sample, state_dn, c, c_ctx, norm1_g, norm2_g, w_ada, b_ada, w_in,
              conv_qkv, a_log, dt_bias, onorm_g, conv_sc, w_pa, w_pb, w_o, w_up, conv_ff,
              w_down, final_g):
    xc, xl = x_prompt, x_sample
    states = []
    for l in range(DEPTH):
        p = {'norm1_g': norm1_g[l], 'norm2_g': norm2_g[l], 'w_in': w_in[l],
             'conv_qkv': conv_qkv[l], 'a_log': a_log[l], 'dt_bias': dt_bias[l],
             'onorm_g': onorm_g[l], 'conv_sc': conv_sc[l], 'w_pa': w_pa[l], 'w_pb': w_pb[l],
             'w_o': w_o[l], 'w_up': w_up[l], 'conv_ff': conv_ff[l], 'w_down': w_down[l]}
        mod_ctx = _adaln(c_ctx[None, :], w_ada[l], b_ada[l])
        mod_lat = _adaln(c, w_ada[l], b_ada[l])
        s_zero = jnp.zeros((xc.shape[0], 2, N_HEADS, HEAD_DK, HEAD_DV), jnp.float32)
        xc, s_ctx = _layer(xc, mod_ctx, s_zero, p, _seq_conv, _seq_conv)
        states.append(s_ctx)
        xl, _ = _layer(xl, mod_lat, state_dn[:, l], p, _row_conv, _col_conv)
    y_prompt = _rmsnorm(xc, final_g)
    y_sample = _rmsnorm(xl, final_g)
    new_state_dn = jnp.stack(states, axis=1).astype(x_prompt.dtype)
    return (y_prompt, y_sample, new_state_dn)
```

```python
import functools
from typing import NamedTuple

import jax
import jax.numpy as jnp
from jax import lax
from jax.experimental import pallas as pl
from jax.experimental.pallas import tpu as pltpu

F32 = jnp.float32
BF16 = jnp.bfloat16
EPS = 1e-6
CHUNK = 64
LANES = 128
MOD_ROWS = 8
V7X_VMEM_BYTES = 64 * 1024 * 1024
VMEM_LIMIT_BYTES = V7X_VMEM_BYTES - 8 * 1024 * 1024


class Cfg(NamedTuple):
    d: int
    batch: int
    seq: int
    depth: int
    dec_batch: int
    dec_seq: int
    grid_w: int
    heads: int
    w_sc: int
    d_ff: int
    tm: int
    tn: int
    ff_tn: int
    ff_tk: int

    @property
    def w_hd(self):
        return self.heads * LANES

    @property
    def tok_ctx(self):
        return self.batch * self.seq

    @property
    def tok(self):
        return self.batch * self.seq + self.dec_batch * self.dec_seq

    @property
    def ff_pad(self):
        return -(-self.d_ff // self.ff_tn) * self.ff_tn


CFG = Cfg(d=4096, batch=32, seq=256, depth=2, dec_batch=2, dec_seq=1024, grid_w=64, heads=16,
          w_sc=2048, d_ff=11008, tm=1024, tn=1024, ff_tn=512, ff_tk=2816)


def _params(*sem):
    return pltpu.CompilerParams(dimension_semantics=sem, vmem_limit_bytes=VMEM_LIMIT_BYTES)


def _mod_row(i, cfg, tm):
    nct = cfg.tok_ctx // tm
    return jnp.where(i < nct, 0, 1 + (i - nct) // (cfg.dec_seq // tm))


def _silu(x):
    return x * jax.nn.sigmoid(x)


def _softplus(x):
    return jnp.maximum(x, 0.0) + jnp.log1p(jnp.exp(-jnp.abs(x)))


def _dot(a, b):
    return jnp.dot(a, b, preferred_element_type=F32)


def _dotb(a, b):
    return _dot(a.astype(BF16), b.astype(BF16))


def _dot_nt(a, b):
    return lax.dot_general(a, b, (((1,), (1,)), ((), ())), preferred_element_type=F32)


def _dot_tn(a, b):
    return lax.dot_general(a, b, (((0,), (0,)), ((), ())), preferred_element_type=F32)


def _ada_body(c_ref, w_ref, b_ref, o_ref):
    s = _silu(c_ref[...]).astype(BF16)
    o_ref[...] = _dot(s, w_ref[...].astype(BF16)) + b_ref[...]


def _ada(cvec, w_ada, b_ada):
    depth, d, n = w_ada.shape
    tn = 512
    return pl.pallas_call(
        _ada_body,
        grid=(depth, n // tn),
        in_specs=[pl.BlockSpec((MOD_ROWS, d), lambda l, j: (0, 0)),
                  pl.BlockSpec((None, d, tn), lambda l, j: (l, 0, j)),
                  pl.BlockSpec((None, 1, tn), lambda l, j: (l, 0, j))],
        out_specs=pl.BlockSpec((None, MOD_ROWS, tn), lambda l, j: (l, 0, j)),
        out_shape=jax.ShapeDtypeStruct((depth, MOD_ROWS, n), F32),
        compiler_params=_params("arbitrary", "arbitrary"),
        name="ada",
    )(cvec, w_ada, b_ada.reshape(depth, 1, n))


def _norm_mod_body(x_ref, g_ref, sh_ref, sc_ref, o_ref):
    x = x_ref[...]
    y = x * lax.rsqrt(jnp.mean(x * x, axis=-1, keepdims=True) + EPS) * g_ref[...]
    o_ref[...] = (y * (1.0 + sc_ref[...]) + sh_ref[...]).astype(o_ref.dtype)


def _norm_mod(x, g, mod3, sh_idx, sc_idx, cfg):
    tok, d = x.shape
    tm = cfg.seq
    spec_mod = lambda idx: pl.BlockSpec((None, 1, d), lambda i: (_mod_row(i, cfg, tm) * 6 + idx, 0, 0))
    return pl.pallas_call(
        _norm_mod_body,
        grid=(tok // tm,),
        in_specs=[pl.BlockSpec((tm, d), lambda i: (i, 0)),
                  pl.BlockSpec((1, d), lambda i: (0, 0)),
                  spec_mod(sh_idx), spec_mod(sc_idx)],
        out_specs=pl.BlockSpec((tm, d), lambda i: (i, 0)),
        out_shape=jax.ShapeDtypeStruct((tok, d), BF16),
        compiler_params=_params("arbitrary"),
        name="norm_mod",
    )(x, g.reshape(1, d), mod3, mod3)


def _final_norm_body(x_ref, g_ref, o_ref):
    x = x_ref[...]
    o_ref[...] = x * lax.rsqrt(jnp.mean(x * x, axis=-1, keepdims=True) + EPS) * g_ref[...]


def _final_norm(x, g, row0, rows, cfg):
    d = x.shape[1]
    tm = cfg.seq
    off = row0 // tm
    return pl.pallas_call(
        _final_norm_body,
        grid=(rows // tm,),
        in_specs=[pl.BlockSpec((tm, d), lambda i: (i + off, 0)),
                  pl.BlockSpec((1, d), lambda i: (0, 0))],
        out_specs=pl.BlockSpec((tm, d), lambda i: (i, 0)),
        out_shape=jax.ShapeDtypeStruct((rows, d), F32),
        compiler_params=_params("arbitrary"),
        name="final_norm",
    )(x, g.reshape(1, d))


def _mm_body(a_ref, w_ref, o_ref):
    o_ref[...] = _dot(a_ref[...], w_ref[...]).astype(o_ref.dtype)


def _mm(a, w, out_dtype, cfg, name):
    tok, k = a.shape
    n = w.shape[1]
    tm, tn = cfg.tm, min(cfg.tn, n)
    return pl.pallas_call(
        _mm_body,
        grid=(tok // tm, n // tn),
        in_specs=[pl.BlockSpec((tm, k), lambda i, j: (i, 0)),
                  pl.BlockSpec((k, tn), lambda i, j: (0, j))],
        out_specs=pl.BlockSpec((tm, tn), lambda i, j: (i, j)),
        out_shape=jax.ShapeDtypeStruct((tok, n), out_dtype),
        compiler_params=_params("arbitrary", "arbitrary"),
        name=name,
    )(a, w)


def _merge_body(o_ref, y_ref, wa_ref, wb_ref, ga_ref, gb_ref, out_ref):
    pa = _dot(o_ref[...], wa_ref[...])
    pb = _dot(y_ref[...], wb_ref[...])
    out_ref[...] = (jax.nn.sigmoid(ga_ref[...]) * pa + jax.nn.sigmoid(gb_ref[...]) * pb).astype(out_ref.dtype)


def _merge(o, y, w_pa, w_pb, p2, gate_col0, cfg):
    tok = o.shape[0]
    d = w_pa.shape[1]
    tm, tn = cfg.tm, min(cfg.tn // 2, d)
    ga0 = gate_col0 // tn
    gb0 = (gate_col0 + d) // tn
    return pl.pallas_call(
        _merge_body,
        grid=(tok // tm, d // tn),
        in_specs=[pl.BlockSpec((tm, o.shape[1]), lambda i, j: (i, 0)),
                  pl.BlockSpec((tm, y.shape[1]), lambda i, j: (i, 0)),
                  pl.BlockSpec((w_pa.shape[0], tn), lambda i, j: (0, j)),
                  pl.BlockSpec((w_pb.shape[0], tn), lambda i, j: (0, j)),
                  pl.BlockSpec((tm, tn), lambda i, j: (i, ga0 + j)),
                  pl.BlockSpec((tm, tn), lambda i, j: (i, gb0 + j))],
        out_specs=pl.BlockSpec((tm, tn), lambda i, j: (i, j)),
        out_shape=jax.ShapeDtypeStruct((tok, d), BF16),
        compiler_params=_params("arbitrary", "arbitrary"),
        name="merge",
    )(o, y, w_pa, w_pb, p2, p2)


def _resid_body(a_ref, w_ref, x_ref, gt_ref, o_ref, *, nk):
    p = _dot(a_ref[...], w_ref[...])
    if nk == 1:
        o_ref[...] = x_ref[...] + gt_ref[...] * p
        return
    k = pl.program_id(2)

    @pl.when(k == 0)
    def _():
        o_ref[...] = p

    @pl.when(jnp.logical_and(k > 0, k < nk - 1))
    def _():
        o_ref[...] += p

    @pl.when(k == nk - 1)
    def _():
        o_ref[...] = x_ref[...] + gt_ref[...] * (o_ref[...] + p)


def _resid(a, w, x, mod3, gt_idx, tk, cfg, name):
    tok, kdim = a.shape
    d = w.shape[1]
    tm, tn = cfg.tm, min(cfg.tn, d)
    nk = kdim // tk
    return pl.pallas_call(
        functools.partial(_resid_body, nk=nk),
        grid=(tok // tm, d // tn, nk),
        in_specs=[pl.BlockSpec((tm, tk), lambda i, j, k: (i, k)),
                  pl.BlockSpec((tk, tn), lambda i, j, k: (k, j)),
                  pl.BlockSpec((tm, tn), lambda i, j, k: (i, j)),
                  pl.BlockSpec((None, 1, tn), lambda i, j, k: (_mod_row(i, cfg, tm) * 6 + gt_idx, 0, j))],
        out_specs=pl.BlockSpec((tm, tn), lambda i, j, k: (i, j)),
        out_shape=jax.ShapeDtypeStruct((tok, d), F32),
        compiler_params=_params("arbitrary", "arbitrary", "arbitrary"),
        name=name,
    )(a, w, x, mod3)


def _ffn_up_body(a_ref, wg_ref, wv_ref, cw_ref, o_ref, *, cfg):
    tm, tn = o_ref.shape
    i = pl.program_id(0)
    a = a_ref[...]
    g = _dot(a, wg_ref[...])
    v = _dot(a, wv_ref[...])
    w = cw_ref[...]
    nct = cfg.tok_ctx // tm

    @pl.when(i < nct)
    def _():
        t = lax.broadcasted_iota(jnp.int32, (tm, tn), 0) & (cfg.seq - 1)
        gm = jnp.where(t == 0, 0.0, pltpu.roll(g, 1, 0))
        gp = jnp.where(t == cfg.seq - 1, 0.0, pltpu.roll(g, tm - 1, 0))
        y = gm * w[0:1] + g * w[1:2] + gp * w[2:3]
        o_ref[...] = (_silu(y) * v).astype(o_ref.dtype)

    @pl.when(i >= nct)
    def _():
        gw = cfg.grid_w
        z = jnp.zeros((gw, tn), F32)
        gm = jnp.concatenate([z, g[:tm - gw]], axis=0)
        gp = jnp.concatenate([g[gw:], z], axis=0)
        y = gm * w[0:1] + g * w[1:2] + gp * w[2:3]
        o_ref[...] = (_silu(y) * v).astype(o_ref.dtype)


def _ffn_up(a, wg, wv, cw, cfg):
    tok, d = a.shape
    n = wg.shape[1]
    tm, tn = cfg.tm, cfg.ff_tn
    assert tm == cfg.dec_seq and tm % cfg.seq == 0
    return pl.pallas_call(
        functools.partial(_ffn_up_body, cfg=cfg),
        grid=(tok // tm, n // tn),
        in_specs=[pl.BlockSpec((tm, d), lambda i, j: (i, 0)),
                  pl.BlockSpec((d, tn), lambda i, j: (0, j)),
                  pl.BlockSpec((d, tn), lambda i, j: (0, j)),
                  pl.BlockSpec((3, tn), lambda i, j: (0, j))],
        out_specs=pl.BlockSpec((tm, tn), lambda i, j: (i, j)),
        out_shape=jax.ShapeDtypeStruct((tok, n), BF16),
        compiler_params=_params("arbitrary", "arbitrary"),
        name="ffn_up",
    )(a, wg, wv, cw)


def _sc_body(xs_ref, bs_ref, cs_ref, w_ref, o_ref, *, cfg):
    tm, tn = o_ref.shape
    i = pl.program_id(0)
    per = jnp.where(i < cfg.tok_ctx // tm, cfg.seq, cfg.grid_w)
    u = cs_ref[...] * xs_ref[...]
    t = lax.broadcasted_iota(jnp.int32, (tm, tn), 0) & (per - 1)
    um = jnp.where(t == 0, 0.0, pltpu.roll(u, 1, 0))
    up = jnp.where(t == per - 1, 0.0, pltpu.roll(u, tm - 1, 0))
    w = w_ref[...]
    o_ref[...] = (bs_ref[...] * (um * w[0:1] + u * w[1:2] + up * w[2:3])).astype(o_ref.dtype)


def _sc_mixer(p2, cw, cfg):
    tok = p2.shape[0]
    ws = cfg.w_sc
    tm, tn = cfg.tm, min(512, ws)
    nb = ws // tn
    return pl.pallas_call(
        functools.partial(_sc_body, cfg=cfg),
        grid=(tok // tm, nb),
        in_specs=[pl.BlockSpec((tm, tn), lambda i, j: (i, j)),
                  pl.BlockSpec((tm, tn), lambda i, j: (i, nb + j)),
                  pl.BlockSpec((tm, tn), lambda i, j: (i, 2 * nb + j)),
                  pl.BlockSpec((3, tn), lambda i, j: (0, j))],
        out_specs=pl.BlockSpec((tm, tn), lambda i, j: (i, j)),
        out_shape=jax.ShapeDtypeStruct((tok, ws), BF16),
        compiler_params=_params("arbitrary", "arbitrary"),
        name="sc_mixer",
    )(p2, p2, p2, cw)


def _dn_body(*refs, n, per, heads, has_s0, emit_state, group):
    it = iter(refs)
    q_ref, k_ref, v_ref, z_ref = next(it), next(it), next(it), next(it)
    wq_ref, wk_ref, wv_ref = next(it), next(it), next(it)
    bgc_ref, bgr_ref, alog_ref, dtb_ref, og_ref = next(it), next(it), next(it), next(it), next(it)
    s0_ref = next(it) if has_s0 else None
    o_ref = next(it)
    st_ref = next(it) if emit_state else None
    qs, ks, vs, of_s, ob_s, wt_s, n_s, qt_s, oc_s, eg_s = it

    nc = n // CHUNK
    h = pl.program_id(1)

    tok = lax.broadcasted_iota(jnp.int32, (n, LANES), 0) & (per - 1)
    first, last = tok == 0, tok == per - 1

    def conv_silu(x_ref, w_ref):
        x, w = x_ref[...], w_ref[...]
        xm = jnp.where(first, 0.0, pltpu.roll(x, 1, 0))
        xp = jnp.where(last, 0.0, pltpu.roll(x, n - 1, 0))
        return _silu(xm * w[0:1] + x * w[1:2] + xp * w[2:3])

    q = conv_silu(q_ref, wq_ref)
    k = conv_silu(k_ref, wk_ref)
    qs[...] = q * lax.rsqrt(jnp.sum(q * q, axis=-1, keepdims=True) + EPS) * (LANES ** -0.5)
    ks[...] = k * lax.rsqrt(jnp.sum(k * k, axis=-1, keepdims=True) + EPS)
    vs[...] = conv_silu(v_ref, wv_ref)

    head_sel = lax.broadcasted_iota(jnp.int32, (2, heads), 1) == h
    ea = jnp.exp(jnp.sum(jnp.where(head_sel, alog_ref[...], 0.0), axis=1, keepdims=True))
    dtb = jnp.sum(jnp.where(head_sel, dtb_ref[...], 0.0), axis=1, keepdims=True)

    nbg = bgc_ref.shape[-1]
    lane_bg = lax.broadcasted_iota(jnp.int32, (CHUNK, nbg), 1)
    sub = lax.broadcasted_iota(jnp.int32, (CHUNK, CHUNK), 0)
    lane = lax.broadcasted_iota(jnp.int32, (CHUNK, CHUNK), 1)
    eye = (sub == lane).astype(F32)
    blk16 = (sub // 16) == (lane // 16)
    blk32 = (sub // 32) == (lane // 32)
    off32 = jnp.logical_and(blk32, jnp.logical_not(blk16))

    def stage(c):
        r0 = c * CHUNK
        if not isinstance(r0, int):
            r0 = pl.multiple_of(r0, CHUNK)
        qc, kc, vc = qs[pl.ds(r0, CHUNK), :], ks[pl.ds(r0, CHUNK), :], vs[pl.ds(r0, CHUNK), :]
        kb = kc.astype(BF16)
        kk = _dot_nt(kb, kb)
        qk = _dot_nt(qc.astype(BF16), kb)
        bgc = bgc_ref[c]
        for d in range(2):
            col = lambda j: jnp.sum(jnp.where(lane_bg == j * heads + h, bgc, 0.0), axis=1, keepdims=True)
            beta = jax.nn.sigmoid(col(d))
            g_col = -ea[d:d + 1] * _softplus(col(2 + d) + dtb[d:d + 1])
            g_row = -ea[d:d + 1] * _softplus(bgr_ref[2 + d, c] + dtb[d:d + 1])
            m = (lane <= sub) if d == 0 else (lane >= sub)
            mt = (sub <= lane) if d == 0 else (sub >= lane)
            ms = (lane < sub) if d == 0 else (lane > sub)
            gc_col = jnp.sum(jnp.where(m, jnp.broadcast_to(g_row, (CHUNK, CHUNK)), 0.0), axis=1, keepdims=True)
            gc_row = jnp.sum(jnp.where(mt, jnp.broadcast_to(g_col, (CHUNK, CHUNK)), 0.0), axis=0, keepdims=True)
            g_last = jnp.sum(g_row, axis=1, keepdims=True)
            decay = jnp.where(m, jnp.exp(jnp.where(m, gc_col - gc_row, 0.0)), 0.0)
            a = jnp.where(ms, kk * decay * beta, 0.0)
            e_gc = jnp.exp(gc_col)
            x = jnp.concatenate([vc * beta, kc * (beta * e_gc)], axis=1)
            a_d = jnp.where(blk16, a, 0.0)
            a_o32 = jnp.where(off32, a, 0.0)
            a_o64 = jnp.where(blk32, 0.0, a)
            p2 = _dotb(a_d, a_d)
            p4 = _dotb(p2, p2)
            p8 = _dotb(p4, p4)
            t = eye - a_d
            t = t + _dotb(t, p2)
            t = t + _dotb(t, p4)
            t = t + _dotb(t, p8)
            t = t - _dotb(_dotb(t, a_o32), t)
            t = t - _dotb(_dotb(t, a_o64), t)
            xb = _dotb(t, x).astype(BF16)
            intra = jnp.where(m, qk * decay, 0.0)
            iw = _dot(intra.astype(BF16), xb)
            k_dec = kc * jnp.exp(g_last - gc_col)
            kt = _dot_tn(k_dec.astype(BF16), xb)
            oc_s[d, c] = iw[:, :LANES]
            qt_s[d, c] = (qc * e_gc - iw[:, LANES:]).astype(BF16)
            n_s[d, c] = kt[:, :LANES]
            wt_s[d, c] = kt[:, LANES:].astype(BF16)
            eg_s[d, c] = jnp.broadcast_to(jnp.exp(g_last), (1, LANES))

    def stage_group(gi, carry):
        for j in range(group):
            stage(gi * group + j)
        return carry

    if nc == group:
        stage_group(0, 0)
    else:
        lax.fori_loop(0, nc // group, stage_group, 0)

    if has_s0:
        s_init = (s0_ref[0], s0_ref[1])
    else:
        s_init = (jnp.zeros((LANES, LANES), F32), jnp.zeros((LANES, LANES), F32))

    def scan_step(t, carry):
        out = []
        for d, (s, acc) in enumerate(zip(carry, (of_s, ob_s))):
            c = t if d == 0 else nc - 1 - t
            r0 = c * CHUNK
            if not isinstance(r0, int):
                r0 = pl.multiple_of(r0, CHUNK)
            sb = s.astype(BF16)
            acc[pl.ds(r0, CHUNK), :] = _dot(qt_s[d, c], sb) + oc_s[d, c]
            out.append(s * eg_s[d, c] - _dot(wt_s[d, c], sb) + n_s[d, c])
        return tuple(out)

    if nc == group:
        s_fin = s_init
        for t in range(nc):
            s_fin = scan_step(t, s_fin)
    else:
        s_fin = lax.fori_loop(0, nc, scan_step, s_init)

    o = of_s[...] + ob_s[...]
    y = o * lax.rsqrt(jnp.mean(o * o, axis=-1, keepdims=True) + EPS) * og_ref[...]
    o_ref[...] = (y * _silu(z_ref[...])).astype(o_ref.dtype)
    if emit_state:
        st_ref[0] = s_fin[0]
        st_ref[1] = s_fin[1]


def _deltanet(p1, bg, conv_qkv, a_log, dt_bias, onorm_g, s0, cfg, *, latent):
    hh = cfg.heads
    tok, nbg = bg.shape
    n = cfg.dec_seq if latent else cfg.seq
    nseq = cfg.dec_batch if latent else cfg.batch
    blk0 = cfg.tok_ctx // n if latent else 0
    per = cfg.grid_w if latent else cfg.seq
    nc = n // CHUNK
    group = min(4, nc)
    bgc = bg.reshape(tok // CHUNK, CHUNK, nbg)
    bgr = bg.T.reshape(4, hh, tok // CHUNK, 1, CHUNK)

    col = lambda g: pl.BlockSpec((n, LANES), lambda s, h: (blk0 + s, g * hh + h))
    cw = lambda g: pl.BlockSpec((3, LANES), lambda s, h: (0, g * hh + h))
    in_specs = [col(0), col(1), col(2), col(3), cw(0), cw(1), cw(2),
                pl.BlockSpec((nc, CHUNK, nbg), lambda s, h: (blk0 + s, 0, 0)),
                pl.BlockSpec((4, None, nc, 1, CHUNK), lambda s, h: (0, h, blk0 + s, 0, 0)),
                pl.BlockSpec((2, hh), lambda s, h: (0, 0)),
                pl.BlockSpec((2, hh), lambda s, h: (0, 0)),
                pl.BlockSpec((1, LANES), lambda s, h: (0, 0))]
    args = [p1, p1, p1, p1, conv_qkv, conv_qkv, conv_qkv, bgc, bgr, a_log, dt_bias, onorm_g.reshape(1, LANES)]
    st_spec = pl.BlockSpec((None, 2, None, LANES, LANES), lambda s, h: (s, 0, h, 0, 0))
    if latent:
        in_specs.append(st_spec)
        args.append(s0)
    out_specs = [pl.BlockSpec((n, LANES), lambda s, h: (s, h))]
    out_shape = [jax.ShapeDtypeStruct((nseq * n, hh * LANES), BF16)]
    if not latent:
        out_specs.append(st_spec)
        out_shape.append(jax.ShapeDtypeStruct((nseq, 2, hh, LANES, LANES), F32))
    scratch = [pltpu.VMEM((n, LANES), F32)] * 5 + [
        pltpu.VMEM((2, nc, LANES, LANES), BF16), pltpu.VMEM((2, nc, LANES, LANES), F32),
        pltpu.VMEM((2, nc, CHUNK, LANES), BF16), pltpu.VMEM((2, nc, CHUNK, LANES), F32),
        pltpu.VMEM((2, nc, 1, LANES), F32)]
    return pl.pallas_call(
        functools.partial(_dn_body, n=n, per=per, heads=hh, has_s0=latent, emit_state=not latent, group=group),
        grid=(nseq, hh),
        in_specs=in_specs,
        out_specs=out_specs,
        out_shape=out_shape,
        scratch_shapes=scratch,
        compiler_params=_params("arbitrary", "arbitrary"),
        name="deltanet_lat" if latent else "deltanet_ctx",
    )(*args)


def _forward(cfg, x_prompt, x_sample, state_dn, c, c_ctx, norm1_g, norm2_g, w_ada, b_ada, w_in,
             conv_qkv, a_log, dt_bias, onorm_g, conv_sc, w_pa, w_pb, w_o, w_up, conv_ff,
             w_down, final_g):
    d, hh = cfg.d, cfg.heads
    w1_n = 4 * cfg.w_hd
    nbg = 4 * hh
    x = jnp.concatenate([x_prompt.reshape(cfg.tok_ctx, d), x_sample.reshape(-1, d)], axis=0)
    cvec = jnp.zeros((MOD_ROWS, d), F32).at[0].set(c_ctx).at[1:1 + cfg.dec_batch].set(c)
    mod = _ada(cvec, w_ada, b_ada)
    ff_extra = cfg.ff_pad - cfg.d_ff
    states = []
    for l in range(cfg.depth):
        mod3 = mod[l].reshape(MOD_ROWS * 6, 1, d)
        w1 = w_in[l][:, :w1_n].astype(BF16)
        wbg = w_in[l][:, w1_n:w1_n + nbg].astype(BF16)
        w2 = w_in[l][:, w1_n + nbg:].astype(BF16)
        h = _norm_mod(x, norm1_g[l], mod3, 0, 1, cfg)
        p1 = _mm(h, w1, F32, cfg, "proj_qkvz")
        bg = _mm(h, wbg, F32, cfg, "proj_bg")
        p2 = _mm(h, w2, F32, cfg, "proj_sc_gates")
        o_ctx, st = _deltanet(p1, bg, conv_qkv[l], a_log[l], dt_bias[l], onorm_g[l], None, cfg, latent=False)
        o_lat, = _deltanet(p1, bg, conv_qkv[l], a_log[l], dt_bias[l], onorm_g[l], state_dn[:, l], cfg, latent=True)
        states.append(st)
        o = jnp.concatenate([o_ctx, o_lat], axis=0)
        y_sc = _sc_mixer(p2, conv_sc[l], cfg)
        merged = _merge(o, y_sc, w_pa[l].astype(BF16), w_pb[l].astype(BF16), p2, 3 * cfg.w_sc, cfg)
        x = _resid(merged, w_o[l].astype(BF16), x, mod3, 2, d, cfg, "out_proj")
        h = _norm_mod(x, norm2_g[l], mod3, 3, 4, cfg)
        pad_c = lambda w: jnp.pad(w.astype(BF16), ((0, 0), (0, ff_extra)))
        wg, wv = pad_c(w_up[l][:, :cfg.d_ff]), pad_c(w_up[l][:, cfg.d_ff:])
        f = _ffn_up(h, wg, wv, jnp.pad(conv_ff[l], ((0, 0), (0, ff_extra))), cfg)
        wd = jnp.pad(w_down[l].astype(BF16), ((0, ff_extra), (0, 0)))
        x = _resid(f, wd, x, mod3, 5, cfg.ff_tk, cfg, "ffn_down")
    y_prompt = _final_norm(x, final_g, 0, cfg.tok_ctx, cfg).reshape(x_prompt.shape)
    y_sample = _final_norm(x, final_g, cfg.tok_ctx, cfg.tok - cfg.tok_ctx, cfg).reshape(x_sample.shape)
    return y_prompt, y_sample, jnp.stack(states, axis=1)


def kernel(x_prompt, x_sample, state_dn, c, c_ctx, norm1_g, norm2_g, w_ada, b_ada, w_in, conv_qkv, a_log, dt_bias, onorm_g, conv_sc, w_pa, w_pb, w_o, w_up, conv_ff, w_down, final_g):
    return _forward(CFG, x_prompt, x_sample, state_dn, c, c_ctx, norm1_g, norm2_g, w_ada, b_ada, w_in,
                    conv_qkv, a_log, dt_bias, onorm_g, conv_sc, w_pa, w_pb, w_o, w_up, conv_ff,
                    w_down, final_g)
```

```python
import functools
from typing import NamedTuple

import jax
import jax.numpy as jnp
from jax import lax
from jax.experimental import pallas as pl
from jax.experimental.pallas import tpu as pltpu

F32 = jnp.float32
BF16 = jnp.bfloat16
EPS = 1e-6
CHUNK = 64
LANES = 128
MOD_ROWS = 8
V7X_VMEM_BYTES = 64 * 1024 * 1024
VMEM_LIMIT_BYTES = V7X_VMEM_BYTES - 8 * 1024 * 1024


class Cfg(NamedTuple):
    d: int
    batch: int
    seq: int
    depth: int
    dec_batch: int
    dec_seq: int
    grid_w: int
    heads: int
    w_sc: int
    d_ff: int
    tm: int
    tn: int
    ff_tn: int
    ff_tk: int
    dn_heads: int

    @property
    def w_hd(self):
        return self.heads * LANES

    @property
    def tok_ctx(self):
        return self.batch * self.seq

    @property
    def tok(self):
        return self.batch * self.seq + self.dec_batch * self.dec_seq

    @property
    def ff_pad(self):
        return -(-self.d_ff // self.ff_tn) * self.ff_tn


CFG = Cfg(d=4096, batch=32, seq=256, depth=2, dec_batch=2, dec_seq=1024, grid_w=64, heads=16,
          w_sc=2048, d_ff=11008, tm=1024, tn=1024, ff_tn=512, ff_tk=2816, dn_heads=4)


def _params(*sem):
    return pltpu.CompilerParams(dimension_semantics=sem, vmem_limit_bytes=VMEM_LIMIT_BYTES)


def _mod_row(i, cfg, tm):
    nct = cfg.tok_ctx // tm
    return jnp.where(i < nct, 0, 1 + (i - nct) // (cfg.dec_seq // tm))


def _silu(x):
    return x * jax.nn.sigmoid(x)


def _softplus(x):
    return jnp.maximum(x, 0.0) + jnp.log1p(jnp.exp(-jnp.abs(x)))


def _dot(a, b):
    return jnp.dot(a, b, preferred_element_type=F32)


def _dotb(a, b):
    return _dot(a.astype(BF16), b.astype(BF16))


def _dot_nt(a, b):
    return lax.dot_general(a, b, (((1,), (1,)), ((), ())), preferred_element_type=F32)


def _dot_tn(a, b):
    return lax.dot_general(a, b, (((0,), (0,)), ((), ())), preferred_element_type=F32)


def _ada_body(c_ref, w_ref, b_ref, o_ref):
    s = _silu(c_ref[...]).astype(BF16)
    o_ref[...] = _dot(s, w_ref[...].astype(BF16)) + b_ref[...]


def _ada(cvec, w_ada, b_ada):
    depth, d, n = w_ada.shape
    tn = 512
    return pl.pallas_call(
        _ada_body,
        grid=(depth, n // tn),
        in_specs=[pl.BlockSpec((MOD_ROWS, d), lambda l, j: (0, 0)),
                  pl.BlockSpec((None, d, tn), lambda l, j: (l, 0, j)),
                  pl.BlockSpec((None, 1, tn), lambda l, j: (l, 0, j))],
        out_specs=pl.BlockSpec((None, MOD_ROWS, tn), lambda l, j: (l, 0, j)),
        out_shape=jax.ShapeDtypeStruct((depth, MOD_ROWS, n), F32),
        compiler_params=_params("arbitrary", "arbitrary"),
        name="ada",
    )(cvec, w_ada, b_ada.reshape(depth, 1, n))


def _norm_mod_body(x_ref, g_ref, sh_ref, sc_ref, o_ref):
    x = x_ref[...]
    y = x * lax.rsqrt(jnp.mean(x * x, axis=-1, keepdims=True) + EPS) * g_ref[...]
    o_ref[...] = (y * (1.0 + sc_ref[...]) + sh_ref[...]).astype(o_ref.dtype)


def _norm_mod(x, g, mod3, sh_idx, sc_idx, cfg):
    tok, d = x.shape
    tm = cfg.seq
    spec_mod = lambda idx: pl.BlockSpec((None, 1, d), lambda i: (_mod_row(i, cfg, tm) * 6 + idx, 0, 0))
    return pl.pallas_call(
        _norm_mod_body,
        grid=(tok // tm,),
        in_specs=[pl.BlockSpec((tm, d), lambda i: (i, 0)),
                  pl.BlockSpec((1, d), lambda i: (0, 0)),
                  spec_mod(sh_idx), spec_mod(sc_idx)],
        out_specs=pl.BlockSpec((tm, d), lambda i: (i, 0)),
        out_shape=jax.ShapeDtypeStruct((tok, d), BF16),
        compiler_params=_params("arbitrary"),
        name="norm_mod",
    )(x, g.reshape(1, d), mod3, mod3)


def _final_norm_body(x_ref, g_ref, o_ref):
    x = x_ref[...]
    o_ref[...] = x * lax.rsqrt(jnp.mean(x * x, axis=-1, keepdims=True) + EPS) * g_ref[...]


def _final_norm(x, g, row0, rows, cfg):
    d = x.shape[1]
    tm = cfg.seq
    off = row0 // tm
    return pl.pallas_call(
        _final_norm_body,
        grid=(rows // tm,),
        in_specs=[pl.BlockSpec((tm, d), lambda i: (i + off, 0)),
                  pl.BlockSpec((1, d), lambda i: (0, 0))],
        out_specs=pl.BlockSpec((tm, d), lambda i: (i, 0)),
        out_shape=jax.ShapeDtypeStruct((rows, d), F32),
        compiler_params=_params("arbitrary"),
        name="final_norm",
    )(x, g.reshape(1, d))


def _mm_body(a_ref, w_ref, o_ref):
    o_ref[...] = _dot(a_ref[...], w_ref[...]).astype(o_ref.dtype)


def _mm(a, w, out_dtype, cfg, name):
    tok, k = a.shape
    n = w.shape[1]
    tm, tn = cfg.tm, min(cfg.tn, n)
    return pl.pallas_call(
        _mm_body,
        grid=(tok // tm, n // tn),
        in_specs=[pl.BlockSpec((tm, k), lambda i, j: (i, 0)),
                  pl.BlockSpec((k, tn), lambda i, j: (0, j))],
        out_specs=pl.BlockSpec((tm, tn), lambda i, j: (i, j)),
        out_shape=jax.ShapeDtypeStruct((tok, n), out_dtype),
        compiler_params=_params("arbitrary", "arbitrary"),
        name=name,
    )(a, w)


def _merge_body(o_ref, y_ref, wa_ref, wb_ref, ga_ref, gb_ref, out_ref):
    pa = _dot(o_ref[...], wa_ref[...])
    pb = _dot(y_ref[...], wb_ref[...])
    out_ref[...] = (jax.nn.sigmoid(ga_ref[...]) * pa + jax.nn.sigmoid(gb_ref[...]) * pb).astype(out_ref.dtype)


def _merge(o, y, w_pa, w_pb, p2, gate_col0, cfg):
    tok = o.shape[0]
    d = w_pa.shape[1]
    tm, tn = cfg.tm, min(cfg.tn // 2, d)
    ga0 = gate_col0 // tn
    gb0 = (gate_col0 + d) // tn
    return pl.pallas_call(
        _merge_body,
        grid=(tok // tm, d // tn),
        in_specs=[pl.BlockSpec((tm, o.shape[1]), lambda i, j: (i, 0)),
                  pl.BlockSpec((tm, y.shape[1]), lambda i, j: (i, 0)),
                  pl.BlockSpec((w_pa.shape[0], tn), lambda i, j: (0, j)),
                  pl.BlockSpec((w_pb.shape[0], tn), lambda i, j: (0, j)),
                  pl.BlockSpec((tm, tn), lambda i, j: (i, ga0 + j)),
                  pl.BlockSpec((tm, tn), lambda i, j: (i, gb0 + j))],
        out_specs=pl.BlockSpec((tm, tn), lambda i, j: (i, j)),
        out_shape=jax.ShapeDtypeStruct((tok, d), BF16),
        compiler_params=_params("arbitrary", "arbitrary"),
        name="merge",
    )(o, y, w_pa, w_pb, p2, p2)


def _resid_body(a_ref, w_ref, x_ref, gt_ref, o_ref, *, nk):
    p = _dot(a_ref[...], w_ref[...])
    if nk == 1:
        o_ref[...] = x_ref[...] + gt_ref[...] * p
        return
    k = pl.program_id(2)

    @pl.when(k == 0)
    def _():
        o_ref[...] = p

    @pl.when(jnp.logical_and(k > 0, k < nk - 1))
    def _():
        o_ref[...] += p

    @pl.when(k == nk - 1)
    def _():
        o_ref[...] = x_ref[...] + gt_ref[...] * (o_ref[...] + p)


def _resid(a, w, x, mod3, gt_idx, tk, cfg, name):
    tok, kdim = a.shape
    d = w.shape[1]
    tm, tn = cfg.tm, min(cfg.tn, d)
    nk = kdim // tk
    return pl.pallas_call(
        functools.partial(_resid_body, nk=nk),
        grid=(tok // tm, d // tn, nk),
        in_specs=[pl.BlockSpec((tm, tk), lambda i, j, k: (i, k)),
                  pl.BlockSpec((tk, tn), lambda i, j, k: (k, j)),
                  pl.BlockSpec((tm, tn), lambda i, j, k: (i, j)),
                  pl.BlockSpec((None, 1, tn), lambda i, j, k: (_mod_row(i, cfg, tm) * 6 + gt_idx, 0, j))],
        out_specs=pl.BlockSpec((tm, tn), lambda i, j, k: (i, j)),
        out_shape=jax.ShapeDtypeStruct((tok, d), F32),
        compiler_params=_params("arbitrary", "arbitrary", "arbitrary"),
        name=name,
    )(a, w, x, mod3)


def _ffn_up_body(a_ref, wg_ref, wv_ref, cw_ref, o_ref, *, cfg):
    tm, tn = o_ref.shape
    i = pl.program_id(0)
    a = a_ref[...]
    g = _dot(a, wg_ref[...])
    v = _dot(a, wv_ref[...])
    w = cw_ref[...]
    nct = cfg.tok_ctx // tm

    @pl.when(i < nct)
    def _():
        t = lax.broadcasted_iota(jnp.int32, (tm, tn), 0) & (cfg.seq - 1)
        gm = jnp.where(t == 0, 0.0, pltpu.roll(g, 1, 0))
        gp = jnp.where(t == cfg.seq - 1, 0.0, pltpu.roll(g, tm - 1, 0))
        y = gm * w[0:1] + g * w[1:2] + gp * w[2:3]
        o_ref[...] = (_silu(y) * v).astype(o_ref.dtype)

    @pl.when(i >= nct)
    def _():
        gw = cfg.grid_w
        z = jnp.zeros((gw, tn), F32)
        gm = jnp.concatenate([z, g[:tm - gw]], axis=0)
        gp = jnp.concatenate([g[gw:], z], axis=0)
        y = gm * w[0:1] + g * w[1:2] + gp * w[2:3]
        o_ref[...] = (_silu(y) * v).astype(o_ref.dtype)


def _ffn_up(a, wg, wv, cw, cfg):
    tok, d = a.shape
    n = wg.shape[1]
    tm, tn = cfg.tm, cfg.ff_tn
    assert tm == cfg.dec_seq and tm % cfg.seq == 0
    return pl.pallas_call(
        functools.partial(_ffn_up_body, cfg=cfg),
        grid=(tok // tm, n // tn),
        in_specs=[pl.BlockSpec((tm, d), lambda i, j: (i, 0)),
                  pl.BlockSpec((d, tn), lambda i, j: (0, j)),
                  pl.BlockSpec((d, tn), lambda i, j: (0, j)),
                  pl.BlockSpec((3, tn), lambda i, j: (0, j))],
        out_specs=pl.BlockSpec((tm, tn), lambda i, j: (i, j)),
        out_shape=jax.ShapeDtypeStruct((tok, n), BF16),
        compiler_params=_params("arbitrary", "arbitrary"),
        name="ffn_up",
    )(a, wg, wv, cw)


def _sc_body(xs_ref, bs_ref, cs_ref, w_ref, o_ref, *, cfg):
    tm, tn = o_ref.shape
    i = pl.program_id(0)
    per = jnp.where(i < cfg.tok_ctx // tm, cfg.seq, cfg.grid_w)
    u = cs_ref[...] * xs_ref[...]
    t = lax.broadcasted_iota(jnp.int32, (tm, tn), 0) & (per - 1)
    um = jnp.where(t == 0, 0.0, pltpu.roll(u, 1, 0))
    up = jnp.where(t == per - 1, 0.0, pltpu.roll(u, tm - 1, 0))
    w = w_ref[...]
    o_ref[...] = (bs_ref[...] * (um * w[0:1] + u * w[1:2] + up * w[2:3])).astype(o_ref.dtype)


def _sc_mixer(p2, cw, cfg):
    tok = p2.shape[0]
    ws = cfg.w_sc
    tm, tn = cfg.tm, min(512, ws)
    nb = ws // tn
    return pl.pallas_call(
        functools.partial(_sc_body, cfg=cfg),
        grid=(tok // tm, nb),
        in_specs=[pl.BlockSpec((tm, tn), lambda i, j: (i, j)),
                  pl.BlockSpec((tm, tn), lambda i, j: (i, nb + j)),
                  pl.BlockSpec((tm, tn), lambda i, j: (i, 2 * nb + j)),
                  pl.BlockSpec((3, tn), lambda i, j: (0, j))],
        out_specs=pl.BlockSpec((tm, tn), lambda i, j: (i, j)),
        out_shape=jax.ShapeDtypeStruct((tok, ws), BF16),
        compiler_params=_params("arbitrary", "arbitrary"),
        name="sc_mixer",
    )(p2, p2, p2, cw)


def _dn_body(*refs, n, per, heads, hb, has_s0, emit_state, group):
    it = iter(refs)
    q_ref, k_ref, v_ref, z_ref = next(it), next(it), next(it), next(it)
    wq_ref, wk_ref, wv_ref = next(it), next(it), next(it)
    bgc_ref, bgr_ref, alog_ref, dtb_ref, og_ref = next(it), next(it), next(it), next(it), next(it)
    s0_ref = next(it) if has_s0 else None
    o_ref = next(it)
    st_ref = next(it) if emit_state else None
    qs, ks, vs, of_s, ob_s, qw_s, n_s, oc_s, eg_s = it

    nc = n // CHUNK
    width = hb * LANES
    hblk = pl.program_id(1)
    head_cols = [slice(b * LANES, (b + 1) * LANES) for b in range(hb)]

    tok = lax.broadcasted_iota(jnp.int32, (n, width), 0) & (per - 1)
    first, last = tok == 0, tok == per - 1

    def conv_silu(x_ref, w_ref):
        x, w = x_ref[...], w_ref[...]
        xm = jnp.where(first, 0.0, pltpu.roll(x, 1, 0))
        xp = jnp.where(last, 0.0, pltpu.roll(x, n - 1, 0))
        return _silu(xm * w[0:1] + x * w[1:2] + xp * w[2:3])

    q = conv_silu(q_ref, wq_ref)
    k = conv_silu(k_ref, wk_ref)
    for sl in head_cols:
        qh, kh = q[:, sl], k[:, sl]
        qs[:, sl] = qh * lax.rsqrt(jnp.sum(qh * qh, axis=-1, keepdims=True) + EPS) * (LANES ** -0.5)
        ks[:, sl] = kh * lax.rsqrt(jnp.sum(kh * kh, axis=-1, keepdims=True) + EPS)
    vs[...] = conv_silu(v_ref, wv_ref)

    head_iota = lax.broadcasted_iota(jnp.int32, (2, heads), 1)
    ea, dtb = [], []
    for b in range(hb):
        sel = head_iota == hblk * hb + b
        ea.append(jnp.exp(jnp.sum(jnp.where(sel, alog_ref[...], 0.0), axis=1, keepdims=True)))
        dtb.append(jnp.sum(jnp.where(sel, dtb_ref[...], 0.0), axis=1, keepdims=True))

    nbg = bgc_ref.shape[-1]
    lane_bg = lax.broadcasted_iota(jnp.int32, (CHUNK, nbg), 1)
    sub = lax.broadcasted_iota(jnp.int32, (CHUNK, CHUNK), 0)
    lane = lax.broadcasted_iota(jnp.int32, (CHUNK, CHUNK), 1)
    eye = (sub == lane).astype(F32)
    blk16 = (sub // 16) == (lane // 16)
    blk32 = (sub // 32) == (lane // 32)
    off32 = jnp.logical_and(blk32, jnp.logical_not(blk16))
    vis = (lane <= sub, lane >= sub)
    vis_t = (sub <= lane, sub >= lane)
    strict = (lane < sub, lane > sub)

    def stage_group(gi, carry):
        units = []
        for b in range(hb):
            for j in range(group):
                c = gi * group + j
                r0 = c * CHUNK
                if not isinstance(r0, int):
                    r0 = pl.multiple_of(r0, CHUNK)
                rows = pl.ds(r0, CHUNK)
                qc, kc, vc = qs[rows, head_cols[b]], ks[rows, head_cols[b]], vs[rows, head_cols[b]]
                kb = kc.astype(BF16)
                kk = _dot_nt(kb, kb)
                qk = _dot_nt(qc.astype(BF16), kb)
                bgc = bgc_ref[c]
                hd = hblk * hb + b
                for d in range(2):
                    col = lambda g: jnp.sum(jnp.where(lane_bg == g * heads + hd, bgc, 0.0), axis=1, keepdims=True)
                    beta = jax.nn.sigmoid(col(d))
                    g_col = -ea[b][d:d + 1] * _softplus(col(2 + d) + dtb[b][d:d + 1])
                    g_row = -ea[b][d:d + 1] * _softplus(bgr_ref[2 + d, b, c] + dtb[b][d:d + 1])
                    m = vis[d]
                    gc_col = jnp.sum(jnp.where(m, jnp.broadcast_to(g_row, (CHUNK, CHUNK)), 0.0), axis=1, keepdims=True)
                    gc_row = jnp.sum(jnp.where(vis_t[d], jnp.broadcast_to(g_col, (CHUNK, CHUNK)), 0.0), axis=0, keepdims=True)
                    g_last = jnp.sum(g_row, axis=1, keepdims=True)
                    decay = jnp.where(m, jnp.exp(jnp.where(m, gc_col - gc_row, 0.0)), 0.0)
                    e_gc = jnp.exp(gc_col)
                    units.append(dict(
                        b=b, c=c, d=d,
                        a=jnp.where(strict[d], kk * decay * beta, 0.0),
                        x=jnp.concatenate([vc * beta, kc * (beta * e_gc)], axis=1),
                        intra=jnp.where(m, qk * decay, 0.0).astype(BF16),
                        k_dec=(kc * jnp.exp(g_last - gc_col)).astype(BF16),
                        q_dec=qc * e_gc,
                        eg=jnp.broadcast_to(jnp.exp(g_last), (1, LANES))))
        a_d = [jnp.where(blk16, u["a"], 0.0).astype(BF16) for u in units]
        p2 = [_dot(x, x).astype(BF16) for x in a_d]
        t = [eye - x.astype(F32) for x in a_d]
        t = [x + _dot(x.astype(BF16), p) for x, p in zip(t, p2)]
        p4 = [_dot(p, p).astype(BF16) for p in p2]
        t = [x + _dot(x.astype(BF16), p) for x, p in zip(t, p4)]
        p8 = [_dot(p, p).astype(BF16) for p in p4]
        t = [x + _dot(x.astype(BF16), p) for x, p in zip(t, p8)]
        mm = [_dotb(x, jnp.where(off32, u["a"], 0.0)) for x, u in zip(t, units)]
        t = [x - _dotb(y, x) for x, y in zip(t, mm)]
        mm = [_dotb(x, jnp.where(blk32, 0.0, u["a"])) for x, u in zip(t, units)]
        t = [x - _dotb(y, x) for x, y in zip(t, mm)]
        xb = [_dotb(x, u["x"]).astype(BF16) for x, u in zip(t, units)]
        iw = [_dot(u["intra"], x) for u, x in zip(units, xb)]
        kt = [_dot_tn(u["k_dec"], x) for u, x in zip(units, xb)]
        for u, iwu, ktu in zip(units, iw, kt):
            b, c, d = u["b"], u["c"], u["d"]
            oc_s[b, d, c] = iwu[:, :LANES]
            qw_s[b, d, c, 0:CHUNK, :] = (u["q_dec"] - iwu[:, LANES:]).astype(BF16)
            qw_s[b, d, c, CHUNK:, :] = ktu[:, LANES:].astype(BF16)
            n_s[b, d, c] = ktu[:, :LANES]
            eg_s[b, d, c] = u["eg"]
        return carry

    if nc == group:
        stage_group(0, 0)
    else:
        lax.fori_loop(0, nc // group, stage_group, 0)

    chains = [(b, d) for b in range(hb) for d in range(2)]
    if has_s0:
        s_init = tuple(s0_ref[d, b] for b, d in chains)
    else:
        s_init = tuple(jnp.zeros((LANES, LANES), F32) for _ in chains)

    def scan_step(t, carry):
        res = []
        for (b, d), s in zip(chains, carry):
            c = t if d == 0 else nc - 1 - t
            res.append(_dot(qw_s[b, d, c], s.astype(BF16)))
        out = []
        for (b, d), s, r in zip(chains, carry, res):
            c = t if d == 0 else nc - 1 - t
            r0 = c * CHUNK
            if not isinstance(r0, int):
                r0 = pl.multiple_of(r0, CHUNK)
            acc = of_s if d == 0 else ob_s
            acc[pl.ds(r0, CHUNK), head_cols[b]] = r[:CHUNK] + oc_s[b, d, c]
            out.append(s * eg_s[b, d, c] - r[CHUNK:] + n_s[b, d, c])
        return tuple(out)

    if nc == group:
        s_fin = s_init
        for t in range(nc):
            s_fin = scan_step(t, s_fin)
    else:
        s_fin = lax.fori_loop(0, nc, scan_step, s_init)

    o = of_s[...] + ob_s[...]
    z = _silu(z_ref[...])
    og = og_ref[...]
    for sl in head_cols:
        oh = o[:, sl]
        y = oh * lax.rsqrt(jnp.mean(oh * oh, axis=-1, keepdims=True) + EPS) * og
        o_ref[:, sl] = (y * z[:, sl]).astype(o_ref.dtype)
    if emit_state:
        for (b, d), s in zip(chains, s_fin):
            st_ref[d, b] = s


def _deltanet(p1, bg, conv_qkv, a_log, dt_bias, onorm_g, s0, cfg, *, latent):
    hh, hb = cfg.heads, cfg.dn_heads
    tok, nbg = bg.shape
    n = cfg.dec_seq if latent else cfg.seq
    nseq = cfg.dec_batch if latent else cfg.batch
    blk0 = cfg.tok_ctx // n if latent else 0
    per = cfg.grid_w if latent else cfg.seq
    nc = n // CHUNK
    group = min(4, nc)
    width = hb * LANES
    nhb = hh // hb
    bgc = bg.reshape(tok // CHUNK, CHUNK, nbg)
    bgr = bg.T.reshape(4, hh, tok // CHUNK, 1, CHUNK)

    col = lambda g: pl.BlockSpec((n, width), lambda s, h: (blk0 + s, g * nhb + h))
    cw = lambda g: pl.BlockSpec((3, width), lambda s, h: (0, g * nhb + h))
    in_specs = [col(0), col(1), col(2), col(3), cw(0), cw(1), cw(2),
                pl.BlockSpec((nc, CHUNK, nbg), lambda s, h: (blk0 + s, 0, 0)),
                pl.BlockSpec((4, hb, nc, 1, CHUNK), lambda s, h: (0, h, blk0 + s, 0, 0)),
                pl.BlockSpec((2, hh), lambda s, h: (0, 0)),
                pl.BlockSpec((2, hh), lambda s, h: (0, 0)),
                pl.BlockSpec((1, LANES), lambda s, h: (0, 0))]
    args = [p1, p1, p1, p1, conv_qkv, conv_qkv, conv_qkv, bgc, bgr, a_log, dt_bias, onorm_g.reshape(1, LANES)]
    st_spec = pl.BlockSpec((None, 2, hb, LANES, LANES), lambda s, h: (s, 0, h, 0, 0))
    if latent:
        in_specs.append(st_spec)
        args.append(s0)
    out_specs = [pl.BlockSpec((n, width), lambda s, h: (s, h))]
    out_shape = [jax.ShapeDtypeStruct((nseq * n, hh * LANES), BF16)]
    if not latent:
        out_specs.append(st_spec)
        out_shape.append(jax.ShapeDtypeStruct((nseq, 2, hh, LANES, LANES), F32))
    scratch = [pltpu.VMEM((n, width), F32)] * 5 + [
        pltpu.VMEM((hb, 2, nc, CHUNK + LANES, LANES), BF16), pltpu.VMEM((hb, 2, nc, LANES, LANES), F32),
        pltpu.VMEM((hb, 2, nc, CHUNK, LANES), F32), pltpu.VMEM((hb, 2, nc, 1, LANES), F32)]
    return pl.pallas_call(
        functools.partial(_dn_body, n=n, per=per, heads=hh, hb=hb, has_s0=latent, emit_state=not latent,
                          group=group),
        grid=(nseq, nhb),
        in_specs=in_specs,
        out_specs=out_specs,
        out_shape=out_shape,
        scratch_shapes=scratch,
        compiler_params=_params("arbitrary", "arbitrary"),
        name="deltanet_lat" if latent else "deltanet_ctx",
    )(*args)


def _forward(cfg, x_prompt, x_sample, state_dn, c, c_ctx, norm1_g, norm2_g, w_ada, b_ada, w_in,
             conv_qkv, a_log, dt_bias, onorm_g, conv_sc, w_pa, w_pb, w_o, w_up, conv_ff,
             w_down, final_g):
    d, hh = cfg.d, cfg.heads
    w1_n = 4 * cfg.w_hd
    nbg = 4 * hh
    x = jnp.concatenate([x_prompt.reshape(cfg.tok_ctx, d), x_sample.reshape(-1, d)], axis=0)
    cvec = jnp.zeros((MOD_ROWS, d), F32).at[0].set(c_ctx).at[1:1 + cfg.dec_batch].set(c)
    mod = _ada(cvec, w_ada, b_ada)
    ff_extra = cfg.ff_pad - cfg.d_ff
    states = []
    for l in range(cfg.depth):
        mod3 = mod[l].reshape(MOD_ROWS * 6, 1, d)
        w1 = w_in[l][:, :w1_n].astype(BF16)
        wbg = w_in[l][:, w1_n:w1_n + nbg].astype(BF16)
        w2 = w_in[l][:, w1_n + nbg:].astype(BF16)
        h = _norm_mod(x, norm1_g[l], mod3, 0, 1, cfg)
        p1 = _mm(h, w1, F32, cfg, "proj_qkvz")
        bg = _mm(h, wbg, F32, cfg, "proj_bg")
        p2 = _mm(h, w2, F32, cfg, "proj_sc_gates")
        o_ctx, st = _deltanet(p1, bg, conv_qkv[l], a_log[l], dt_bias[l], onorm_g[l], None, cfg, latent=False)
        o_lat, = _deltanet(p1, bg, conv_qkv[l], a_log[l], dt_bias[l], onorm_g[l], state_dn[:, l], cfg, latent=True)
        states.append(st)
        o = jnp.concatenate([o_ctx, o_lat], axis=0)
        y_sc = _sc_mixer(p2, conv_sc[l], cfg)
        merged = _merge(o, y_sc, w_pa[l].astype(BF16), w_pb[l].astype(BF16), p2, 3 * cfg.w_sc, cfg)
        x = _resid(merged, w_o[l].astype(BF16), x, mod3, 2, d, cfg, "out_proj")
        h = _norm_mod(x, norm2_g[l], mod3, 3, 4, cfg)
        pad_c = lambda w: jnp.pad(w.astype(BF16), ((0, 0), (0, ff_extra)))
        wg, wv = pad_c(w_up[l][:, :cfg.d_ff]), pad_c(w_up[l][:, cfg.d_ff:])
        f = _ffn_up(h, wg, wv, jnp.pad(conv_ff[l], ((0, 0), (0, ff_extra))), cfg)
        wd = jnp.pad(w_down[l].astype(BF16), ((0, ff_extra), (0, 0)))
        x = _resid(f, wd, x, mod3, 5, cfg.ff_tk, cfg, "ffn_down")
    y_prompt = _final_norm(x, final_g, 0, cfg.tok_ctx, cfg).reshape(x_prompt.shape)
    y_sample = _final_norm(x, final_g, cfg.tok_ctx, cfg.tok - cfg.tok_ctx, cfg).reshape(x_sample.shape)
    return y_prompt, y_sample, jnp.stack(states, axis=1)


def kernel(x_prompt, x_sample, state_dn, c, c_ctx, norm1_g, norm2_g, w_ada, b_ada, w_in, conv_qkv, a_log, dt_bias, onorm_g, conv_sc, w_pa, w_pb, w_o, w_up, conv_ff, w_down, final_g):
    return _forward(CFG, x_prompt, x_sample, state_dn, c, c_ctx, norm1_g, norm2_g, w_ada, b_ada, w_in,
                    conv_qkv, a_log, dt_bias, onorm_g, conv_sc, w_pa, w_pb, w_o, w_up, conv_ff,
                    w_down, final_g)
```

```python
import functools
from typing import NamedTuple

import jax
import jax.numpy as jnp
from jax import lax
from jax.experimental import pallas as pl
from jax.experimental.pallas import tpu as pltpu

F32 = jnp.float32
BF16 = jnp.bfloat16
EPS = 1e-6
CHUNK = 64
LANES = 128
MOD_ROWS = 8
V7X_VMEM_BYTES = 64 * 1024 * 1024
VMEM_LIMIT_BYTES = V7X_VMEM_BYTES - 8 * 1024 * 1024


class Cfg(NamedTuple):
    d: int
    batch: int
    seq: int
    depth: int
    dec_batch: int
    dec_seq: int
    grid_w: int
    heads: int
    w_sc: int
    d_ff: int
    tm: int
    tn: int
    ff_tn: int
    ff_tk: int
    dn_heads: int

    @property
    def w_hd(self):
        return self.heads * LANES

    @property
    def tok_ctx(self):
        return self.batch * self.seq

    @property
    def tok(self):
        return self.batch * self.seq + self.dec_batch * self.dec_seq

    @property
    def ff_pad(self):
        return -(-self.d_ff // self.ff_tk) * self.ff_tk


CFG = Cfg(d=4096, batch=32, seq=256, depth=2, dec_batch=2, dec_seq=1024, grid_w=64, heads=16,
          w_sc=2048, d_ff=11008, tm=1024, tn=1024, ff_tn=256, ff_tk=2816, dn_heads=4)


def _params(*sem):
    return pltpu.CompilerParams(dimension_semantics=sem, vmem_limit_bytes=VMEM_LIMIT_BYTES)


def _mod_row(i, cfg, tm):
    nct = cfg.tok_ctx // tm
    return jnp.where(i < nct, 0, 1 + (i - nct) // (cfg.dec_seq // tm))


def _silu(x):
    return x * jax.nn.sigmoid(x)


def _softplus(x):
    return jnp.maximum(x, 0.0) + jnp.log1p(jnp.exp(-jnp.abs(x)))


def _dot(a, b):
    return jnp.dot(a, b, preferred_element_type=F32)


def _dotb(a, b):
    return _dot(a.astype(BF16), b.astype(BF16))


def _dot_nt(a, b):
    return lax.dot_general(a, b, (((1,), (1,)), ((), ())), preferred_element_type=F32)


def _dot_tn(a, b):
    return lax.dot_general(a, b, (((0,), (0,)), ((), ())), preferred_element_type=F32)


def _ada_body(c_ref, w_ref, b_ref, o_ref):
    s = _silu(c_ref[...]).astype(BF16)
    o_ref[...] = _dot(s, w_ref[...].astype(BF16)) + b_ref[...]


def _ada(cvec, w_ada, b_ada):
    depth, d, n = w_ada.shape
    tn = 512
    return pl.pallas_call(
        _ada_body,
        grid=(depth, n // tn),
        in_specs=[pl.BlockSpec((MOD_ROWS, d), lambda l, j: (0, 0)),
                  pl.BlockSpec((None, d, tn), lambda l, j: (l, 0, j)),
                  pl.BlockSpec((None, 1, tn), lambda l, j: (l, 0, j))],
        out_specs=pl.BlockSpec((None, MOD_ROWS, tn), lambda l, j: (l, 0, j)),
        out_shape=jax.ShapeDtypeStruct((depth, MOD_ROWS, n), F32),
        compiler_params=_params("arbitrary", "arbitrary"),
        name="ada",
    )(cvec, w_ada, b_ada.reshape(depth, 1, n))


def _norm_mod_body(x_ref, g_ref, sh_ref, sc_ref, o_ref):
    x = x_ref[...]
    y = x * lax.rsqrt(jnp.mean(x * x, axis=-1, keepdims=True) + EPS) * g_ref[...]
    o_ref[...] = (y * (1.0 + sc_ref[...]) + sh_ref[...]).astype(o_ref.dtype)


def _norm_mod(x, g, mod3, sh_idx, sc_idx, cfg):
    tok, d = x.shape
    tm = cfg.seq
    spec_mod = lambda idx: pl.BlockSpec((None, 1, d), lambda i: (_mod_row(i, cfg, tm) * 6 + idx, 0, 0))
    return pl.pallas_call(
        _norm_mod_body,
        grid=(tok // tm,),
        in_specs=[pl.BlockSpec((tm, d), lambda i: (i, 0)),
                  pl.BlockSpec((1, d), lambda i: (0, 0)),
                  spec_mod(sh_idx), spec_mod(sc_idx)],
        out_specs=pl.BlockSpec((tm, d), lambda i: (i, 0)),
        out_shape=jax.ShapeDtypeStruct((tok, d), BF16),
        compiler_params=_params("arbitrary"),
        name="norm_mod",
    )(x, g.reshape(1, d), mod3, mod3)


def _final_norm_body(x_ref, g_ref, o_ref):
    x = x_ref[...]
    o_ref[...] = x * lax.rsqrt(jnp.mean(x * x, axis=-1, keepdims=True) + EPS) * g_ref[...]


def _final_norm(x, g, row0, rows, cfg):
    d = x.shape[1]
    tm = cfg.seq
    off = row0 // tm
    return pl.pallas_call(
        _final_norm_body,
        grid=(rows // tm,),
        in_specs=[pl.BlockSpec((tm, d), lambda i: (i + off, 0)),
                  pl.BlockSpec((1, d), lambda i: (0, 0))],
        out_specs=pl.BlockSpec((tm, d), lambda i: (i, 0)),
        out_shape=jax.ShapeDtypeStruct((rows, d), F32),
        compiler_params=_params("arbitrary"),
        name="final_norm",
    )(x, g.reshape(1, d))


class WCols(NamedTuple):
    w: jax.Array
    layer: int
    col0: int
    shift: int
    act: int
    transposed: bool


CAST_ROWS = 256


def _cast_tile(w_ref, scr):
    def body(r, carry):
        rows = pl.ds(pl.multiple_of(r * CAST_ROWS, CAST_ROWS), CAST_ROWS)
        scr[rows, :] = w_ref[rows, :].astype(BF16)
        return carry

    lax.fori_loop(0, scr.shape[0] // CAST_ROWS, body, 0)


def _cast_tile_t(w_ref, nxt_ref, scr, shift):
    tn = scr.shape[1]
    for r in range(tn // CAST_ROWS):
        lo = r * CAST_ROWS + shift
        if lo + CAST_ROWS <= tn:
            w = w_ref[lo:lo + CAST_ROWS, :]
        else:
            w = jnp.concatenate([w_ref[lo:tn, :], nxt_ref[0:lo + CAST_ROWS - tn, :]], axis=0)
        scr[:, r * CAST_ROWS:(r + 1) * CAST_ROWS] = w.T.astype(BF16)


def _wres_body(*refs, n_act, weights, n_extra, epilogue):
    acts = refs[:n_act]
    pos = n_act
    w_refs = []
    for wc in weights:
        nxt = refs[pos + 1] if wc.shift else None
        w_refs.append((refs[pos], nxt))
        pos += 2 if wc.shift else 1
    extras = refs[pos:pos + n_extra]
    o_ref = refs[pos + n_extra]
    scrs = refs[pos + n_extra + 1:]

    @pl.when(pl.program_id(1) == 0)
    def _():
        for wc, (w_ref, nxt), scr in zip(weights, w_refs, scrs):
            if wc.transposed:
                _cast_tile_t(w_ref, nxt, scr, wc.shift)
            else:
                _cast_tile(w_ref, scr)

    prods = [_dot(acts[wc.act][...], scr[...]) for wc, scr in zip(weights, scrs)]
    epilogue(prods, extras, o_ref)


def _wres_matmul(name, acts, weights, extras, epilogue, n_out, out_dtype, tm, tn):
    tok = acts[0].shape[0]
    in_specs = [pl.BlockSpec((tm, a.shape[1]), lambda j, i: (i, 0)) for a in acts]
    args = list(acts)
    scratch = []
    for wc in weights:
        assert wc.col0 % tn == 0 and tn % CAST_ROWS == 0
        c0, layer = wc.col0 // tn, wc.layer
        if wc.transposed:
            k = wc.w.shape[2]
            in_specs.append(pl.BlockSpec((None, tn, k), lambda j, i, c0=c0, layer=layer: (layer, c0 + j, 0)))
            args.append(wc.w)
            if wc.shift:
                assert tn % wc.shift == 0 and wc.shift % 8 == 0
                per = tn // wc.shift
                in_specs.append(pl.BlockSpec((None, wc.shift, k),
                                             lambda j, i, c0=c0, layer=layer, per=per: (layer, (c0 + j + 1) * per, 0)))
                args.append(wc.w)
        else:
            k = wc.w.shape[1]
            assert wc.shift == 0 and k % CAST_ROWS == 0
            last = pl.cdiv(wc.w.shape[2], tn) - 1
            in_specs.append(pl.BlockSpec((None, k, tn),
                                         lambda j, i, c0=c0, layer=layer, last=last: (layer, 0, jnp.minimum(c0 + j, last))))
            args.append(wc.w)
        scratch.append(pltpu.VMEM((k, tn), BF16))
    for arr, spec in extras:
        in_specs.append(spec)
        args.append(arr)
    return pl.pallas_call(
        functools.partial(_wres_body, n_act=len(acts), weights=tuple(wc._replace(w=None) for wc in weights),
                          n_extra=len(extras), epilogue=epilogue),
        grid=(n_out // tn, tok // tm),
        in_specs=in_specs,
        out_specs=pl.BlockSpec((tm, tn), lambda j, i: (i, j)),
        out_shape=jax.ShapeDtypeStruct((tok, n_out), out_dtype),
        scratch_shapes=scratch,
        compiler_params=_params("arbitrary", "arbitrary"),
        name=name,
    )(*args)


def _store_epilogue(prods, extras, o_ref):
    o_ref[...] = prods[0].astype(o_ref.dtype)


def _bg_body(a_ref, w_ref, o_ref):
    o_ref[...] = _dot_nt(a_ref[...], w_ref[...].astype(BF16))


def _bg_proj(h, w_t, layer, row0, nbg, cfg):
    tok, k = h.shape
    tm = cfg.tm
    assert row0 % nbg == 0
    return pl.pallas_call(
        _bg_body,
        grid=(tok // tm,),
        in_specs=[pl.BlockSpec((tm, k), lambda i: (i, 0)),
                  pl.BlockSpec((None, nbg, k), lambda i: (layer, row0 // nbg, 0))],
        out_specs=pl.BlockSpec((tm, nbg), lambda i: (i, 0)),
        out_shape=jax.ShapeDtypeStruct((tok, nbg), F32),
        compiler_params=_params("arbitrary"),
        name="proj_bg",
    )(h, w_t)


def _merge_body(o_ref, y_ref, wa_ref, wb_ref, ga_ref, gb_ref, out_ref):
    pa = _dot(o_ref[...], wa_ref[...])
    pb = _dot(y_ref[...], wb_ref[...])
    out_ref[...] = (jax.nn.sigmoid(ga_ref[...]) * pa + jax.nn.sigmoid(gb_ref[...]) * pb).astype(out_ref.dtype)


def _merge(o, y, w_pa, w_pb, p2, gate_col0, cfg):
    tok = o.shape[0]
    d = w_pa.shape[1]
    tm, tn = cfg.tm, min(cfg.tn // 2, d)
    ga0 = gate_col0 // tn
    gb0 = (gate_col0 + d) // tn
    return pl.pallas_call(
        _merge_body,
        grid=(tok // tm, d // tn),
        in_specs=[pl.BlockSpec((tm, o.shape[1]), lambda i, j: (i, 0)),
                  pl.BlockSpec((tm, y.shape[1]), lambda i, j: (i, 0)),
                  pl.BlockSpec((w_pa.shape[0], tn), lambda i, j: (0, j)),
                  pl.BlockSpec((w_pb.shape[0], tn), lambda i, j: (0, j)),
                  pl.BlockSpec((tm, tn), lambda i, j: (i, ga0 + j)),
                  pl.BlockSpec((tm, tn), lambda i, j: (i, gb0 + j))],
        out_specs=pl.BlockSpec((tm, tn), lambda i, j: (i, j)),
        out_shape=jax.ShapeDtypeStruct((tok, d), BF16),
        compiler_params=_params("arbitrary", "arbitrary"),
        name="merge",
    )(o, y, w_pa, w_pb, p2, p2)


def _ffn_up_epilogue(prods, extras, o_ref, *, cfg):
    cw_ref, = extras
    g, v = prods
    tm, tn = o_ref.shape
    j, i = pl.program_id(0), pl.program_id(1)
    w = cw_ref[...]
    nct = cfg.tok_ctx // tm
    valid = (lax.broadcasted_iota(jnp.int32, (tm, tn), 1) + j * tn) < cfg.d_ff

    @pl.when(i < nct)
    def _():
        t = lax.broadcasted_iota(jnp.int32, (tm, tn), 0) & (cfg.seq - 1)
        gm = jnp.where(t == 0, 0.0, pltpu.roll(g, 1, 0))
        gp = jnp.where(t == cfg.seq - 1, 0.0, pltpu.roll(g, tm - 1, 0))
        y = gm * w[0:1] + g * w[1:2] + gp * w[2:3]
        o_ref[...] = jnp.where(valid, _silu(y) * v, 0.0).astype(o_ref.dtype)

    @pl.when(i >= nct)
    def _():
        gw = cfg.grid_w
        z = jnp.zeros((gw, tn), F32)
        gm = jnp.concatenate([z, g[:tm - gw]], axis=0)
        gp = jnp.concatenate([g[gw:], z], axis=0)
        y = gm * w[0:1] + g * w[1:2] + gp * w[2:3]
        o_ref[...] = jnp.where(valid, _silu(y) * v, 0.0).astype(o_ref.dtype)


def _resid_body(a_ref, w_ref, x_ref, gt_ref, o_ref, *, nk):
    p = _dot(a_ref[...], w_ref[...])
    if nk == 1:
        o_ref[...] = x_ref[...] + gt_ref[...] * p
        return
    k = pl.program_id(2)

    @pl.when(k == 0)
    def _():
        o_ref[...] = p

    @pl.when(jnp.logical_and(k > 0, k < nk - 1))
    def _():
        o_ref[...] += p

    @pl.when(k == nk - 1)
    def _():
        o_ref[...] = x_ref[...] + gt_ref[...] * (o_ref[...] + p)


def _resid(a, w, x, mod3, gt_idx, tk, cfg, name):
    tok, kdim = a.shape
    d = w.shape[1]
    tm, tn = cfg.tm, min(cfg.tn, d)
    nk = kdim // tk
    return pl.pallas_call(
        functools.partial(_resid_body, nk=nk),
        grid=(tok // tm, d // tn, nk),
        in_specs=[pl.BlockSpec((tm, tk), lambda i, j, k: (i, k)),
                  pl.BlockSpec((tk, tn), lambda i, j, k: (k, j)),
                  pl.BlockSpec((tm, tn), lambda i, j, k: (i, j)),
                  pl.BlockSpec((None, 1, tn), lambda i, j, k: (_mod_row(i, cfg, tm) * 6 + gt_idx, 0, j))],
        out_specs=pl.BlockSpec((tm, tn), lambda i, j, k: (i, j)),
        out_shape=jax.ShapeDtypeStruct((tok, d), F32),
        compiler_params=_params("arbitrary", "arbitrary", "arbitrary"),
        name=name,
    )(a, w, x, mod3)


def _sc_body(xs_ref, bs_ref, cs_ref, w_ref, o_ref, *, cfg):
    tm, tn = o_ref.shape
    i = pl.program_id(0)
    per = jnp.where(i < cfg.tok_ctx // tm, cfg.seq, cfg.grid_w)
    u = cs_ref[...] * xs_ref[...]
    t = lax.broadcasted_iota(jnp.int32, (tm, tn), 0) & (per - 1)
    um = jnp.where(t == 0, 0.0, pltpu.roll(u, 1, 0))
    up = jnp.where(t == per - 1, 0.0, pltpu.roll(u, tm - 1, 0))
    w = w_ref[...]
    o_ref[...] = (bs_ref[...] * (um * w[0:1] + u * w[1:2] + up * w[2:3])).astype(o_ref.dtype)


def _sc_mixer(p2, cw, cfg):
    tok = p2.shape[0]
    ws = cfg.w_sc
    tm, tn = cfg.tm, min(512, ws)
    nb = ws // tn
    return pl.pallas_call(
        functools.partial(_sc_body, cfg=cfg),
        grid=(tok // tm, nb),
        in_specs=[pl.BlockSpec((tm, tn), lambda i, j: (i, j)),
                  pl.BlockSpec((tm, tn), lambda i, j: (i, nb + j)),
                  pl.BlockSpec((tm, tn), lambda i, j: (i, 2 * nb + j)),
                  pl.BlockSpec((3, tn), lambda i, j: (0, j))],
        out_specs=pl.BlockSpec((tm, tn), lambda i, j: (i, j)),
        out_shape=jax.ShapeDtypeStruct((tok, ws), BF16),
        compiler_params=_params("arbitrary", "arbitrary"),
        name="sc_mixer",
    )(p2, p2, p2, cw)


def _dn_body(*refs, n, per, heads, hb, has_s0, emit_state, group):
    it = iter(refs)
    q_ref, k_ref, v_ref, z_ref = next(it), next(it), next(it), next(it)
    wq_ref, wk_ref, wv_ref = next(it), next(it), next(it)
    bgc_ref, bgr_ref, alog_ref, dtb_ref, og_ref = next(it), next(it), next(it), next(it), next(it)
    s0_ref = next(it) if has_s0 else None
    o_ref = next(it)
    st_ref = next(it) if emit_state else None
    qs, ks, vs, of_s, ob_s, qw_s, n_s, oc_s, eg_s = it

    nc = n // CHUNK
    width = hb * LANES
    hblk = pl.program_id(1)
    head_cols = [slice(b * LANES, (b + 1) * LANES) for b in range(hb)]

    tok = lax.broadcasted_iota(jnp.int32, (n, width), 0) & (per - 1)
    first, last = tok == 0, tok == per - 1

    def conv_silu(x_ref, w_ref):
        x, w = x_ref[...], w_ref[...]
        xm = jnp.where(first, 0.0, pltpu.roll(x, 1, 0))
        xp = jnp.where(last, 0.0, pltpu.roll(x, n - 1, 0))
        return _silu(xm * w[0:1] + x * w[1:2] + xp * w[2:3])

    q = conv_silu(q_ref, wq_ref)
    k = conv_silu(k_ref, wk_ref)
    for sl in head_cols:
        qh, kh = q[:, sl], k[:, sl]
        qs[:, sl] = qh * lax.rsqrt(jnp.sum(qh * qh, axis=-1, keepdims=True) + EPS) * (LANES ** -0.5)
        ks[:, sl] = kh * lax.rsqrt(jnp.sum(kh * kh, axis=-1, keepdims=True) + EPS)
    vs[...] = conv_silu(v_ref, wv_ref)

    head_iota = lax.broadcasted_iota(jnp.int32, (2, heads), 1)
    ea, dtb = [], []
    for b in range(hb):
        sel = head_iota == hblk * hb + b
        ea.append(jnp.exp(jnp.sum(jnp.where(sel, alog_ref[...], 0.0), axis=1, keepdims=True)))
        dtb.append(jnp.sum(jnp.where(sel, dtb_ref[...], 0.0), axis=1, keepdims=True))

    nbg = bgc_ref.shape[-1]
    lane_bg = lax.broadcasted_iota(jnp.int32, (CHUNK, nbg), 1)
    sub = lax.broadcasted_iota(jnp.int32, (CHUNK, CHUNK), 0)
    lane = lax.broadcasted_iota(jnp.int32, (CHUNK, CHUNK), 1)
    eye = (sub == lane).astype(F32)
    blk16 = (sub // 16) == (lane // 16)
    blk32 = (sub // 32) == (lane // 32)
    off32 = jnp.logical_and(blk32, jnp.logical_not(blk16))
    vis = (lane <= sub, lane >= sub)
    vis_t = (sub <= lane, sub >= lane)
    strict = (lane < sub, lane > sub)

    def stage_group(gi, carry):
        units = []
        for b in range(hb):
            for j in range(group):
                c = gi * group + j
                r0 = c * CHUNK
                if not isinstance(r0, int):
                    r0 = pl.multiple_of(r0, CHUNK)
                rows = pl.ds(r0, CHUNK)
                qc, kc, vc = qs[rows, head_cols[b]], ks[rows, head_cols[b]], vs[rows, head_cols[b]]
                kb = kc.astype(BF16)
                kk = _dot_nt(kb, kb)
                qk = _dot_nt(qc.astype(BF16), kb)
                bgc = bgc_ref[c]
                hd = hblk * hb + b
                for d in range(2):
                    col = lambda g: jnp.sum(jnp.where(lane_bg == g * heads + hd, bgc, 0.0), axis=1, keepdims=True)
                    beta = jax.nn.sigmoid(col(d))
                    g_col = -ea[b][d:d + 1] * _softplus(col(2 + d) + dtb[b][d:d + 1])
                    g_row = -ea[b][d:d + 1] * _softplus(bgr_ref[2 + d, b, c] + dtb[b][d:d + 1])
                    m = vis[d]
                    gc_col = jnp.sum(jnp.where(m, jnp.broadcast_to(g_row, (CHUNK, CHUNK)), 0.0), axis=1, keepdims=True)
                    gc_row = jnp.sum(jnp.where(vis_t[d], jnp.broadcast_to(g_col, (CHUNK, CHUNK)), 0.0), axis=0, keepdims=True)
                    g_last = jnp.sum(g_row, axis=1, keepdims=True)
                    decay = jnp.where(m, jnp.exp(jnp.where(m, gc_col - gc_row, 0.0)), 0.0)
                    e_gc = jnp.exp(gc_col)
                    units.append(dict(
                        b=b, c=c, d=d,
                        a=jnp.where(strict[d], kk * decay * beta, 0.0),
                        x=jnp.concatenate([vc * beta, kc * (beta * e_gc)], axis=1),
                        intra=jnp.where(m, qk * decay, 0.0).astype(BF16),
                        k_dec=(kc * jnp.exp(g_last - gc_col)).astype(BF16),
                        q_dec=qc * e_gc,
                        eg=jnp.broadcast_to(jnp.exp(g_last), (1, LANES))))
        a_d = [jnp.where(blk16, u["a"], 0.0).astype(BF16) for u in units]
        p2 = [_dot(x, x).astype(BF16) for x in a_d]
        t = [eye - x.astype(F32) for x in a_d]
        t = [x + _dot(x.astype(BF16), p) for x, p in zip(t, p2)]
        p4 = [_dot(p, p).astype(BF16) for p in p2]
        t = [x + _dot(x.astype(BF16), p) for x, p in zip(t, p4)]
        p8 = [_dot(p, p).astype(BF16) for p in p4]
        t = [x + _dot(x.astype(BF16), p) for x, p in zip(t, p8)]
        mm = [_dotb(x, jnp.where(off32, u["a"], 0.0)) for x, u in zip(t, units)]
        t = [x - _dotb(y, x) for x, y in zip(t, mm)]
        mm = [_dotb(x, jnp.where(blk32, 0.0, u["a"])) for x, u in zip(t, units)]
        t = [x - _dotb(y, x) for x, y in zip(t, mm)]
        xb = [_dotb(x, u["x"]).astype(BF16) for x, u in zip(t, units)]
        iw = [_dot(u["intra"], x) for u, x in zip(units, xb)]
        kt = [_dot_tn(u["k_dec"], x) for u, x in zip(units, xb)]
        for u, iwu, ktu in zip(units, iw, kt):
            b, c, d = u["b"], u["c"], u["d"]
            oc_s[b, d, c] = iwu[:, :LANES]
            qw_s[b, d, c, 0:CHUNK, :] = (u["q_dec"] - iwu[:, LANES:]).astype(BF16)
            qw_s[b, d, c, CHUNK:, :] = ktu[:, LANES:].astype(BF16)
            n_s[b, d, c] = ktu[:, :LANES]
            eg_s[b, d, c] = u["eg"]
        return carry

    if nc == group:
        stage_group(0, 0)
    else:
        lax.fori_loop(0, nc // group, stage_group, 0)

    chains = [(b, d) for b in range(hb) for d in range(2)]
    if has_s0:
        s_init = tuple(s0_ref[d, b] for b, d in chains)
    else:
        s_init = tuple(jnp.zeros((LANES, LANES), F32) for _ in chains)

    def scan_step(t, carry):
        res = []
        for (b, d), s in zip(chains, carry):
            c = t if d == 0 else nc - 1 - t
            res.append(_dot(qw_s[b, d, c], s.astype(BF16)))
        out = []
        for (b, d), s, r in zip(chains, carry, res):
            c = t if d == 0 else nc - 1 - t
            r0 = c * CHUNK
            if not isinstance(r0, int):
                r0 = pl.multiple_of(r0, CHUNK)
            acc = of_s if d == 0 else ob_s
            acc[pl.ds(r0, CHUNK), head_cols[b]] = r[:CHUNK] + oc_s[b, d, c]
            out.append(s * eg_s[b, d, c] - r[CHUNK:] + n_s[b, d, c])
        return tuple(out)

    if nc == group:
        s_fin = s_init
        for t in range(nc):
            s_fin = scan_step(t, s_fin)
    else:
        s_fin = lax.fori_loop(0, nc, scan_step, s_init)

    o = of_s[...] + ob_s[...]
    z = _silu(z_ref[...])
    og = og_ref[...]
    for sl in head_cols:
        oh = o[:, sl]
        y = oh * lax.rsqrt(jnp.mean(oh * oh, axis=-1, keepdims=True) + EPS) * og
        o_ref[:, sl] = (y * z[:, sl]).astype(o_ref.dtype)
    if emit_state:
        for (b, d), s in zip(chains, s_fin):
            st_ref[d, b] = s


def _deltanet(p1, bg, conv_qkv, a_log, dt_bias, onorm_g, s0, cfg, *, latent):
    hh, hb = cfg.heads, cfg.dn_heads
    tok, nbg = bg.shape
    n = cfg.dec_seq if latent else cfg.seq
    nseq = cfg.dec_batch if latent else cfg.batch
    blk0 = cfg.tok_ctx // n if latent else 0
    per = cfg.grid_w if latent else cfg.seq
    nc = n // CHUNK
    group = min(4, nc)
    width = hb * LANES
    nhb = hh // hb
    bgc = bg.reshape(tok // CHUNK, CHUNK, nbg)
    bgr = bg[:, :4 * hh].T.reshape(4, hh, tok // CHUNK, 1, CHUNK)

    col = lambda g: pl.BlockSpec((n, width), lambda s, h: (blk0 + s, g * nhb + h))
    cw = lambda g: pl.BlockSpec((3, width), lambda s, h: (0, g * nhb + h))
    in_specs = [col(0), col(1), col(2), col(3), cw(0), cw(1), cw(2),
                pl.BlockSpec((nc, CHUNK, nbg), lambda s, h: (blk0 + s, 0, 0)),
                pl.BlockSpec((4, hb, nc, 1, CHUNK), lambda s, h: (0, h, blk0 + s, 0, 0)),
                pl.BlockSpec((2, hh), lambda s, h: (0, 0)),
                pl.BlockSpec((2, hh), lambda s, h: (0, 0)),
                pl.BlockSpec((1, LANES), lambda s, h: (0, 0))]
    args = [p1, p1, p1, p1, conv_qkv, conv_qkv, conv_qkv, bgc, bgr, a_log, dt_bias, onorm_g.reshape(1, LANES)]
    st_spec = pl.BlockSpec((None, 2, hb, LANES, LANES), lambda s, h: (s, 0, h, 0, 0))
    if latent:
        in_specs.append(st_spec)
        args.append(s0)
    out_specs = [pl.BlockSpec((n, width), lambda s, h: (s, h))]
    out_shape = [jax.ShapeDtypeStruct((nseq * n, hh * LANES), BF16)]
    if not latent:
        out_specs.append(st_spec)
        out_shape.append(jax.ShapeDtypeStruct((nseq, 2, hh, LANES, LANES), F32))
    scratch = [pltpu.VMEM((n, width), F32)] * 5 + [
        pltpu.VMEM((hb, 2, nc, CHUNK + LANES, LANES), BF16), pltpu.VMEM((hb, 2, nc, LANES, LANES), F32),
        pltpu.VMEM((hb, 2, nc, CHUNK, LANES), F32), pltpu.VMEM((hb, 2, nc, 1, LANES), F32)]
    return pl.pallas_call(
        functools.partial(_dn_body, n=n, per=per, heads=hh, hb=hb, has_s0=latent, emit_state=not latent,
                          group=group),
        grid=(nseq, nhb),
        in_specs=in_specs,
        out_specs=out_specs,
        out_shape=out_shape,
        scratch_shapes=scratch,
        compiler_params=_params("arbitrary", "arbitrary"),
        name="deltanet_lat" if latent else "deltanet_ctx",
    )(*args)


def _forward(cfg, x_prompt, x_sample, state_dn, c, c_ctx, norm1_g, norm2_g, w_ada, b_ada, w_in,
             conv_qkv, a_log, dt_bias, onorm_g, conv_sc, w_pa, w_pb, w_o, w_up, conv_ff,
             w_down, final_g):
    d, hh = cfg.d, cfg.heads
    w1_n = 4 * cfg.w_hd
    nbg = 4 * hh
    x = jnp.concatenate([x_prompt.reshape(cfg.tok_ctx, d), x_sample.reshape(-1, d)], axis=0)
    cvec = jnp.zeros((MOD_ROWS, d), F32).at[0].set(c_ctx).at[1:1 + cfg.dec_batch].set(c)
    mod = _ada(cvec, w_ada, b_ada)
    ff_extra = cfg.ff_pad - cfg.d_ff
    w_in_t = jnp.swapaxes(w_in, 1, 2)
    states = []
    n2 = 3 * cfg.w_sc + 2 * d
    tnh = cfg.tn // 2
    ff_tn = cfg.ff_tn
    ff_last = cfg.d_ff // ff_tn - 1
    assert cfg.tm == cfg.dec_seq and cfg.tm % cfg.seq == 0 and cfg.d_ff % ff_tn == 0 and cfg.ff_pad % ff_tn == 0
    for l in range(cfg.depth):
        mod3 = mod[l].reshape(MOD_ROWS * 6, 1, d)
        h = _norm_mod(x, norm1_g[l], mod3, 0, 1, cfg)
        p1 = _wres_matmul("proj_qkvz", [h], [WCols(w_in_t, l, 0, 0, 0, True)], [], _store_epilogue,
                          w1_n, F32, cfg.tm, tnh)
        bg = _bg_proj(h, w_in_t, l, w1_n, nbg, cfg)
        p2 = _wres_matmul("proj_sc_gates", [h], [WCols(w_in_t, l, w1_n, nbg, 0, True)], [], _store_epilogue,
                          n2, F32, cfg.tm, tnh)
        o_ctx, st = _deltanet(p1, bg, conv_qkv[l], a_log[l], dt_bias[l], onorm_g[l], None, cfg, latent=False)
        o_lat, = _deltanet(p1, bg, conv_qkv[l], a_log[l], dt_bias[l], onorm_g[l], state_dn[:, l], cfg, latent=True)
        states.append(st)
        o = jnp.concatenate([o_ctx, o_lat], axis=0)
        y_sc = _sc_mixer(p2, conv_sc[l], cfg)
        merged = _merge(o, y_sc, w_pa[l].astype(BF16), w_pb[l].astype(BF16), p2, 3 * cfg.w_sc, cfg)
        x = _resid(merged, w_o[l].astype(BF16), x, mod3, 2, d, cfg, "out_proj")
        h = _norm_mod(x, norm2_g[l], mod3, 3, 4, cfg)
        f = _wres_matmul(
            "ffn_up", [h], [WCols(w_up, l, 0, 0, 0, False), WCols(w_up, l, cfg.d_ff, 0, 0, False)],
            [(conv_ff, pl.BlockSpec((None, 3, ff_tn), lambda j, i, l=l: (l, 0, jnp.minimum(j, ff_last))))],
            functools.partial(_ffn_up_epilogue, cfg=cfg), cfg.ff_pad, BF16, cfg.tm, ff_tn)
        wd = jnp.pad(w_down[l].astype(BF16), ((0, ff_extra), (0, 0)))
        x = _resid(f, wd, x, mod3, 5, cfg.ff_tk, cfg, "ffn_down")
    y_prompt = _final_norm(x, final_g, 0, cfg.tok_ctx, cfg).reshape(x_prompt.shape)
    y_sample = _final_norm(x, final_g, cfg.tok_ctx, cfg.tok - cfg.tok_ctx, cfg).reshape(x_sample.shape)
    return y_prompt, y_sample, jnp.stack(states, axis=1)


def kernel(x_prompt, x_sample, state_dn, c, c_ctx, norm1_g, norm2_g, w_ada, b_ada, w_in, conv_qkv, a_log, dt_bias, onorm_g, conv_sc, w_pa, w_pb, w_o, w_up, conv_ff, w_down, final_g):
    return _forward(CFG, x_prompt, x_sample, state_dn, c, c_ctx, norm1_g, norm2_g, w_ada, b_ada, w_in,
                    conv_qkv, a_log, dt_bias, onorm_g, conv_sc, w_pa, w_pb, w_o, w_up, conv_ff,
                    w_down, final_g)
```

```python
import functools
from typing import NamedTuple

import jax
import jax.numpy as jnp
from jax import lax
from jax.experimental import pallas as pl
from jax.experimental.pallas import tpu as pltpu

F32 = jnp.float32
BF16 = jnp.bfloat16
EPS = 1e-6
CHUNK = 64
LANES = 128
MOD_ROWS = 8
V7X_VMEM_BYTES = 64 * 1024 * 1024
VMEM_LIMIT_BYTES = V7X_VMEM_BYTES - 8 * 1024 * 1024


class Cfg(NamedTuple):
    d: int
    batch: int
    seq: int
    depth: int
    dec_batch: int
    dec_seq: int
    grid_w: int
    heads: int
    w_sc: int
    d_ff: int
    tm: int
    tn: int
    ff_tn: int
    ff_tk: int
    dn_heads: int
    kc: int

    @property
    def w_hd(self):
        return self.heads * LANES

    @property
    def tok_ctx(self):
        return self.batch * self.seq

    @property
    def tok(self):
        return self.batch * self.seq + self.dec_batch * self.dec_seq

    @property
    def ff_pad(self):
        return -(-self.d_ff // self.ff_tk) * self.ff_tk


CFG = Cfg(d=4096, batch=32, seq=256, depth=2, dec_batch=2, dec_seq=1024, grid_w=64, heads=16,
          w_sc=2048, d_ff=11008, tm=1024, tn=1024, ff_tn=1024, ff_tk=2816, dn_heads=4, kc=512)
FFN_SUBTILES = 4


def _params(*sem):
    return pltpu.CompilerParams(dimension_semantics=sem, vmem_limit_bytes=VMEM_LIMIT_BYTES)


def _mod_row(i, cfg, tm):
    nct = cfg.tok_ctx // tm
    return jnp.where(i < nct, 0, 1 + (i - nct) // (cfg.dec_seq // tm))


def _silu(x):
    return x * jax.nn.sigmoid(x)


def _softplus(x):
    return jnp.maximum(x, 0.0) + jnp.log1p(jnp.exp(-jnp.abs(x)))


def _dot(a, b):
    return jnp.dot(a, b, preferred_element_type=F32)


def _dotb(a, b):
    return _dot(a.astype(BF16), b.astype(BF16))


def _dot_nt(a, b):
    return lax.dot_general(a, b, (((1,), (1,)), ((), ())), preferred_element_type=F32)


def _dot_tn(a, b):
    return lax.dot_general(a, b, (((0,), (0,)), ((), ())), preferred_element_type=F32)


def _ada_body(c_ref, w_ref, b_ref, o_ref):
    s = _silu(c_ref[...]).astype(BF16)
    o_ref[...] = _dot(s, w_ref[...].astype(BF16)) + b_ref[...]


def _ada(cvec, w_ada, b_ada):
    depth, d, n = w_ada.shape
    tn = 512
    return pl.pallas_call(
        _ada_body,
        grid=(depth, n // tn),
        in_specs=[pl.BlockSpec((MOD_ROWS, d), lambda l, j: (0, 0)),
                  pl.BlockSpec((None, d, tn), lambda l, j: (l, 0, j)),
                  pl.BlockSpec((None, 1, tn), lambda l, j: (l, 0, j))],
        out_specs=pl.BlockSpec((None, MOD_ROWS, tn), lambda l, j: (l, 0, j)),
        out_shape=jax.ShapeDtypeStruct((depth, MOD_ROWS, n), F32),
        compiler_params=_params("arbitrary", "arbitrary"),
        name="ada",
    )(cvec, w_ada, b_ada.reshape(depth, 1, n))


def _norm_mod_body(x_ref, g_ref, sh_ref, sc_ref, o_ref):
    x = x_ref[...]
    y = x * lax.rsqrt(jnp.mean(x * x, axis=-1, keepdims=True) + EPS) * g_ref[...]
    o_ref[...] = (y * (1.0 + sc_ref[...]) + sh_ref[...]).astype(o_ref.dtype)


def _norm_mod(x, g, mod3, sh_idx, sc_idx, cfg):
    tok, d = x.shape
    tm = cfg.seq
    spec_mod = lambda idx: pl.BlockSpec((None, 1, d), lambda i: (_mod_row(i, cfg, tm) * 6 + idx, 0, 0))
    return pl.pallas_call(
        _norm_mod_body,
        grid=(tok // tm,),
        in_specs=[pl.BlockSpec((tm, d), lambda i: (i, 0)),
                  pl.BlockSpec((1, d), lambda i: (0, 0)),
                  spec_mod(sh_idx), spec_mod(sc_idx)],
        out_specs=pl.BlockSpec((tm, d), lambda i: (i, 0)),
        out_shape=jax.ShapeDtypeStruct((tok, d), BF16),
        compiler_params=_params("arbitrary"),
        name="norm_mod",
    )(x, g.reshape(1, d), mod3, mod3)


def _final_norm_body(x_ref, g_ref, o_ref):
    x = x_ref[...]
    o_ref[...] = x * lax.rsqrt(jnp.mean(x * x, axis=-1, keepdims=True) + EPS) * g_ref[...]


def _final_norm(x, g, row0, rows, cfg):
    d = x.shape[1]
    tm = cfg.seq
    off = row0 // tm
    return pl.pallas_call(
        _final_norm_body,
        grid=(rows // tm,),
        in_specs=[pl.BlockSpec((tm, d), lambda i: (i + off, 0)),
                  pl.BlockSpec((1, d), lambda i: (0, 0))],
        out_specs=pl.BlockSpec((tm, d), lambda i: (i, 0)),
        out_shape=jax.ShapeDtypeStruct((rows, d), F32),
        compiler_params=_params("arbitrary"),
        name="final_norm",
    )(x, g.reshape(1, d))


class WCols(NamedTuple):
    w: jax.Array
    layer: int
    col0: int
    shift: int
    act: int
    transposed: bool
    stride: int


def _wres_body(*refs, n_act, weights, n_extra, epilogue, kc, n_cast):
    acts = refs[:n_act]
    pos = n_act
    w_refs = []
    for wc in weights:
        nxt = refs[pos + 1] if wc.shift else None
        w_refs.append((refs[pos], nxt))
        pos += 2 if wc.shift else 1
    extras = refs[pos:pos + n_extra]
    o_ref = refs[pos + n_extra]
    scrs = refs[pos + n_extra + 1:]
    s = pl.program_id(1)

    @pl.when(s < n_cast)
    def _():
        rows = pl.ds(pl.multiple_of(s * kc, kc), kc)
        for wc, (w_ref, nxt), scr in zip(weights, w_refs, scrs):
            if not wc.transposed:
                scr[rows, :] = w_ref[...].astype(BF16)
                continue
            w = w_ref[...]
            if wc.shift:
                w = jnp.concatenate([w[wc.shift:], nxt[...]], axis=0)
            scr[rows, :] = w.T.astype(BF16)

    @pl.when(s >= n_cast)
    def _():
        prods = [_dot(acts[wc.act][...], scr[...]) for wc, scr in zip(weights, scrs)]
        epilogue(prods, extras, o_ref, s - n_cast)


def _wres_matmul(name, acts, weights, extras, epilogue, n_out, out_dtype, tm, tn, tn_out, kc):
    tok = acts[0].shape[0]
    kdim = acts[0].shape[1]
    n_cast = kdim // kc
    assert kdim % kc == 0 and all(a.shape[1] == kdim for a in acts)
    row = lambda s: jnp.maximum(s - n_cast, 0)
    chunk = lambda s: jnp.minimum(s, n_cast - 1)
    in_specs = [pl.BlockSpec((tm, kdim), lambda j, s: (row(s), 0)) for a in acts]
    args = list(acts)
    scratch = []
    for wc in weights:
        assert wc.col0 % tn == 0
        c0, layer, stride = wc.col0 // tn, wc.layer, wc.stride
        if wc.transposed:
            assert wc.w.shape[2] == kdim
            in_specs.append(pl.BlockSpec(
                (None, tn, kc),
                lambda j, s, c0=c0, layer=layer, stride=stride: (layer, c0 + j * stride, chunk(s))))
            args.append(wc.w)
            if wc.shift:
                assert tn % wc.shift == 0 and wc.shift % 8 == 0
                per = tn // wc.shift
                in_specs.append(pl.BlockSpec(
                    (None, wc.shift, kc),
                    lambda j, s, c0=c0, layer=layer, stride=stride, per=per:
                    (layer, (c0 + j * stride + 1) * per, chunk(s))))
                args.append(wc.w)
        else:
            assert wc.shift == 0 and wc.w.shape[1] == kdim
            last = pl.cdiv(wc.w.shape[2], tn) - 1
            in_specs.append(pl.BlockSpec(
                (None, kc, tn),
                lambda j, s, c0=c0, layer=layer, stride=stride, last=last:
                (layer, chunk(s), jnp.minimum(c0 + j * stride, last))))
            args.append(wc.w)
        scratch.append(pltpu.VMEM((kdim, tn), BF16))
    for arr, block, index_map in extras:
        in_specs.append(pl.BlockSpec(block, lambda j, s, index_map=index_map: index_map(j, row(s))))
        args.append(arr)
    return pl.pallas_call(
        functools.partial(_wres_body, n_act=len(acts), weights=tuple(wc._replace(w=None) for wc in weights),
                          n_extra=len(extras), epilogue=epilogue, kc=kc, n_cast=n_cast),
        grid=(n_out // tn_out, n_cast + tok // tm),
        in_specs=in_specs,
        out_specs=pl.BlockSpec((tm, tn_out), lambda j, s: (row(s), j)),
        out_shape=jax.ShapeDtypeStruct((tok, n_out), out_dtype),
        scratch_shapes=scratch,
        compiler_params=_params("arbitrary", "arbitrary"),
        name=name,
    )(*args)


def _store_epilogue(prods, extras, o_ref, i):
    o_ref[...] = prods[0].astype(o_ref.dtype)


def _bg_body(a_ref, w_ref, o_ref):
    o_ref[...] = _dot_nt(a_ref[...], w_ref[...].astype(BF16))


def _bg_proj(h, w_t, layer, row0, nbg, cfg):
    tok, k = h.shape
    tm = cfg.tm
    assert row0 % nbg == 0
    return pl.pallas_call(
        _bg_body,
        grid=(tok // tm,),
        in_specs=[pl.BlockSpec((tm, k), lambda i: (i, 0)),
                  pl.BlockSpec((None, nbg, k), lambda i: (layer, row0 // nbg, 0))],
        out_specs=pl.BlockSpec((tm, nbg), lambda i: (i, 0)),
        out_shape=jax.ShapeDtypeStruct((tok, nbg), F32),
        compiler_params=_params("arbitrary"),
        name="proj_bg",
    )(h, w_t)


def _merge_body(o_ref, y_ref, wa_ref, wb_ref, ga_ref, gb_ref, out_ref):
    pa = _dot(o_ref[...], wa_ref[...])
    pb = _dot(y_ref[...], wb_ref[...])
    out_ref[...] = (jax.nn.sigmoid(ga_ref[...]) * pa + jax.nn.sigmoid(gb_ref[...]) * pb).astype(out_ref.dtype)


def _merge(o, y, w_pa, w_pb, layer, p2, gate_col0, cfg):
    tok = o.shape[0]
    d = w_pa.shape[2]
    tm, tn = cfg.tm, min(cfg.tn // 2, d)
    ga0 = gate_col0 // tn
    gb0 = (gate_col0 + d) // tn
    return pl.pallas_call(
        _merge_body,
        grid=(tok // tm, d // tn),
        in_specs=[pl.BlockSpec((tm, o.shape[1]), lambda i, j: (i, 0)),
                  pl.BlockSpec((tm, y.shape[1]), lambda i, j: (i, 0)),
                  pl.BlockSpec((None, w_pa.shape[1], tn), lambda i, j: (layer, 0, j)),
                  pl.BlockSpec((None, w_pb.shape[1], tn), lambda i, j: (layer, 0, j)),
                  pl.BlockSpec((tm, tn), lambda i, j: (i, ga0 + j)),
                  pl.BlockSpec((tm, tn), lambda i, j: (i, gb0 + j))],
        out_specs=pl.BlockSpec((tm, tn), lambda i, j: (i, j)),
        out_shape=jax.ShapeDtypeStruct((tok, d), BF16),
        compiler_params=_params("arbitrary", "arbitrary"),
        name="merge",
    )(o, y, w_pa, w_pb, p2, p2)


def _ffn_up_epilogue(prods, extras, o_ref, i, *, cfg):
    cw_ref, = extras
    tm, tn_out = o_ref.shape
    nsub = len(prods) // 2
    tn = tn_out // nsub
    j = pl.program_id(0)
    is_ctx = i < cfg.tok_ctx // tm
    gw = cfg.grid_w
    lane = lax.broadcasted_iota(jnp.int32, (tm, tn), 1)
    t = lax.broadcasted_iota(jnp.int32, (tm, tn), 0) & (cfg.seq - 1)
    zeros = jnp.zeros((gw, tn), F32)
    for u in range(nsub):
        g, v = prods[2 * u], prods[2 * u + 1]
        cols = slice(u * tn, (u + 1) * tn)
        w = cw_ref[:, cols]
        gm = jnp.where(is_ctx, jnp.where(t == 0, 0.0, pltpu.roll(g, 1, 0)),
                       jnp.concatenate([zeros, g[:tm - gw]], axis=0))
        gp = jnp.where(is_ctx, jnp.where(t == cfg.seq - 1, 0.0, pltpu.roll(g, tm - 1, 0)),
                       jnp.concatenate([g[gw:], zeros], axis=0))
        y = gm * w[0:1] + g * w[1:2] + gp * w[2:3]
        valid = lane + (j * tn_out + u * tn) < cfg.d_ff
        o_ref[:, cols] = jnp.where(valid, _silu(y) * v, 0.0).astype(o_ref.dtype)


def _resid_body(a_ref, w_ref, x_ref, gt_ref, o_ref, *, nk):
    p = _dot(a_ref[...], w_ref[...])
    if nk == 1:
        o_ref[...] = x_ref[...] + gt_ref[...] * p
        return
    k = pl.program_id(2)

    @pl.when(k == 0)
    def _():
        o_ref[...] = p

    @pl.when(jnp.logical_and(k > 0, k < nk - 1))
    def _():
        o_ref[...] += p

    @pl.when(k == nk - 1)
    def _():
        o_ref[...] = x_ref[...] + gt_ref[...] * (o_ref[...] + p)


def _resid(a, w, layer, x, mod3, gt_idx, tk, cfg, name):
    tok, kdim = a.shape
    d = w.shape[2]
    tm, tn = cfg.tm, min(cfg.tn, d)
    nk = kdim // tk
    return pl.pallas_call(
        functools.partial(_resid_body, nk=nk),
        grid=(tok // tm, d // tn, nk),
        in_specs=[pl.BlockSpec((tm, tk), lambda i, j, k: (i, k)),
                  pl.BlockSpec((None, tk, tn), lambda i, j, k: (layer, k, j)),
                  pl.BlockSpec((tm, tn), lambda i, j, k: (i, j)),
                  pl.BlockSpec((None, 1, tn), lambda i, j, k: (_mod_row(i, cfg, tm) * 6 + gt_idx, 0, j))],
        out_specs=pl.BlockSpec((tm, tn), lambda i, j, k: (i, j)),
        out_shape=jax.ShapeDtypeStruct((tok, d), F32),
        compiler_params=_params("arbitrary", "arbitrary", "arbitrary"),
        name=name,
    )(a, w, x, mod3)


def _sc_body(xs_ref, bs_ref, cs_ref, w_ref, o_ref, *, cfg):
    tm, tn = o_ref.shape
    i = pl.program_id(0)
    per = jnp.where(i < cfg.tok_ctx // tm, cfg.seq, cfg.grid_w)
    u = cs_ref[...] * xs_ref[...]
    t = lax.broadcasted_iota(jnp.int32, (tm, tn), 0) & (per - 1)
    um = jnp.where(t == 0, 0.0, pltpu.roll(u, 1, 0))
    up = jnp.where(t == per - 1, 0.0, pltpu.roll(u, tm - 1, 0))
    w = w_ref[...]
    o_ref[...] = (bs_ref[...] * (um * w[0:1] + u * w[1:2] + up * w[2:3])).astype(o_ref.dtype)


def _sc_mixer(p2, cw, cfg):
    tok = p2.shape[0]
    ws = cfg.w_sc
    tm, tn = cfg.tm, min(512, ws)
    nb = ws // tn
    return pl.pallas_call(
        functools.partial(_sc_body, cfg=cfg),
        grid=(tok // tm, nb),
        in_specs=[pl.BlockSpec((tm, tn), lambda i, j: (i, j)),
                  pl.BlockSpec((tm, tn), lambda i, j: (i, nb + j)),
                  pl.BlockSpec((tm, tn), lambda i, j: (i, 2 * nb + j)),
                  pl.BlockSpec((3, tn), lambda i, j: (0, j))],
        out_specs=pl.BlockSpec((tm, tn), lambda i, j: (i, j)),
        out_shape=jax.ShapeDtypeStruct((tok, ws), BF16),
        compiler_params=_params("arbitrary", "arbitrary"),
        name="sc_mixer",
    )(p2, p2, p2, cw)


def _dn_body(*refs, n, per, heads, hb, has_s0, emit_state, group):
    it = iter(refs)
    q_ref, k_ref, v_ref, z_ref = next(it), next(it), next(it), next(it)
    wq_ref, wk_ref, wv_ref = next(it), next(it), next(it)
    bgc_ref, bgr_ref, alog_ref, dtb_ref, og_ref = next(it), next(it), next(it), next(it), next(it)
    s0_ref = next(it) if has_s0 else None
    o_ref = next(it)
    st_ref = next(it) if emit_state else None
    qs, ks, vs, of_s, ob_s, qw_s, n_s, oc_s, eg_s = it

    nc = n // CHUNK
    width = hb * LANES
    hblk = pl.program_id(1)
    head_cols = [slice(b * LANES, (b + 1) * LANES) for b in range(hb)]

    tok = lax.broadcasted_iota(jnp.int32, (n, width), 0) & (per - 1)
    first, last = tok == 0, tok == per - 1

    def conv_silu(x_ref, w_ref):
        x, w = x_ref[...], w_ref[...]
        xm = jnp.where(first, 0.0, pltpu.roll(x, 1, 0))
        xp = jnp.where(last, 0.0, pltpu.roll(x, n - 1, 0))
        return _silu(xm * w[0:1] + x * w[1:2] + xp * w[2:3])

    q = conv_silu(q_ref, wq_ref)
    k = conv_silu(k_ref, wk_ref)
    for sl in head_cols:
        qh, kh = q[:, sl], k[:, sl]
        qs[:, sl] = qh * lax.rsqrt(jnp.sum(qh * qh, axis=-1, keepdims=True) + EPS) * (LANES ** -0.5)
        ks[:, sl] = kh * lax.rsqrt(jnp.sum(kh * kh, axis=-1, keepdims=True) + EPS)
    vs[...] = conv_silu(v_ref, wv_ref)

    head_iota = lax.broadcasted_iota(jnp.int32, (2, heads), 1)
    ea, dtb = [], []
    for b in range(hb):
        sel = head_iota == hblk * hb + b
        ea.append(jnp.exp(jnp.sum(jnp.where(sel, alog_ref[...], 0.0), axis=1, keepdims=True)))
        dtb.append(jnp.sum(jnp.where(sel, dtb_ref[...], 0.0), axis=1, keepdims=True))

    nbg = bgc_ref.shape[-1]
    lane_bg = lax.broadcasted_iota(jnp.int32, (CHUNK, nbg), 1)
    sub = lax.broadcasted_iota(jnp.int32, (CHUNK, CHUNK), 0)
    lane = lax.broadcasted_iota(jnp.int32, (CHUNK, CHUNK), 1)
    eye = (sub == lane).astype(F32)
    blk16 = (sub // 16) == (lane // 16)
    blk32 = (sub // 32) == (lane // 32)
    off32 = jnp.logical_and(blk32, jnp.logical_not(blk16))
    vis = (lane <= sub, lane >= sub)
    vis_t = (sub <= lane, sub >= lane)
    strict = (lane < sub, lane > sub)

    def stage_group(gi, carry):
        units = []
        for b in range(hb):
            for j in range(group):
                c = gi * group + j
                r0 = c * CHUNK
                if not isinstance(r0, int):
                    r0 = pl.multiple_of(r0, CHUNK)
                rows = pl.ds(r0, CHUNK)
                qc, kc, vc = qs[rows, head_cols[b]], ks[rows, head_cols[b]], vs[rows, head_cols[b]]
                kb = kc.astype(BF16)
                kk = _dot_nt(kb, kb)
                qk = _dot_nt(qc.astype(BF16), kb)
                bgc = bgc_ref[c]
                hd = hblk * hb + b
                for d in range(2):
                    col = lambda g: jnp.sum(jnp.where(lane_bg == g * heads + hd, bgc, 0.0), axis=1, keepdims=True)
                    beta = jax.nn.sigmoid(col(d))
                    g_col = -ea[b][d:d + 1] * _softplus(col(2 + d) + dtb[b][d:d + 1])
                    g_row = -ea[b][d:d + 1] * _softplus(bgr_ref[2 + d, b, c] + dtb[b][d:d + 1])
                    m = vis[d]
                    gc_col = jnp.sum(jnp.where(m, jnp.broadcast_to(g_row, (CHUNK, CHUNK)), 0.0), axis=1, keepdims=True)
                    gc_row = jnp.sum(jnp.where(vis_t[d], jnp.broadcast_to(g_col, (CHUNK, CHUNK)), 0.0), axis=0, keepdims=True)
                    g_last = jnp.sum(g_row, axis=1, keepdims=True)
                    decay = jnp.where(m, jnp.exp(jnp.where(m, gc_col - gc_row, 0.0)), 0.0)
                    e_gc = jnp.exp(gc_col)
                    units.append(dict(
                        b=b, c=c, d=d,
                        a=jnp.where(strict[d], kk * decay * beta, 0.0),
                        x=jnp.concatenate([vc * beta, kc * (beta * e_gc)], axis=1),
                        intra=jnp.where(m, qk * decay, 0.0).astype(BF16),
                        k_dec=(kc * jnp.exp(g_last - gc_col)).astype(BF16),
                        q_dec=qc * e_gc,
                        eg=jnp.broadcast_to(jnp.exp(g_last), (1, LANES))))
        a_d = [jnp.where(blk16, u["a"], 0.0).astype(BF16) for u in units]
        p2 = [_dot(x, x).astype(BF16) for x in a_d]
        t = [eye - x.astype(F32) for x in a_d]
        t = [x + _dot(x.astype(BF16), p) for x, p in zip(t, p2)]
        p4 = [_dot(p, p).astype(BF16) for p in p2]
        t = [x + _dot(x.astype(BF16), p) for x, p in zip(t, p4)]
        p8 = [_dot(p, p).astype(BF16) for p in p4]
        t = [x + _dot(x.astype(BF16), p) for x, p in zip(t, p8)]
        mm = [_dotb(x, jnp.where(off32, u["a"], 0.0)) for x, u in zip(t, units)]
        t = [x - _dotb(y, x) for x, y in zip(t, mm)]
        mm = [_dotb(x, jnp.where(blk32, 0.0, u["a"])) for x, u in zip(t, units)]
        t = [x - _dotb(y, x) for x, y in zip(t, mm)]
        xb = [_dotb(x, u["x"]).astype(BF16) for x, u in zip(t, units)]
        iw = [_dot(u["intra"], x) for u, x in zip(units, xb)]
        kt = [_dot_tn(u["k_dec"], x) for u, x in zip(units, xb)]
        for u, iwu, ktu in zip(units, iw, kt):
            b, c, d = u["b"], u["c"], u["d"]
            oc_s[b, d, c] = iwu[:, :LANES]
            qw_s[b, d, c, 0:CHUNK, :] = (u["q_dec"] - iwu[:, LANES:]).astype(BF16)
            qw_s[b, d, c, CHUNK:, :] = ktu[:, LANES:].astype(BF16)
            n_s[b, d, c] = ktu[:, :LANES]
            eg_s[b, d, c] = u["eg"]
        return carry

    if nc == group:
        stage_group(0, 0)
    else:
        lax.fori_loop(0, nc // group, stage_group, 0)

    chains = [(b, d) for b in range(hb) for d in range(2)]
    if has_s0:
        s_init = tuple(s0_ref[d, b] for b, d in chains)
    else:
        s_init = tuple(jnp.zeros((LANES, LANES), F32) for _ in chains)

    def scan_step(t, carry):
        res = []
        for (b, d), s in zip(chains, carry):
            c = t if d == 0 else nc - 1 - t
            res.append(_dot(qw_s[b, d, c], s.astype(BF16)))
        out = []
        for (b, d), s, r in zip(chains, carry, res):
            c = t if d == 0 else nc - 1 - t
            r0 = c * CHUNK
            if not isinstance(r0, int):
                r0 = pl.multiple_of(r0, CHUNK)
            acc = of_s if d == 0 else ob_s
            acc[pl.ds(r0, CHUNK), head_cols[b]] = r[:CHUNK] + oc_s[b, d, c]
            out.append(s * eg_s[b, d, c] - r[CHUNK:] + n_s[b, d, c])
        return tuple(out)

    if nc == group:
        s_fin = s_init
        for t in range(nc):
            s_fin = scan_step(t, s_fin)
    else:
        s_fin = lax.fori_loop(0, nc, scan_step, s_init)

    o = of_s[...] + ob_s[...]
    z = _silu(z_ref[...])
    og = og_ref[...]
    for sl in head_cols:
        oh = o[:, sl]
        y = oh * lax.rsqrt(jnp.mean(oh * oh, axis=-1, keepdims=True) + EPS) * og
        o_ref[:, sl] = (y * z[:, sl]).astype(o_ref.dtype)
    if emit_state:
        for (b, d), s in zip(chains, s_fin):
            st_ref[d, b] = s


def _deltanet(p1, bg, conv_qkv, a_log, dt_bias, onorm_g, s0, cfg, *, latent):
    hh, hb = cfg.heads, cfg.dn_heads
    tok, nbg = bg.shape
    n = cfg.dec_seq if latent else cfg.seq
    nseq = cfg.dec_batch if latent else cfg.batch
    blk0 = cfg.tok_ctx // n if latent else 0
    per = cfg.grid_w if latent else cfg.seq
    nc = n // CHUNK
    group = min(4, nc)
    width = hb * LANES
    nhb = hh // hb
    bgc = bg.reshape(tok // CHUNK, CHUNK, nbg)
    bgr = bg[:, :4 * hh].T.reshape(4, hh, tok // CHUNK, 1, CHUNK)

    col = lambda g: pl.BlockSpec((n, width), lambda s, h: (blk0 + s, g * nhb + h))
    cw = lambda g: pl.BlockSpec((3, width), lambda s, h: (0, g * nhb + h))
    in_specs = [col(0), col(1), col(2), col(3), cw(0), cw(1), cw(2),
                pl.BlockSpec((nc, CHUNK, nbg), lambda s, h: (blk0 + s, 0, 0)),
                pl.BlockSpec((4, hb, nc, 1, CHUNK), lambda s, h: (0, h, blk0 + s, 0, 0)),
                pl.BlockSpec((2, hh), lambda s, h: (0, 0)),
                pl.BlockSpec((2, hh), lambda s, h: (0, 0)),
                pl.BlockSpec((1, LANES), lambda s, h: (0, 0))]
    args = [p1, p1, p1, p1, conv_qkv, conv_qkv, conv_qkv, bgc, bgr, a_log, dt_bias, onorm_g.reshape(1, LANES)]
    st_spec = pl.BlockSpec((None, 2, hb, LANES, LANES), lambda s, h: (s, 0, h, 0, 0))
    if latent:
        in_specs.append(st_spec)
        args.append(s0)
    out_specs = [pl.BlockSpec((n, width), lambda s, h: (s, h))]
    out_shape = [jax.ShapeDtypeStruct((nseq * n, hh * LANES), BF16)]
    if not latent:
        out_specs.append(st_spec)
        out_shape.append(jax.ShapeDtypeStruct((nseq, 2, hh, LANES, LANES), F32))
    scratch = [pltpu.VMEM((n, width), F32)] * 5 + [
        pltpu.VMEM((hb, 2, nc, CHUNK + LANES, LANES), BF16), pltpu.VMEM((hb, 2, nc, LANES, LANES), F32),
        pltpu.VMEM((hb, 2, nc, CHUNK, LANES), F32), pltpu.VMEM((hb, 2, nc, 1, LANES), F32)]
    return pl.pallas_call(
        functools.partial(_dn_body, n=n, per=per, heads=hh, hb=hb, has_s0=latent, emit_state=not latent,
                          group=group),
        grid=(nseq, nhb),
        in_specs=in_specs,
        out_specs=out_specs,
        out_shape=out_shape,
        scratch_shapes=scratch,
        compiler_params=_params("arbitrary", "arbitrary"),
        name="deltanet_lat" if latent else "deltanet_ctx",
    )(*args)


def _forward(cfg, x_prompt, x_sample, state_dn, c, c_ctx, norm1_g, norm2_g, w_ada, b_ada, w_in,
             conv_qkv, a_log, dt_bias, onorm_g, conv_sc, w_pa, w_pb, w_o, w_up, conv_ff,
             w_down, final_g):
    d, hh = cfg.d, cfg.heads
    w1_n = 4 * cfg.w_hd
    nbg = 4 * hh
    x = jnp.concatenate([x_prompt.reshape(cfg.tok_ctx, d), x_sample.reshape(-1, d)], axis=0)
    cvec = jnp.zeros((MOD_ROWS, d), F32).at[0].set(c_ctx).at[1:1 + cfg.dec_batch].set(c)
    mod = _ada(cvec, w_ada, b_ada)
    ff_extra = cfg.ff_pad - cfg.d_ff
    w_in_t = jnp.swapaxes(w_in, 1, 2)
    w_pa_b, w_pb_b, w_o_b = w_pa.astype(BF16), w_pb.astype(BF16), w_o.astype(BF16)
    w_down_b = jnp.pad(w_down.astype(BF16), ((0, 0), (0, ff_extra), (0, 0)))
    states = []
    n2 = 3 * cfg.w_sc + 2 * d
    ff_tn, ff_sub = cfg.ff_tn, FFN_SUBTILES
    ff_w = ff_tn // ff_sub
    assert cfg.tm == cfg.dec_seq and cfg.tm % cfg.seq == 0 and cfg.d_ff % ff_w == 0 and cfg.ff_pad % ff_tn == 0
    ff_weights = [WCols(w_up, 0, half * cfg.d_ff + u * ff_w, 0, 0, False, ff_sub)
                  for u in range(ff_sub) for half in range(2)]
    for l in range(cfg.depth):
        mod3 = mod[l].reshape(MOD_ROWS * 6, 1, d)
        h = _norm_mod(x, norm1_g[l], mod3, 0, 1, cfg)
        p1 = _wres_matmul("proj_qkvz", [h], [WCols(w_in_t, l, 0, 0, 0, True, 1)], [], _store_epilogue,
                          w1_n, F32, cfg.tm, cfg.tn, cfg.tn, cfg.kc)
        bg = _bg_proj(h, w_in_t, l, w1_n, nbg, cfg)
        p2 = _wres_matmul("proj_sc_gates", [h], [WCols(w_in_t, l, w1_n, nbg, 0, True, 1)], [], _store_epilogue,
                          n2, F32, cfg.tm, cfg.tn, cfg.tn, cfg.kc)
        o_ctx, st = _deltanet(p1, bg, conv_qkv[l], a_log[l], dt_bias[l], onorm_g[l], None, cfg, latent=False)
        o_lat, = _deltanet(p1, bg, conv_qkv[l], a_log[l], dt_bias[l], onorm_g[l], state_dn[:, l], cfg, latent=True)
        states.append(st)
        o = jnp.concatenate([o_ctx, o_lat], axis=0)
        y_sc = _sc_mixer(p2, conv_sc[l], cfg)
        merged = _merge(o, y_sc, w_pa_b, w_pb_b, l, p2, 3 * cfg.w_sc, cfg)
        x = _resid(merged, w_o_b, l, x, mod3, 2, d, cfg, "out_proj")
        h = _norm_mod(x, norm2_g[l], mod3, 3, 4, cfg)
        f = _wres_matmul(
            "ffn_up", [h], [wc._replace(layer=l) for wc in ff_weights],
            [(conv_ff, (None, 3, ff_tn), lambda j, i, l=l: (l, 0, j))],
            functools.partial(_ffn_up_epilogue, cfg=cfg), cfg.ff_pad, BF16, cfg.tm, ff_w, ff_tn, cfg.kc)
        x = _resid(f, w_down_b, l, x, mod3, 5, cfg.ff_tk, cfg, "ffn_down")
    y_prompt = _final_norm(x, final_g, 0, cfg.tok_ctx, cfg).reshape(x_prompt.shape)
    y_sample = _final_norm(x, final_g, cfg.tok_ctx, cfg.tok - cfg.tok_ctx, cfg).reshape(x_sample.shape)
    return y_prompt, y_sample, jnp.stack(states, axis=1)


def kernel(x_prompt, x_sample, state_dn, c, c_ctx, norm1_g, norm2_g, w_ada, b_ada, w_in, conv_qkv, a_log, dt_bias, onorm_g, conv_sc, w_pa, w_pb, w_o, w_up, conv_ff, w_down, final_g):
    return _forward(CFG, x_prompt, x_sample, state_dn, c, c_ctx, norm1_g, norm2_g, w_ada, b_ada, w_in,
                    conv_qkv, a_log, dt_bias, onorm_g, conv_sc, w_pa, w_pb, w_o, w_up, conv_ff,
                    w_down, final_g)
```

```python
import functools
from typing import NamedTuple

import jax
import jax.numpy as jnp
from jax import lax
from jax.experimental import pallas as pl
from jax.experimental.pallas import tpu as pltpu

F32 = jnp.float32
BF16 = jnp.bfloat16
EPS = 1e-6
CHUNK = 64
LANES = 128
MOD_ROWS = 8
V7X_VMEM_BYTES = 64 * 1024 * 1024
VMEM_LIMIT_BYTES = V7X_VMEM_BYTES - 6 * 1024 * 1024


class Cfg(NamedTuple):
    d: int
    batch: int
    seq: int
    depth: int
    dec_batch: int
    dec_seq: int
    grid_w: int
    heads: int
    w_sc: int
    d_ff: int
    tm: int
    tn: int
    ff_tn: int
    ff_tk: int
    dn_heads: int
    kc: int

    @property
    def w_hd(self):
        return self.heads * LANES

    @property
    def tok_ctx(self):
        return self.batch * self.seq

    @property
    def tok(self):
        return self.batch * self.seq + self.dec_batch * self.dec_seq

    @property
    def ff_pad(self):
        return -(-self.d_ff // self.ff_tk) * self.ff_tk


CFG = Cfg(d=4096, batch=32, seq=256, depth=2, dec_batch=2, dec_seq=1024, grid_w=64, heads=16,
          w_sc=2048, d_ff=11008, tm=1024, tn=1024, ff_tn=1024, ff_tk=2816, dn_heads=4, kc=512)
FFN_SUBTILES = 4


def _params(*sem):
    return pltpu.CompilerParams(dimension_semantics=sem, vmem_limit_bytes=VMEM_LIMIT_BYTES)


def _mod_row(i, cfg, tm):
    nct = cfg.tok_ctx // tm
    return jnp.where(i < nct, 0, 1 + (i - nct) // (cfg.dec_seq // tm))


def _silu(x):
    return x * jax.nn.sigmoid(x)


def _softplus(x):
    return jnp.maximum(x, 0.0) + jnp.log1p(jnp.exp(-jnp.abs(x)))


def _dot(a, b):
    return jnp.dot(a, b, preferred_element_type=F32)


def _dotb(a, b):
    return _dot(a.astype(BF16), b.astype(BF16))


def _dot_nt(a, b):
    return lax.dot_general(a, b, (((1,), (1,)), ((), ())), preferred_element_type=F32)


def _dot_tn(a, b):
    return lax.dot_general(a, b, (((0,), (0,)), ((), ())), preferred_element_type=F32)


def _ada_body(c_ref, w_ref, b_ref, o_ref):
    s = _silu(c_ref[...]).astype(BF16)
    o_ref[...] = _dot(s, w_ref[...].astype(BF16)) + b_ref[...]


def _ada(cvec, w_ada, b_ada):
    depth, d, n = w_ada.shape
    tn = 512
    return pl.pallas_call(
        _ada_body,
        grid=(depth, n // tn),
        in_specs=[pl.BlockSpec((MOD_ROWS, d), lambda l, j: (0, 0)),
                  pl.BlockSpec((None, d, tn), lambda l, j: (l, 0, j)),
                  pl.BlockSpec((None, 1, tn), lambda l, j: (l, 0, j))],
        out_specs=pl.BlockSpec((None, MOD_ROWS, tn), lambda l, j: (l, 0, j)),
        out_shape=jax.ShapeDtypeStruct((depth, MOD_ROWS, n), F32),
        compiler_params=_params("arbitrary", "arbitrary"),
        name="ada",
    )(cvec, w_ada, b_ada.reshape(depth, 1, n))


def _norm_mod_body(x_ref, g_ref, sh_ref, sc_ref, o_ref):
    x = x_ref[...]
    y = x * lax.rsqrt(jnp.mean(x * x, axis=-1, keepdims=True) + EPS) * g_ref[...]
    o_ref[...] = (y * (1.0 + sc_ref[...]) + sh_ref[...]).astype(o_ref.dtype)


def _norm_mod(x, g, mod3, sh_idx, sc_idx, cfg):
    tok, d = x.shape
    tm = cfg.seq
    spec_mod = lambda idx: pl.BlockSpec((None, 1, d), lambda i: (_mod_row(i, cfg, tm) * 6 + idx, 0, 0))
    return pl.pallas_call(
        _norm_mod_body,
        grid=(tok // tm,),
        in_specs=[pl.BlockSpec((tm, d), lambda i: (i, 0)),
                  pl.BlockSpec((1, d), lambda i: (0, 0)),
                  spec_mod(sh_idx), spec_mod(sc_idx)],
        out_specs=pl.BlockSpec((tm, d), lambda i: (i, 0)),
        out_shape=jax.ShapeDtypeStruct((tok, d), BF16),
        compiler_params=_params("arbitrary"),
        name="norm_mod",
    )(x, g.reshape(1, d), mod3, mod3)


def _final_norm_body(x_ref, g_ref, o_ref):
    x = x_ref[...]
    o_ref[...] = x * lax.rsqrt(jnp.mean(x * x, axis=-1, keepdims=True) + EPS) * g_ref[...]


def _final_norm(x, g, row0, rows, cfg):
    d = x.shape[1]
    tm = cfg.seq
    off = row0 // tm
    return pl.pallas_call(
        _final_norm_body,
        grid=(rows // tm,),
        in_specs=[pl.BlockSpec((tm, d), lambda i: (i + off, 0)),
                  pl.BlockSpec((1, d), lambda i: (0, 0))],
        out_specs=pl.BlockSpec((tm, d), lambda i: (i, 0)),
        out_shape=jax.ShapeDtypeStruct((rows, d), F32),
        compiler_params=_params("arbitrary"),
        name="final_norm",
    )(x, g.reshape(1, d))


class WCols(NamedTuple):
    w: jax.Array
    layer: int
    col0: int
    act: int
    stride: int


def _wres_body(*refs, n_act, weights, n_extra, epilogue, kc, n_cast):
    acts = refs[:n_act]
    w_refs = refs[n_act:n_act + len(weights)]
    pos = n_act + len(weights)
    extras = refs[pos:pos + n_extra]
    o_ref = refs[pos + n_extra]
    scrs = refs[pos + n_extra + 1:]
    s = pl.program_id(1)

    @pl.when(s < n_cast)
    def _():
        rows = pl.ds(pl.multiple_of(s * kc, kc), kc)
        for w_ref, scr in zip(w_refs, scrs):
            scr[rows, :] = w_ref[...].astype(BF16)

    @pl.when(s >= n_cast)
    def _():
        prods = [_dot(acts[wc.act][...], scr[...]) for wc, scr in zip(weights, scrs)]
        epilogue(prods, extras, o_ref, s - n_cast)


def _wres_matmul(name, acts, weights, extras, epilogue, n_out, out_dtype, tm, tn, tn_out, kc):
    tok = acts[0].shape[0]
    kdim = acts[0].shape[1]
    n_cast = kdim // kc
    assert kdim % kc == 0 and all(a.shape[1] == kdim for a in acts)
    row = lambda s: jnp.maximum(s - n_cast, 0)
    chunk = lambda s: jnp.minimum(s, n_cast - 1)
    in_specs = [pl.BlockSpec((tm, kdim), lambda j, s: (row(s), 0)) for a in acts]
    args = list(acts)
    scratch = []
    for wc in weights:
        assert wc.col0 % tn == 0 and wc.w.shape[1] == kdim
        c0, layer, stride = wc.col0 // tn, wc.layer, wc.stride
        last = pl.cdiv(wc.w.shape[2], tn) - 1
        in_specs.append(pl.BlockSpec(
            (None, kc, tn),
            lambda j, s, c0=c0, layer=layer, stride=stride, last=last:
            (layer, chunk(s), jnp.minimum(c0 + j * stride, last))))
        args.append(wc.w)
        scratch.append(pltpu.VMEM((kdim, tn), BF16))
    for arr, block, index_map in extras:
        in_specs.append(pl.BlockSpec(block, lambda j, s, index_map=index_map: index_map(j, row(s))))
        args.append(arr)
    return pl.pallas_call(
        functools.partial(_wres_body, n_act=len(acts), weights=tuple(wc._replace(w=None) for wc in weights),
                          n_extra=len(extras), epilogue=epilogue, kc=kc, n_cast=n_cast),
        grid=(n_out // tn_out, n_cast + tok // tm),
        in_specs=in_specs,
        out_specs=pl.BlockSpec((tm, tn_out), lambda j, s: (row(s), j)),
        out_shape=jax.ShapeDtypeStruct((tok, n_out), out_dtype),
        scratch_shapes=scratch,
        compiler_params=_params("arbitrary", "arbitrary"),
        name=name,
    )(*args)


def _store_epilogue(prods, extras, o_ref, i):
    o_ref[...] = prods[0].astype(o_ref.dtype)


def _proj_body(*refs, kc, n_cast, shift):
    if shift:
        a_ref, w_ref, nxt_ref, o_ref, scr = refs
    else:
        a_ref, w_ref, o_ref, scr = refs
    sweep, s = pl.program_id(0), pl.program_id(1)
    slot = sweep % 2

    def cast_chunk():
        rows = pl.ds(pl.multiple_of(jnp.minimum(s, n_cast - 1) * kc, kc), kc)
        w = w_ref[...]
        if shift:
            w = jnp.concatenate([w[shift:], nxt_ref[...]], axis=0)
        scr[slot, rows, :] = w.T.astype(BF16)

    @pl.when(sweep == 0)
    def _():
        cast_chunk()

    @pl.when(sweep > 0)
    def _():
        cast_chunk()
        o_ref[...] = _dot(a_ref[...], scr[1 - slot]).astype(o_ref.dtype)


def _proj(name, a, w_t, layer, row0, shift, n_out, cfg):
    tok, kdim = a.shape
    tm, tn, kc = cfg.tm, cfg.tn, cfg.kc
    n_cast, ni, nj = kdim // kc, tok // tm, n_out // tn
    assert kdim % kc == 0 and n_cast <= ni and row0 % tn == 0 and n_out % tn == 0
    c0 = row0 // tn
    tile = lambda t: jnp.minimum(t, nj - 1)
    chunk = lambda t, s: jnp.where(t == nj, n_cast - 1, jnp.minimum(s, n_cast - 1))
    row = lambda t, s: jnp.where(t == 0, 0, s)
    in_specs = [pl.BlockSpec((tm, kdim), lambda t, s: (row(t, s), 0)),
                pl.BlockSpec((None, tn, kc), lambda t, s: (layer, c0 + tile(t), chunk(t, s)))]
    args = [a, w_t]
    if shift:
        assert tn % shift == 0 and shift % 8 == 0
        per = tn // shift
        in_specs.append(pl.BlockSpec((None, shift, kc),
                                     lambda t, s: (layer, (c0 + tile(t) + 1) * per, chunk(t, s))))
        args.append(w_t)
    return pl.pallas_call(
        functools.partial(_proj_body, kc=kc, n_cast=n_cast, shift=shift),
        grid=(nj + 1, ni),
        in_specs=in_specs,
        out_specs=pl.BlockSpec((tm, tn), lambda t, s: (row(t, s), jnp.maximum(t - 1, 0))),
        out_shape=jax.ShapeDtypeStruct((tok, n_out), F32),
        scratch_shapes=[pltpu.VMEM((2, kdim, tn), BF16)],
        compiler_params=_params("arbitrary", "arbitrary"),
        name=name,
    )(*args)


def _bg_body(a_ref, w_ref, o_ref):
    o_ref[...] = _dot_nt(a_ref[...], w_ref[...].astype(BF16))


def _bg_proj(h, w_t, layer, row0, nbg, cfg):
    tok, k = h.shape
    tm = cfg.tm
    assert row0 % nbg == 0
    return pl.pallas_call(
        _bg_body,
        grid=(tok // tm,),
        in_specs=[pl.BlockSpec((tm, k), lambda i: (i, 0)),
                  pl.BlockSpec((None, nbg, k), lambda i: (layer, row0 // nbg, 0))],
        out_specs=pl.BlockSpec((tm, nbg), lambda i: (i, 0)),
        out_shape=jax.ShapeDtypeStruct((tok, nbg), F32),
        compiler_params=_params("arbitrary"),
        name="proj_bg",
    )(h, w_t)


def _merge_body(o_ref, y_ref, wa_ref, wb_ref, ga_ref, gb_ref, out_ref):
    pa = _dot(o_ref[...], wa_ref[...])
    pb = _dot(y_ref[...], wb_ref[...])
    out_ref[...] = (jax.nn.sigmoid(ga_ref[...]) * pa + jax.nn.sigmoid(gb_ref[...]) * pb).astype(out_ref.dtype)


def _merge(o, y, w_pa, w_pb, layer, p2, gate_col0, cfg):
    tok = o.shape[0]
    d = w_pa.shape[2]
    tm, tn = cfg.tm, min(cfg.tn // 2, d)
    ga0 = gate_col0 // tn
    gb0 = (gate_col0 + d) // tn
    return pl.pallas_call(
        _merge_body,
        grid=(tok // tm, d // tn),
        in_specs=[pl.BlockSpec((tm, o.shape[1]), lambda i, j: (i, 0)),
                  pl.BlockSpec((tm, y.shape[1]), lambda i, j: (i, 0)),
                  pl.BlockSpec((None, w_pa.shape[1], tn), lambda i, j: (layer, 0, j)),
                  pl.BlockSpec((None, w_pb.shape[1], tn), lambda i, j: (layer, 0, j)),
                  pl.BlockSpec((tm, tn), lambda i, j: (i, ga0 + j)),
                  pl.BlockSpec((tm, tn), lambda i, j: (i, gb0 + j))],
        out_specs=pl.BlockSpec((tm, tn), lambda i, j: (i, j)),
        out_shape=jax.ShapeDtypeStruct((tok, d), BF16),
        compiler_params=_params("arbitrary", "arbitrary"),
        name="merge",
    )(o, y, w_pa, w_pb, p2, p2)


def _ffn_up_epilogue(prods, extras, o_ref, i, *, cfg):
    cw_ref, = extras
    tm, tn_out = o_ref.shape
    nsub = len(prods) // 2
    tn = tn_out // nsub
    j = pl.program_id(0)
    is_ctx = i < cfg.tok_ctx // tm
    gw = cfg.grid_w
    lane = lax.broadcasted_iota(jnp.int32, (tm, tn), 1)
    t = lax.broadcasted_iota(jnp.int32, (tm, tn), 0) & (cfg.seq - 1)
    zeros = jnp.zeros((gw, tn), F32)
    for u in range(nsub):
        g, v = prods[2 * u], prods[2 * u + 1]
        cols = slice(u * tn, (u + 1) * tn)
        w = cw_ref[:, cols]
        gm = jnp.where(is_ctx, jnp.where(t == 0, 0.0, pltpu.roll(g, 1, 0)),
                       jnp.concatenate([zeros, g[:tm - gw]], axis=0))
        gp = jnp.where(is_ctx, jnp.where(t == cfg.seq - 1, 0.0, pltpu.roll(g, tm - 1, 0)),
                       jnp.concatenate([g[gw:], zeros], axis=0))
        y = gm * w[0:1] + g * w[1:2] + gp * w[2:3]
        valid = lane + (j * tn_out + u * tn) < cfg.d_ff
        o_ref[:, cols] = jnp.where(valid, _silu(y) * v, 0.0).astype(o_ref.dtype)


def _resid_body(a_ref, w_ref, x_ref, gt_ref, o_ref, *, nk):
    p = _dot(a_ref[...], w_ref[...])
    if nk == 1:
        o_ref[...] = x_ref[...] + gt_ref[...] * p
        return
    k = pl.program_id(2)

    @pl.when(k == 0)
    def _():
        o_ref[...] = p

    @pl.when(jnp.logical_and(k > 0, k < nk - 1))
    def _():
        o_ref[...] += p

    @pl.when(k == nk - 1)
    def _():
        o_ref[...] = x_ref[...] + gt_ref[...] * (o_ref[...] + p)


def _resid(a, w, layer, x, mod3, gt_idx, tk, cfg, name):
    tok, kdim = a.shape
    d = w.shape[2]
    tm, tn = cfg.tm, min(cfg.tn, d)
    nk = kdim // tk
    return pl.pallas_call(
        functools.partial(_resid_body, nk=nk),
        grid=(tok // tm, d // tn, nk),
        in_specs=[pl.BlockSpec((tm, tk), lambda i, j, k: (i, k)),
                  pl.BlockSpec((None, tk, tn), lambda i, j, k: (layer, k, j)),
                  pl.BlockSpec((tm, tn), lambda i, j, k: (i, j)),
                  pl.BlockSpec((None, 1, tn), lambda i, j, k: (_mod_row(i, cfg, tm) * 6 + gt_idx, 0, j))],
        out_specs=pl.BlockSpec((tm, tn), lambda i, j, k: (i, j)),
        out_shape=jax.ShapeDtypeStruct((tok, d), F32),
        compiler_params=_params("arbitrary", "arbitrary", "arbitrary"),
        name=name,
    )(a, w, x, mod3)


def _sc_body(xs_ref, bs_ref, cs_ref, w_ref, o_ref, *, cfg):
    tm, tn = o_ref.shape
    i = pl.program_id(0)
    per = jnp.where(i < cfg.tok_ctx // tm, cfg.seq, cfg.grid_w)
    u = cs_ref[...] * xs_ref[...]
    t = lax.broadcasted_iota(jnp.int32, (tm, tn), 0) & (per - 1)
    um = jnp.where(t == 0, 0.0, pltpu.roll(u, 1, 0))
    up = jnp.where(t == per - 1, 0.0, pltpu.roll(u, tm - 1, 0))
    w = w_ref[...]
    o_ref[...] = (bs_ref[...] * (um * w[0:1] + u * w[1:2] + up * w[2:3])).astype(o_ref.dtype)


def _sc_mixer(p2, cw, cfg):
    tok = p2.shape[0]
    ws = cfg.w_sc
    tm, tn = cfg.tm, min(512, ws)
    nb = ws // tn
    return pl.pallas_call(
        functools.partial(_sc_body, cfg=cfg),
        grid=(tok // tm, nb),
        in_specs=[pl.BlockSpec((tm, tn), lambda i, j: (i, j)),
                  pl.BlockSpec((tm, tn), lambda i, j: (i, nb + j)),
                  pl.BlockSpec((tm, tn), lambda i, j: (i, 2 * nb + j)),
                  pl.BlockSpec((3, tn), lambda i, j: (0, j))],
        out_specs=pl.BlockSpec((tm, tn), lambda i, j: (i, j)),
        out_shape=jax.ShapeDtypeStruct((tok, ws), BF16),
        compiler_params=_params("arbitrary", "arbitrary"),
        name="sc_mixer",
    )(p2, p2, p2, cw)


def _dn_body(*refs, n, per, heads, hb, has_s0, emit_state, group):
    it = iter(refs)
    q_ref, k_ref, v_ref, z_ref = next(it), next(it), next(it), next(it)
    wq_ref, wk_ref, wv_ref = next(it), next(it), next(it)
    bgc_ref, bgt_ref, a_row_ref, dtb_row_ref, a_col_ref, dtb_col_ref, og_ref = (next(it) for _ in range(7))
    s0_ref = next(it) if has_s0 else None
    o_ref = next(it)
    st_ref = next(it) if emit_state else None
    qs, ks, vs, of_s, ob_s, qw_s, n_s, oc_s, eg_s, gcol_s, grow_s = it

    nc = n // CHUNK
    width = hb * LANES
    hblk = pl.program_id(1)
    head_cols = [slice(b * LANES, (b + 1) * LANES) for b in range(hb)]

    tok = lax.broadcasted_iota(jnp.int32, (n, width), 0) & (per - 1)
    first, last = tok == 0, tok == per - 1

    def conv_silu(x_ref, w_ref):
        x, w = x_ref[...], w_ref[...]
        xm = jnp.where(first, 0.0, pltpu.roll(x, 1, 0))
        xp = jnp.where(last, 0.0, pltpu.roll(x, n - 1, 0))
        return _silu(xm * w[0:1] + x * w[1:2] + xp * w[2:3])

    q = conv_silu(q_ref, wq_ref)
    k = conv_silu(k_ref, wk_ref)
    for sl in head_cols:
        qh, kh = q[:, sl], k[:, sl]
        qs[:, sl] = qh * lax.rsqrt(jnp.sum(qh * qh, axis=-1, keepdims=True) + EPS) * (LANES ** -0.5)
        ks[:, sl] = kh * lax.rsqrt(jnp.sum(kh * kh, axis=-1, keepdims=True) + EPS)
    vs[...] = conv_silu(v_ref, wv_ref)

    nbg = bgc_ref.shape[-1]
    lane_bg = lax.broadcasted_iota(jnp.int32, (CHUNK, nbg), 1)
    is_beta = lane_bg < 2 * heads
    neg_a_row, dtb_row = -jnp.exp(a_row_ref[...]), dtb_row_ref[...]
    neg_a_col, dtb_col = -jnp.exp(a_col_ref[...]), dtb_col_ref[...]

    def gates(c, carry):
        bgc = bgc_ref[c]
        gcol_s[c] = jnp.where(is_beta, jax.nn.sigmoid(bgc), neg_a_row * _softplus(bgc + dtb_row))
        grow_s[c] = neg_a_col * _softplus(bgt_ref[c] + dtb_col)
        return carry

    if nc == group:
        for c in range(nc):
            gates(c, 0)
    else:
        lax.fori_loop(0, nc, gates, 0)
    sub = lax.broadcasted_iota(jnp.int32, (CHUNK, CHUNK), 0)
    lane = lax.broadcasted_iota(jnp.int32, (CHUNK, CHUNK), 1)
    eye = (sub == lane).astype(F32)
    blk16 = (sub // 16) == (lane // 16)
    blk32 = (sub // 32) == (lane // 32)
    off32 = jnp.logical_and(blk32, jnp.logical_not(blk16))
    vis = (lane <= sub, lane >= sub)
    vis_t = (sub <= lane, sub >= lane)
    strict = (lane < sub, lane > sub)

    def stage_group(gi, carry):
        units = []
        for b in range(hb):
            for j in range(group):
                c = gi * group + j
                r0 = c * CHUNK
                if not isinstance(r0, int):
                    r0 = pl.multiple_of(r0, CHUNK)
                rows = pl.ds(r0, CHUNK)
                qc, kc, vc = qs[rows, head_cols[b]], ks[rows, head_cols[b]], vs[rows, head_cols[b]]
                kb = kc.astype(BF16)
                kk = _dot_nt(kb, kb)
                qk = _dot_nt(qc.astype(BF16), kb)
                gall = gcol_s[c]
                hd = hblk * hb + b
                for d in range(2):
                    col = lambda g: jnp.sum(jnp.where(lane_bg == g * heads + hd, gall, 0.0), axis=1, keepdims=True)
                    beta = col(d)
                    g_col = col(2 + d)
                    g_row = grow_s[c, pl.ds((2 + d) * heads + hd, 1), :]
                    m = vis[d]
                    gc_col = jnp.sum(jnp.where(m, jnp.broadcast_to(g_row, (CHUNK, CHUNK)), 0.0), axis=1, keepdims=True)
                    gc_row = jnp.sum(jnp.where(vis_t[d], jnp.broadcast_to(g_col, (CHUNK, CHUNK)), 0.0), axis=0, keepdims=True)
                    g_last = jnp.sum(g_row, axis=1, keepdims=True)
                    decay = jnp.where(m, jnp.exp(jnp.where(m, gc_col - gc_row, 0.0)), 0.0)
                    e_gc = jnp.exp(gc_col)
                    units.append(dict(
                        b=b, c=c, d=d,
                        a=jnp.where(strict[d], kk * decay * beta, 0.0),
                        x=jnp.concatenate([vc * beta, kc * (beta * e_gc)], axis=1),
                        intra=jnp.where(m, qk * decay, 0.0).astype(BF16),
                        k_dec=(kc * jnp.exp(g_last - gc_col)).astype(BF16),
                        q_dec=qc * e_gc,
                        eg=jnp.broadcast_to(jnp.exp(g_last), (1, LANES))))
        a_d = [jnp.where(blk16, u["a"], 0.0).astype(BF16) for u in units]
        p2 = [_dot(x, x).astype(BF16) for x in a_d]
        t = [eye - x.astype(F32) for x in a_d]
        t = [x + _dot(x.astype(BF16), p) for x, p in zip(t, p2)]
        p4 = [_dot(p, p).astype(BF16) for p in p2]
        t = [x + _dot(x.astype(BF16), p) for x, p in zip(t, p4)]
        p8 = [_dot(p, p).astype(BF16) for p in p4]
        t = [x + _dot(x.astype(BF16), p) for x, p in zip(t, p8)]
        mm = [_dotb(x, jnp.where(off32, u["a"], 0.0)) for x, u in zip(t, units)]
        t = [x - _dotb(y, x) for x, y in zip(t, mm)]
        mm = [_dotb(x, jnp.where(blk32, 0.0, u["a"])) for x, u in zip(t, units)]
        t = [x - _dotb(y, x) for x, y in zip(t, mm)]
        xb = [_dotb(x, u["x"]).astype(BF16) for x, u in zip(t, units)]
        iw = [_dot(u["intra"], x) for u, x in zip(units, xb)]
        kt = [_dot_tn(u["k_dec"], x) for u, x in zip(units, xb)]
        for u, iwu, ktu in zip(units, iw, kt):
            b, c, d = u["b"], u["c"], u["d"]
            oc_s[b, d, c] = iwu[:, :LANES]
            qw_s[b, d, c, 0:CHUNK, :] = (u["q_dec"] - iwu[:, LANES:]).astype(BF16)
            qw_s[b, d, c, CHUNK:, :] = ktu[:, LANES:].astype(BF16)
            n_s[b, d, c] = ktu[:, :LANES]
            eg_s[b, d, c] = u["eg"]
        return carry

    if nc == group:
        stage_group(0, 0)
    else:
        lax.fori_loop(0, nc // group, stage_group, 0)

    chains = [(b, d) for b in range(hb) for d in range(2)]
    if has_s0:
        s_init = tuple(s0_ref[d, b] for b, d in chains)
    else:
        s_init = tuple(jnp.zeros((LANES, LANES), F32) for _ in chains)

    def scan_step(t, carry):
        res = []
        for (b, d), s in zip(chains, carry):
            c = t if d == 0 else nc - 1 - t
            res.append(_dot(qw_s[b, d, c], s.astype(BF16)))
        out = []
        for (b, d), s, r in zip(chains, carry, res):
            c = t if d == 0 else nc - 1 - t
            r0 = c * CHUNK
            if not isinstance(r0, int):
                r0 = pl.multiple_of(r0, CHUNK)
            acc = of_s if d == 0 else ob_s
            acc[pl.ds(r0, CHUNK), head_cols[b]] = r[:CHUNK] + oc_s[b, d, c]
            out.append(s * eg_s[b, d, c] - r[CHUNK:] + n_s[b, d, c])
        return tuple(out)

    if nc == group:
        s_fin = s_init
        for t in range(nc):
            s_fin = scan_step(t, s_fin)
    else:
        s_fin = lax.fori_loop(0, nc, scan_step, s_init)

    o = of_s[...] + ob_s[...]
    z = _silu(z_ref[...])
    og = og_ref[...]
    for sl in head_cols:
        oh = o[:, sl]
        y = oh * lax.rsqrt(jnp.mean(oh * oh, axis=-1, keepdims=True) + EPS) * og
        o_ref[:, sl] = (y * z[:, sl]).astype(o_ref.dtype)
    if emit_state:
        for (b, d), s in zip(chains, s_fin):
            st_ref[d, b] = s


def _deltanet(p1, bg, conv_qkv, a_log, dt_bias, onorm_g, s0, cfg, *, latent):
    hh, hb = cfg.heads, cfg.dn_heads
    tok, nbg = bg.shape
    n = cfg.dec_seq if latent else cfg.seq
    nseq = cfg.dec_batch if latent else cfg.batch
    blk0 = cfg.tok_ctx // n if latent else 0
    per = cfg.grid_w if latent else cfg.seq
    nc = n // CHUNK
    group = min(4, nc)
    width = hb * LANES
    nhb = hh // hb
    assert nbg == 4 * hh
    bgc = bg.reshape(tok // CHUNK, CHUNK, nbg)
    bgt = bgc.transpose(0, 2, 1)
    a_row = jnp.concatenate([jnp.zeros((2 * hh,), F32), a_log.reshape(-1)]).reshape(1, nbg)
    dtb_row = jnp.concatenate([jnp.zeros((2 * hh,), F32), dt_bias.reshape(-1)]).reshape(1, nbg)

    col = lambda g: pl.BlockSpec((n, width), lambda s, h: (blk0 + s, g * nhb + h))
    cw = lambda g: pl.BlockSpec((3, width), lambda s, h: (0, g * nhb + h))
    whole = lambda shape: pl.BlockSpec(shape, lambda s, h: (0,) * len(shape))
    in_specs = [col(0), col(1), col(2), col(3), cw(0), cw(1), cw(2),
                pl.BlockSpec((nc, CHUNK, nbg), lambda s, h: (blk0 + s, 0, 0)),
                pl.BlockSpec((nc, nbg, CHUNK), lambda s, h: (blk0 + s, 0, 0)),
                whole((1, nbg)), whole((1, nbg)), whole((nbg, 1)), whole((nbg, 1)), whole((1, LANES))]
    args = [p1, p1, p1, p1, conv_qkv, conv_qkv, conv_qkv, bgc, bgt, a_row, dtb_row,
            a_row.reshape(nbg, 1), dtb_row.reshape(nbg, 1), onorm_g.reshape(1, LANES)]
    st_spec = pl.BlockSpec((None, 2, hb, LANES, LANES), lambda s, h: (s, 0, h, 0, 0))
    if latent:
        in_specs.append(st_spec)
        args.append(s0)
    out_specs = [pl.BlockSpec((n, width), lambda s, h: (s, h))]
    out_shape = [jax.ShapeDtypeStruct((nseq * n, hh * LANES), BF16)]
    if not latent:
        out_specs.append(st_spec)
        out_shape.append(jax.ShapeDtypeStruct((nseq, 2, hh, LANES, LANES), F32))
    scratch = [pltpu.VMEM((n, width), F32)] * 5 + [
        pltpu.VMEM((hb, 2, nc, CHUNK + LANES, LANES), BF16), pltpu.VMEM((hb, 2, nc, LANES, LANES), F32),
        pltpu.VMEM((hb, 2, nc, CHUNK, LANES), F32), pltpu.VMEM((hb, 2, nc, 1, LANES), F32),
        pltpu.VMEM((nc, CHUNK, nbg), F32), pltpu.VMEM((nc, nbg, CHUNK), F32)]
    return pl.pallas_call(
        functools.partial(_dn_body, n=n, per=per, heads=hh, hb=hb, has_s0=latent, emit_state=not latent,
                          group=group),
        grid=(nseq, nhb),
        in_specs=in_specs,
        out_specs=out_specs,
        out_shape=out_shape,
        scratch_shapes=scratch,
        compiler_params=_params("arbitrary", "arbitrary"),
        name="deltanet_lat" if latent else "deltanet_ctx",
    )(*args)


def _forward(cfg, x_prompt, x_sample, state_dn, c, c_ctx, norm1_g, norm2_g, w_ada, b_ada, w_in,
             conv_qkv, a_log, dt_bias, onorm_g, conv_sc, w_pa, w_pb, w_o, w_up, conv_ff,
             w_down, final_g):
    d, hh = cfg.d, cfg.heads
    w1_n = 4 * cfg.w_hd
    nbg = 4 * hh
    x = jnp.concatenate([x_prompt.reshape(cfg.tok_ctx, d), x_sample.reshape(-1, d)], axis=0)
    cvec = jnp.zeros((MOD_ROWS, d), F32).at[0].set(c_ctx).at[1:1 + cfg.dec_batch].set(c)
    mod = _ada(cvec, w_ada, b_ada)
    ff_extra = cfg.ff_pad - cfg.d_ff
    w_in_t = jnp.swapaxes(w_in, 1, 2)
    w_pa_b, w_pb_b, w_o_b = w_pa.astype(BF16), w_pb.astype(BF16), w_o.astype(BF16)
    w_down_b = jnp.pad(w_down.astype(BF16), ((0, 0), (0, ff_extra), (0, 0)))
    states = []
    n2 = 3 * cfg.w_sc + 2 * d
    ff_tn, ff_sub = cfg.ff_tn, FFN_SUBTILES
    ff_w = ff_tn // ff_sub
    assert cfg.tm == cfg.dec_seq and cfg.tm % cfg.seq == 0 and cfg.d_ff % ff_w == 0 and cfg.ff_pad % ff_tn == 0
    ff_weights = [WCols(w_up, 0, half * cfg.d_ff + u * ff_w, 0, ff_sub)
                  for u in range(ff_sub) for half in range(2)]
    for l in range(cfg.depth):
        mod3 = mod[l].reshape(MOD_ROWS * 6, 1, d)
        h = _norm_mod(x, norm1_g[l], mod3, 0, 1, cfg)
        p1 = _proj("proj_qkvz", h, w_in_t, l, 0, 0, w1_n, cfg)
        bg = _bg_proj(h, w_in_t, l, w1_n, nbg, cfg)
        p2 = _proj("proj_sc_gates", h, w_in_t, l, w1_n, nbg, n2, cfg)
        o_ctx, st = _deltanet(p1, bg, conv_qkv[l], a_log[l], dt_bias[l], onorm_g[l], None, cfg, latent=False)
        o_lat, = _deltanet(p1, bg, conv_qkv[l], a_log[l], dt_bias[l], onorm_g[l], state_dn[:, l], cfg, latent=True)
        states.append(st)
        o = jnp.concatenate([o_ctx, o_lat], axis=0)
        y_sc = _sc_mixer(p2, conv_sc[l], cfg)
        merged = _merge(o, y_sc, w_pa_b, w_pb_b, l, p2, 3 * cfg.w_sc, cfg)
        x = _resid(merged, w_o_b, l, x, mod3, 2, d, cfg, "out_proj")
        h = _norm_mod(x, norm2_g[l], mod3, 3, 4, cfg)
        f = _wres_matmul(
            "ffn_up", [h], [wc._replace(layer=l) for wc in ff_weights],
            [(conv_ff, (None, 3, ff_tn), lambda j, i, l=l: (l, 0, j))],
            functools.partial(_ffn_up_epilogue, cfg=cfg), cfg.ff_pad, BF16, cfg.tm, ff_w, ff_tn, cfg.kc)
        x = _resid(f, w_down_b, l, x, mod3, 5, cfg.ff_tk, cfg, "ffn_down")
    y_prompt = _final_norm(x, final_g, 0, cfg.tok_ctx, cfg).reshape(x_prompt.shape)
    y_sample = _final_norm(x, final_g, cfg.tok_ctx, cfg.tok - cfg.tok_ctx, cfg).reshape(x_sample.shape)
    return y_prompt, y_sample, jnp.stack(states, axis=1)


def kernel(x_prompt, x_sample, state_dn, c, c_ctx, norm1_g, norm2_g, w_ada, b_ada, w_in, conv_qkv, a_log, dt_bias, onorm_g, conv_sc, w_pa, w_pb, w_o, w_up, conv_ff, w_down, final_g):
    return _forward(CFG, x_prompt, x_sample, state_dn, c, c_ctx, norm1_g, norm2_g, w_ada, b_ada, w_in,
                    conv_qkv, a_log, dt_bias, onorm_g, conv_sc, w_pa, w_pb, w_o, w_up, conv_ff,
                    w_down, final_g)
```

```python
import functools
from typing import NamedTuple

import jax
import jax.numpy as jnp
from jax import lax
from jax.experimental import pallas as pl
from jax.experimental.pallas import tpu as pltpu

F32 = jnp.float32
BF16 = jnp.bfloat16
EPS = 1e-6
CHUNK = 64
LANES = 128
MOD_ROWS = 8
V7X_VMEM_BYTES = 64 * 1024 * 1024
VMEM_LIMIT_BYTES = V7X_VMEM_BYTES - 6 * 1024 * 1024


class Cfg(NamedTuple):
    d: int
    batch: int
    seq: int
    depth: int
    dec_batch: int
    dec_seq: int
    grid_w: int
    heads: int
    w_sc: int
    d_ff: int
    tm: int
    tn: int
    ff_tn: int
    ff_tk: int
    dn_heads: int
    kc: int

    @property
    def w_hd(self):
        return self.heads * LANES

    @property
    def tok_ctx(self):
        return self.batch * self.seq

    @property
    def tok(self):
        return self.batch * self.seq + self.dec_batch * self.dec_seq

    @property
    def ff_pad(self):
        return -(-self.d_ff // self.ff_tk) * self.ff_tk


CFG = Cfg(d=4096, batch=32, seq=256, depth=2, dec_batch=2, dec_seq=1024, grid_w=64, heads=16,
          w_sc=2048, d_ff=11008, tm=1024, tn=1024, ff_tn=1024, ff_tk=2816, dn_heads=4, kc=512)
FFN_SUBTILES = 4
CAST_PAD_ROWS = 256


def _params(*sem):
    return pltpu.CompilerParams(dimension_semantics=sem, vmem_limit_bytes=VMEM_LIMIT_BYTES)


def _mod_row(i, cfg, tm):
    nct = cfg.tok_ctx // tm
    return jnp.where(i < nct, 0, 1 + (i - nct) // (cfg.dec_seq // tm))


def _silu(x):
    return x * jax.nn.sigmoid(x)


def _softplus(x):
    return jnp.maximum(x, 0.0) + jnp.log1p(jnp.exp(-jnp.abs(x)))


def _dot(a, b):
    return jnp.dot(a, b, preferred_element_type=F32)


def _dotb(a, b):
    return _dot(a.astype(BF16), b.astype(BF16))


def _dot_nt(a, b):
    return lax.dot_general(a, b, (((1,), (1,)), ((), ())), preferred_element_type=F32)


def _dot_tn(a, b):
    return lax.dot_general(a, b, (((0,), (0,)), ((), ())), preferred_element_type=F32)


def _ada_body(c_ref, w_ref, b_ref, o_ref):
    s = _silu(c_ref[...]).astype(BF16)
    o_ref[...] = _dot(s, w_ref[...].astype(BF16)) + b_ref[...]


def _ada(cvec, w_ada, b_ada):
    depth, d, n = w_ada.shape
    tn = 512
    return pl.pallas_call(
        _ada_body,
        grid=(depth, n // tn),
        in_specs=[pl.BlockSpec((MOD_ROWS, d), lambda l, j: (0, 0)),
                  pl.BlockSpec((None, d, tn), lambda l, j: (l, 0, j)),
                  pl.BlockSpec((None, 1, tn), lambda l, j: (l, 0, j))],
        out_specs=pl.BlockSpec((None, MOD_ROWS, tn), lambda l, j: (l, 0, j)),
        out_shape=jax.ShapeDtypeStruct((depth, MOD_ROWS, n), F32),
        compiler_params=_params("arbitrary", "arbitrary"),
        name="ada",
    )(cvec, w_ada, b_ada.reshape(depth, 1, n))


def _row_sources(srcs, tm, cols, index_fn):
    if len(srcs) == 1:
        return [pl.BlockSpec((tm, cols), index_fn)]
    na = srcs[0].shape[0] // tm

    def first(*g):
        i, j = index_fn(*g)
        return jnp.minimum(i, na - 1), j

    def second(*g):
        i, j = index_fn(*g)
        return jnp.maximum(i - na, 0), j

    return [pl.BlockSpec((tm, cols), first), pl.BlockSpec((tm, cols), second)]


def _pick_rows(refs, i, na):
    if len(refs) == 1:
        return refs[0][...]
    return jnp.where(i < na, refs[0][...], refs[1][...])


def _norm_mod_body(*refs, n_src, na):
    xs, (g_ref, sh_ref, sc_ref, o_ref) = refs[:n_src], refs[n_src:]
    x = _pick_rows(xs, pl.program_id(0), na)
    y = x * lax.rsqrt(jnp.mean(x * x, axis=-1, keepdims=True) + EPS) * g_ref[...]
    o_ref[...] = (y * (1.0 + sc_ref[...]) + sh_ref[...]).astype(o_ref.dtype)


def _norm_mod(xs, g, mod3, sh_idx, sc_idx, cfg):
    tok, d = sum(x.shape[0] for x in xs), xs[0].shape[1]
    tm = cfg.seq
    spec_mod = lambda idx: pl.BlockSpec((None, 1, d), lambda i: (_mod_row(i, cfg, tm) * 6 + idx, 0, 0))
    return pl.pallas_call(
        functools.partial(_norm_mod_body, n_src=len(xs), na=xs[0].shape[0] // tm),
        grid=(tok // tm,),
        in_specs=_row_sources(xs, tm, d, lambda i: (i, 0)) + [
            pl.BlockSpec((1, d), lambda i: (0, 0)), spec_mod(sh_idx), spec_mod(sc_idx)],
        out_specs=pl.BlockSpec((tm, d), lambda i: (i, 0)),
        out_shape=jax.ShapeDtypeStruct((tok, d), BF16),
        compiler_params=_params("arbitrary"),
        name="norm_mod",
    )(*xs, g.reshape(1, d), mod3, mod3)


def _final_norm_body(x_ref, g_ref, o_ref):
    x = x_ref[...]
    o_ref[...] = x * lax.rsqrt(jnp.mean(x * x, axis=-1, keepdims=True) + EPS) * g_ref[...]


def _final_norm(x, g, row0, rows, cfg):
    d = x.shape[1]
    tm = cfg.seq
    off = row0 // tm
    return pl.pallas_call(
        _final_norm_body,
        grid=(rows // tm,),
        in_specs=[pl.BlockSpec((tm, d), lambda i: (i + off, 0)),
                  pl.BlockSpec((1, d), lambda i: (0, 0))],
        out_specs=pl.BlockSpec((tm, d), lambda i: (i, 0)),
        out_shape=jax.ShapeDtypeStruct((rows, d), F32),
        compiler_params=_params("arbitrary"),
        name="final_norm",
    )(x, g.reshape(1, d))


class WCols(NamedTuple):
    w: jax.Array
    layer: int
    col0: int
    act: int
    stride: int


def _wres_body(*refs, n_act, weights, n_extra, epilogue, kc, n_cast):
    acts = refs[:n_act]
    w_refs = refs[n_act:n_act + len(weights)]
    pos = n_act + len(weights)
    extras = refs[pos:pos + n_extra]
    o_ref = refs[pos + n_extra]
    scrs = refs[pos + n_extra + 1:]
    s = pl.program_id(1)

    @pl.when(s < n_cast)
    def _():
        rows = pl.ds(pl.multiple_of(s * kc, kc), kc)
        for w_ref, scr in zip(w_refs, scrs):
            scr[rows, :] = w_ref[...].astype(BF16)

    @pl.when(s >= n_cast)
    def _():
        prods = [_dot(acts[wc.act][...], scr[...]) for wc, scr in zip(weights, scrs)]
        epilogue(prods, extras, o_ref, s - n_cast)


def _wres_matmul(name, acts, weights, extras, epilogue, n_out, out_dtype, tm, tn, tn_out, kc):
    tok = acts[0].shape[0]
    kdim = acts[0].shape[1]
    n_cast = kdim // kc
    assert kdim % kc == 0 and all(a.shape[1] == kdim for a in acts)
    row = lambda s: jnp.maximum(s - n_cast, 0)
    chunk = lambda s: jnp.minimum(s, n_cast - 1)
    in_specs = [pl.BlockSpec((tm, kdim), lambda j, s: (row(s), 0)) for a in acts]
    args = list(acts)
    scratch = []
    for wc in weights:
        assert wc.col0 % tn == 0 and wc.w.shape[1] == kdim
        c0, layer, stride = wc.col0 // tn, wc.layer, wc.stride
        last = pl.cdiv(wc.w.shape[2], tn) - 1
        in_specs.append(pl.BlockSpec(
            (None, kc, tn),
            lambda j, s, c0=c0, layer=layer, stride=stride, last=last:
            (layer, chunk(s), jnp.minimum(c0 + j * stride, last))))
        args.append(wc.w)
        scratch.append(pltpu.VMEM((kdim, tn), BF16))
    for arr, block, index_map in extras:
        in_specs.append(pl.BlockSpec(block, lambda j, s, index_map=index_map: index_map(j, row(s))))
        args.append(arr)
    return pl.pallas_call(
        functools.partial(_wres_body, n_act=len(acts), weights=tuple(wc._replace(w=None) for wc in weights),
                          n_extra=len(extras), epilogue=epilogue, kc=kc, n_cast=n_cast),
        grid=(n_out // tn_out, n_cast + tok // tm),
        in_specs=in_specs,
        out_specs=pl.BlockSpec((tm, tn_out), lambda j, s: (row(s), j)),
        out_shape=jax.ShapeDtypeStruct((tok, n_out), out_dtype),
        scratch_shapes=scratch,
        compiler_params=_params("arbitrary", "arbitrary"),
        name=name,
    )(*args)


def _store_epilogue(prods, extras, o_ref, i):
    o_ref[...] = prods[0].astype(o_ref.dtype)


def _proj_body(*refs, kc, n_cast, shift):
    if shift:
        a_ref, w_ref, nxt_ref, o_ref, scr = refs
    else:
        a_ref, w_ref, o_ref, scr = refs
    sweep, s = pl.program_id(0), pl.program_id(1)
    slot = sweep % 2

    def cast_chunk():
        rows = pl.ds(pl.multiple_of(jnp.minimum(s, n_cast - 1) * kc, kc), kc)
        w = w_ref[...]
        if shift:
            w = jnp.concatenate([w[shift:], nxt_ref[...]], axis=0)
        scr[slot, rows, :] = w.T.astype(BF16)

    @pl.when(sweep == 0)
    def _():
        cast_chunk()

    @pl.when(sweep > 0)
    def _():
        cast_chunk()
        o_ref[...] = _dot(a_ref[...], scr[1 - slot]).astype(o_ref.dtype)


def _proj(name, a, w_t, layer, row0, shift, n_out, cfg):
    tok, kdim = a.shape
    tm, tn, kc = cfg.tm, cfg.tn, cfg.kc
    n_cast, ni, nj = kdim // kc, tok // tm, n_out // tn
    assert kdim % kc == 0 and n_cast <= ni and row0 % tn == 0 and n_out % tn == 0
    c0 = row0 // tn
    tile = lambda t: jnp.minimum(t, nj - 1)
    chunk = lambda t, s: jnp.where(t == nj, n_cast - 1, jnp.minimum(s, n_cast - 1))
    row = lambda t, s: jnp.where(t == 0, 0, s)
    in_specs = [pl.BlockSpec((tm, kdim), lambda t, s: (row(t, s), 0)),
                pl.BlockSpec((None, tn, kc), lambda t, s: (layer, c0 + tile(t), chunk(t, s)))]
    args = [a, w_t]
    if shift:
        assert tn % shift == 0 and shift % 8 == 0
        per = tn // shift
        in_specs.append(pl.BlockSpec((None, shift, kc),
                                     lambda t, s: (layer, (c0 + tile(t) + 1) * per, chunk(t, s))))
        args.append(w_t)
    return pl.pallas_call(
        functools.partial(_proj_body, kc=kc, n_cast=n_cast, shift=shift),
        grid=(nj + 1, ni),
        in_specs=in_specs,
        out_specs=pl.BlockSpec((tm, tn), lambda t, s: (row(t, s), jnp.maximum(t - 1, 0))),
        out_shape=jax.ShapeDtypeStruct((tok, n_out), F32),
        scratch_shapes=[pltpu.VMEM((2, kdim, tn), BF16)],
        compiler_params=_params("arbitrary", "arbitrary"),
        name=name,
    )(*args)


def _bg_body(a_ref, w_ref, o_ref):
    o_ref[...] = _dot_nt(a_ref[...], w_ref[...].astype(BF16))


def _bg_proj(h, w_t, layer, row0, nbg, cfg):
    tok, k = h.shape
    tm = cfg.tm
    assert row0 % nbg == 0
    return pl.pallas_call(
        _bg_body,
        grid=(tok // tm,),
        in_specs=[pl.BlockSpec((tm, k), lambda i: (i, 0)),
                  pl.BlockSpec((None, nbg, k), lambda i: (layer, row0 // nbg, 0))],
        out_specs=pl.BlockSpec((tm, nbg), lambda i: (i, 0)),
        out_shape=jax.ShapeDtypeStruct((tok, nbg), F32),
        compiler_params=_params("arbitrary"),
        name="proj_bg",
    )(h, w_t)


def _merge_body(*refs, n_src, na):
    os, (y_ref, wa_ref, wb_ref, ga_ref, gb_ref, out_ref) = refs[:n_src], refs[n_src:]
    pa = _dot(_pick_rows(os, pl.program_id(0), na), wa_ref[...])
    pb = _dot(y_ref[...], wb_ref[...])
    out_ref[...] = (jax.nn.sigmoid(ga_ref[...]) * pa + jax.nn.sigmoid(gb_ref[...]) * pb).astype(out_ref.dtype)


def _merge(os, y, w_pa, w_pb, layer, p2, gate_col0, cfg):
    tok = y.shape[0]
    d = w_pa.shape[2]
    tm, tn = cfg.tm, min(cfg.tn // 2, d)
    ga0 = gate_col0 // tn
    gb0 = (gate_col0 + d) // tn
    return pl.pallas_call(
        functools.partial(_merge_body, n_src=len(os), na=os[0].shape[0] // tm),
        grid=(tok // tm, d // tn),
        in_specs=_row_sources(os, tm, os[0].shape[1], lambda i, j: (i, 0)) + [
            pl.BlockSpec((tm, y.shape[1]), lambda i, j: (i, 0)),
            pl.BlockSpec((None, w_pa.shape[1], tn), lambda i, j: (layer, 0, j)),
            pl.BlockSpec((None, w_pb.shape[1], tn), lambda i, j: (layer, 0, j)),
            pl.BlockSpec((tm, tn), lambda i, j: (i, ga0 + j)),
            pl.BlockSpec((tm, tn), lambda i, j: (i, gb0 + j))],
        out_specs=pl.BlockSpec((tm, tn), lambda i, j: (i, j)),
        out_shape=jax.ShapeDtypeStruct((tok, d), BF16),
        compiler_params=_params("arbitrary", "arbitrary"),
        name="merge",
    )(*os, y, w_pa, w_pb, p2, p2)


def _ffn_up_epilogue(prods, extras, o_ref, i, *, cfg):
    cw_ref, = extras
    tm, tn_out = o_ref.shape
    nsub = len(prods) // 2
    tn = tn_out // nsub
    j = pl.program_id(0)
    is_ctx = i < cfg.tok_ctx // tm
    gw = cfg.grid_w
    lane = lax.broadcasted_iota(jnp.int32, (tm, tn), 1)
    t = lax.broadcasted_iota(jnp.int32, (tm, tn), 0) & (cfg.seq - 1)
    zeros = jnp.zeros((gw, tn), F32)
    for u in range(nsub):
        g, v = prods[2 * u], prods[2 * u + 1]
        cols = slice(u * tn, (u + 1) * tn)
        w = cw_ref[:, cols]
        gm = jnp.where(is_ctx, jnp.where(t == 0, 0.0, pltpu.roll(g, 1, 0)),
                       jnp.concatenate([zeros, g[:tm - gw]], axis=0))
        gp = jnp.where(is_ctx, jnp.where(t == cfg.seq - 1, 0.0, pltpu.roll(g, tm - 1, 0)),
                       jnp.concatenate([g[gw:], zeros], axis=0))
        y = gm * w[0:1] + g * w[1:2] + gp * w[2:3]
        valid = lane + (j * tn_out + u * tn) < cfg.d_ff
        o_ref[:, cols] = jnp.where(valid, _silu(y) * v, 0.0).astype(o_ref.dtype)


def _resid_body(*refs, nk, n_src, na):
    a_ref, w_ref = refs[:2]
    xs, (gt_ref, o_ref) = refs[2:2 + n_src], refs[2 + n_src:]
    prod = lambda: _dot(a_ref[...], w_ref[...])
    finish = lambda acc: _pick_rows(xs, pl.program_id(0), na) + gt_ref[...] * acc
    if nk == 1:
        o_ref[...] = finish(prod())
        return
    k = pl.program_id(2)

    @pl.when(k == 0)
    def _():
        o_ref[...] = prod()

    @pl.when(jnp.logical_and(k > 0, k < nk - 1))
    def _():
        o_ref[...] += prod()

    @pl.when(k == nk - 1)
    def _():
        o_ref[...] = finish(o_ref[...] + prod())


def _resid(a, w, layer, xs, mod3, gt_idx, tk, cfg, name):
    tok, kdim = a.shape
    d = w.shape[2]
    tm, tn = cfg.tm, min(cfg.tn // len(xs), d)
    nk = kdim // tk
    return pl.pallas_call(
        functools.partial(_resid_body, nk=nk, n_src=len(xs), na=xs[0].shape[0] // tm),
        grid=(tok // tm, d // tn, nk),
        in_specs=[pl.BlockSpec((tm, tk), lambda i, j, k: (i, k)),
                  pl.BlockSpec((None, tk, tn), lambda i, j, k: (layer, k, j))]
        + _row_sources(xs, tm, tn, lambda i, j, k: (i, j))
        + [pl.BlockSpec((None, 1, tn), lambda i, j, k: (_mod_row(i, cfg, tm) * 6 + gt_idx, 0, j))],
        out_specs=pl.BlockSpec((tm, tn), lambda i, j, k: (i, j)),
        out_shape=jax.ShapeDtypeStruct((tok, d), F32),
        compiler_params=_params("arbitrary", "arbitrary", "arbitrary"),
        name=name,
    )(a, w, *xs, mod3)


def _cast_pad_body(w_ref, o_ref, *, rows_in):
    tr = o_ref.shape[0]
    row = lax.broadcasted_iota(jnp.int32, o_ref.shape, 0) + pl.program_id(1) * tr
    o_ref[...] = jnp.where(row < rows_in, w_ref[...], 0.0).astype(o_ref.dtype)


def _cast_pad_rows(w, rows_out, tr):
    depth, rows_in, n = w.shape
    last = pl.cdiv(rows_in, tr) - 1
    return pl.pallas_call(
        functools.partial(_cast_pad_body, rows_in=rows_in),
        grid=(depth, rows_out // tr),
        in_specs=[pl.BlockSpec((None, tr, n), lambda l, i: (l, jnp.minimum(i, last), 0))],
        out_specs=pl.BlockSpec((None, tr, n), lambda l, i: (l, i, 0)),
        out_shape=jax.ShapeDtypeStruct((depth, rows_out, n), BF16),
        compiler_params=_params("arbitrary", "arbitrary"),
        name="cast_pad",
    )(w)


def _sc_body(xs_ref, bs_ref, cs_ref, w_ref, o_ref, *, cfg):
    tm, tn = o_ref.shape
    i = pl.program_id(0)
    per = jnp.where(i < cfg.tok_ctx // tm, cfg.seq, cfg.grid_w)
    u = cs_ref[...] * xs_ref[...]
    t = lax.broadcasted_iota(jnp.int32, (tm, tn), 0) & (per - 1)
    um = jnp.where(t == 0, 0.0, pltpu.roll(u, 1, 0))
    up = jnp.where(t == per - 1, 0.0, pltpu.roll(u, tm - 1, 0))
    w = w_ref[...]
    o_ref[...] = (bs_ref[...] * (um * w[0:1] + u * w[1:2] + up * w[2:3])).astype(o_ref.dtype)


def _sc_mixer(p2, cw, cfg):
    tok = p2.shape[0]
    ws = cfg.w_sc
    tm, tn = cfg.tm, min(512, ws)
    nb = ws // tn
    return pl.pallas_call(
        functools.partial(_sc_body, cfg=cfg),
        grid=(tok // tm, nb),
        in_specs=[pl.BlockSpec((tm, tn), lambda i, j: (i, j)),
                  pl.BlockSpec((tm, tn), lambda i, j: (i, nb + j)),
                  pl.BlockSpec((tm, tn), lambda i, j: (i, 2 * nb + j)),
                  pl.BlockSpec((3, tn), lambda i, j: (0, j))],
        out_specs=pl.BlockSpec((tm, tn), lambda i, j: (i, j)),
        out_shape=jax.ShapeDtypeStruct((tok, ws), BF16),
        compiler_params=_params("arbitrary", "arbitrary"),
        name="sc_mixer",
    )(p2, p2, p2, cw)


def _dn_body(*refs, n, per, heads, hb, has_s0, emit_state, group):
    it = iter(refs)
    q_ref, k_ref, v_ref, z_ref = next(it), next(it), next(it), next(it)
    wq_ref, wk_ref, wv_ref = next(it), next(it), next(it)
    bgc_ref, bgt_ref, a_row_ref, dtb_row_ref, a_col_ref, dtb_col_ref, og_ref = (next(it) for _ in range(7))
    s0_ref = next(it) if has_s0 else None
    o_ref = next(it)
    st_ref = next(it) if emit_state else None
    qs, ks, vs, of_s, ob_s, qw_s, n_s, oc_s, eg_s, gcol_s, grow_s = it

    nc = n // CHUNK
    width = hb * LANES
    hblk = pl.program_id(1)
    head_cols = [slice(b * LANES, (b + 1) * LANES) for b in range(hb)]

    tok = lax.broadcasted_iota(jnp.int32, (n, width), 0) & (per - 1)
    first, last = tok == 0, tok == per - 1

    def conv_silu(x_ref, w_ref):
        x, w = x_ref[...], w_ref[...]
        xm = jnp.where(first, 0.0, pltpu.roll(x, 1, 0))
        xp = jnp.where(last, 0.0, pltpu.roll(x, n - 1, 0))
        return _silu(xm * w[0:1] + x * w[1:2] + xp * w[2:3])

    q = conv_silu(q_ref, wq_ref)
    k = conv_silu(k_ref, wk_ref)
    for sl in head_cols:
        qh, kh = q[:, sl], k[:, sl]
        qs[:, sl] = qh * lax.rsqrt(jnp.sum(qh * qh, axis=-1, keepdims=True) + EPS) * (LANES ** -0.5)
        ks[:, sl] = kh * lax.rsqrt(jnp.sum(kh * kh, axis=-1, keepdims=True) + EPS)
    vs[...] = conv_silu(v_ref, wv_ref)

    nbg = bgc_ref.shape[-1]
    lane_bg = lax.broadcasted_iota(jnp.int32, (CHUNK, nbg), 1)
    is_beta = lane_bg < 2 * heads
    neg_a_row, dtb_row = -jnp.exp(a_row_ref[...]), dtb_row_ref[...]
    neg_a_col, dtb_col = -jnp.exp(a_col_ref[...]), dtb_col_ref[...]

    def gates(c, carry):
        bgc = bgc_ref[c]
        gcol_s[c] = jnp.where(is_beta, jax.nn.sigmoid(bgc), neg_a_row * _softplus(bgc + dtb_row))
        grow_s[c] = neg_a_col * _softplus(bgt_ref[c] + dtb_col)
        return carry

    if nc == group:
        for c in range(nc):
            gates(c, 0)
    else:
        lax.fori_loop(0, nc, gates, 0)
    sub = lax.broadcasted_iota(jnp.int32, (CHUNK, CHUNK), 0)
    lane = lax.broadcasted_iota(jnp.int32, (CHUNK, CHUNK), 1)
    eye = (sub == lane).astype(F32)
    blk16 = (sub // 16) == (lane // 16)
    blk32 = (sub // 32) == (lane // 32)
    off32 = jnp.logical_and(blk32, jnp.logical_not(blk16))
    vis = (lane <= sub, lane >= sub)
    vis_t = (sub <= lane, sub >= lane)
    strict = (lane < sub, lane > sub)

    def stage_group(gi, carry):
        units = []
        for b in range(hb):
            for j in range(group):
                c = gi * group + j
                r0 = c * CHUNK
                if not isinstance(r0, int):
                    r0 = pl.multiple_of(r0, CHUNK)
                rows = pl.ds(r0, CHUNK)
                qc, kc, vc = qs[rows, head_cols[b]], ks[rows, head_cols[b]], vs[rows, head_cols[b]]
                kb = kc.astype(BF16)
                kk = _dot_nt(kb, kb)
                qk = _dot_nt(qc.astype(BF16), kb)
                gall = gcol_s[c]
                hd = hblk * hb + b
                for d in range(2):
                    col = lambda g: jnp.sum(jnp.where(lane_bg == g * heads + hd, gall, 0.0), axis=1, keepdims=True)
                    beta = col(d)
                    g_col = col(2 + d)
                    g_row = grow_s[c, pl.ds((2 + d) * heads + hd, 1), :]
                    m = vis[d]
                    gc_col = jnp.sum(jnp.where(m, jnp.broadcast_to(g_row, (CHUNK, CHUNK)), 0.0), axis=1, keepdims=True)
                    gc_row = jnp.sum(jnp.where(vis_t[d], jnp.broadcast_to(g_col, (CHUNK, CHUNK)), 0.0), axis=0, keepdims=True)
                    g_last = jnp.sum(g_row, axis=1, keepdims=True)
                    decay = jnp.where(m, jnp.exp(jnp.where(m, gc_col - gc_row, 0.0)), 0.0)
                    e_gc = jnp.exp(gc_col)
                    units.append(dict(
                        b=b, c=c, d=d,
                        a=jnp.where(strict[d], kk * decay * beta, 0.0),
                        x=jnp.concatenate([vc * beta, kc * (beta * e_gc)], axis=1),
                        intra=jnp.where(m, qk * decay, 0.0).astype(BF16),
                        k_dec=(kc * jnp.exp(g_last - gc_col)).astype(BF16),
                        q_dec=qc * e_gc,
                        eg=jnp.broadcast_to(jnp.exp(g_last), (1, LANES))))
        a_d = [jnp.where(blk16, u["a"], 0.0).astype(BF16) for u in units]
        p2 = [_dot(x, x).astype(BF16) for x in a_d]
        t = [eye - x.astype(F32) for x in a_d]
        t = [x + _dot(x.astype(BF16), p) for x, p in zip(t, p2)]
        p4 = [_dot(p, p).astype(BF16) for p in p2]
        t = [x + _dot(x.astype(BF16), p) for x, p in zip(t, p4)]
        p8 = [_dot(p, p).astype(BF16) for p in p4]
        t = [x + _dot(x.astype(BF16), p) for x, p in zip(t, p8)]
        mm = [_dotb(x, jnp.where(off32, u["a"], 0.0)) for x, u in zip(t, units)]
        t = [x - _dotb(y, x) for x, y in zip(t, mm)]
        mm = [_dotb(x, jnp.where(blk32, 0.0, u["a"])) for x, u in zip(t, units)]
        t = [x - _dotb(y, x) for x, y in zip(t, mm)]
        xb = [_dotb(x, u["x"]).astype(BF16) for x, u in zip(t, units)]
        iw = [_dot(u["intra"], x) for u, x in zip(units, xb)]
        kt = [_dot_tn(u["k_dec"], x) for u, x in zip(units, xb)]
        for u, iwu, ktu in zip(units, iw, kt):
            b, c, d = u["b"], u["c"], u["d"]
            oc_s[b, d, c] = iwu[:, :LANES]
            qw_s[b, d, c, 0:CHUNK, :] = (u["q_dec"] - iwu[:, LANES:]).astype(BF16)
            qw_s[b, d, c, CHUNK:, :] = ktu[:, LANES:].astype(BF16)
            n_s[b, d, c] = ktu[:, :LANES]
            eg_s[b, d, c] = u["eg"]
        return carry

    if nc == group:
        stage_group(0, 0)
    else:
        lax.fori_loop(0, nc // group, stage_group, 0)

    chains = [(b, d) for b in range(hb) for d in range(2)]
    if has_s0:
        s_init = tuple(s0_ref[d, b] for b, d in chains)
    else:
        s_init = tuple(jnp.zeros((LANES, LANES), F32) for _ in chains)

    def scan_step(t, carry):
        res = []
        for (b, d), s in zip(chains, carry):
            c = t if d == 0 else nc - 1 - t
            res.append(_dot(qw_s[b, d, c], s.astype(BF16)))
        out = []
        for (b, d), s, r in zip(chains, carry, res):
            c = t if d == 0 else nc - 1 - t
            r0 = c * CHUNK
            if not isinstance(r0, int):
                r0 = pl.multiple_of(r0, CHUNK)
            acc = of_s if d == 0 else ob_s
            acc[pl.ds(r0, CHUNK), head_cols[b]] = r[:CHUNK] + oc_s[b, d, c]
            out.append(s * eg_s[b, d, c] - r[CHUNK:] + n_s[b, d, c])
        return tuple(out)

    if nc == group:
        s_fin = s_init
        for t in range(nc):
            s_fin = scan_step(t, s_fin)
    else:
        s_fin = lax.fori_loop(0, nc, scan_step, s_init)

    o = of_s[...] + ob_s[...]
    z = _silu(z_ref[...])
    og = og_ref[...]
    for sl in head_cols:
        oh = o[:, sl]
        y = oh * lax.rsqrt(jnp.mean(oh * oh, axis=-1, keepdims=True) + EPS) * og
        o_ref[:, sl] = (y * z[:, sl]).astype(o_ref.dtype)
    if emit_state:
        for (b, d), s in zip(chains, s_fin):
            st_ref[d, b] = s


def _deltanet(p1, bg, conv_qkv, a_log, dt_bias, onorm_g, s0, cfg, *, latent):
    hh, hb = cfg.heads, cfg.dn_heads
    tok, nbg = bg.shape
    n = cfg.dec_seq if latent else cfg.seq
    nseq = cfg.dec_batch if latent else cfg.batch
    blk0 = cfg.tok_ctx // n if latent else 0
    per = cfg.grid_w if latent else cfg.seq
    nc = n // CHUNK
    group = min(4, nc)
    width = hb * LANES
    nhb = hh // hb
    assert nbg == 4 * hh
    bgc = bg.reshape(tok // CHUNK, CHUNK, nbg)
    bgt = bgc.transpose(0, 2, 1)
    a_row = jnp.concatenate([jnp.zeros((2 * hh,), F32), a_log.reshape(-1)]).reshape(1, nbg)
    dtb_row = jnp.concatenate([jnp.zeros((2 * hh,), F32), dt_bias.reshape(-1)]).reshape(1, nbg)

    col = lambda g: pl.BlockSpec((n, width), lambda s, h: (blk0 + s, g * nhb + h))
    cw = lambda g: pl.BlockSpec((3, width), lambda s, h: (0, g * nhb + h))
    whole = lambda shape: pl.BlockSpec(shape, lambda s, h: (0,) * len(shape))
    in_specs = [col(0), col(1), col(2), col(3), cw(0), cw(1), cw(2),
                pl.BlockSpec((nc, CHUNK, nbg), lambda s, h: (blk0 + s, 0, 0)),
                pl.BlockSpec((nc, nbg, CHUNK), lambda s, h: (blk0 + s, 0, 0)),
                whole((1, nbg)), whole((1, nbg)), whole((nbg, 1)), whole((nbg, 1)), whole((1, LANES))]
    args = [p1, p1, p1, p1, conv_qkv, conv_qkv, conv_qkv, bgc, bgt, a_row, dtb_row,
            a_row.reshape(nbg, 1), dtb_row.reshape(nbg, 1), onorm_g.reshape(1, LANES)]
    st_spec = pl.BlockSpec((None, 2, hb, LANES, LANES), lambda s, h: (s, 0, h, 0, 0))
    if latent:
        in_specs.append(st_spec)
        args.append(s0)
    out_specs = [pl.BlockSpec((n, width), lambda s, h: (s, h))]
    out_shape = [jax.ShapeDtypeStruct((nseq * n, hh * LANES), BF16)]
    if not latent:
        out_specs.append(st_spec)
        out_shape.append(jax.ShapeDtypeStruct((nseq, 2, hh, LANES, LANES), F32))
    scratch = [pltpu.VMEM((n, width), F32)] * 5 + [
        pltpu.VMEM((hb, 2, nc, CHUNK + LANES, LANES), BF16), pltpu.VMEM((hb, 2, nc, LANES, LANES), F32),
        pltpu.VMEM((hb, 2, nc, CHUNK, LANES), F32), pltpu.VMEM((hb, 2, nc, 1, LANES), F32),
        pltpu.VMEM((nc, CHUNK, nbg), F32), pltpu.VMEM((nc, nbg, CHUNK), F32)]
    return pl.pallas_call(
        functools.partial(_dn_body, n=n, per=per, heads=hh, hb=hb, has_s0=latent, emit_state=not latent,
                          group=group),
        grid=(nseq, nhb),
        in_specs=in_specs,
        out_specs=out_specs,
        out_shape=out_shape,
        scratch_shapes=scratch,
        compiler_params=_params("arbitrary", "arbitrary"),
        name="deltanet_lat" if latent else "deltanet_ctx",
    )(*args)


def _forward(cfg, x_prompt, x_sample, state_dn, c, c_ctx, norm1_g, norm2_g, w_ada, b_ada, w_in,
             conv_qkv, a_log, dt_bias, onorm_g, conv_sc, w_pa, w_pb, w_o, w_up, conv_ff,
             w_down, final_g):
    d, hh = cfg.d, cfg.heads
    w1_n = 4 * cfg.w_hd
    nbg = 4 * hh
    xs = (x_prompt.reshape(cfg.tok_ctx, d), x_sample.reshape(-1, d))
    cvec = jnp.zeros((MOD_ROWS, d), F32).at[0].set(c_ctx).at[1:1 + cfg.dec_batch].set(c)
    mod = _ada(cvec, w_ada, b_ada)
    ff_extra = cfg.ff_pad - cfg.d_ff
    w_in_t = jnp.swapaxes(w_in, 1, 2)
    w_pa_b, w_pb_b, w_o_b = w_pa.astype(BF16), w_pb.astype(BF16), w_o.astype(BF16)
    w_down_b = _cast_pad_rows(w_down, cfg.ff_pad, CAST_PAD_ROWS)
    states = []
    n2 = 3 * cfg.w_sc + 2 * d
    ff_tn, ff_sub = cfg.ff_tn, FFN_SUBTILES
    ff_w = ff_tn // ff_sub
    assert cfg.tm == cfg.dec_seq and cfg.tm % cfg.seq == 0 and cfg.d_ff % ff_w == 0 and cfg.ff_pad % ff_tn == 0
    ff_weights = [WCols(w_up, 0, half * cfg.d_ff + u * ff_w, 0, ff_sub)
                  for u in range(ff_sub) for half in range(2)]
    for l in range(cfg.depth):
        mod3 = mod[l].reshape(MOD_ROWS * 6, 1, d)
        h = _norm_mod(xs, norm1_g[l], mod3, 0, 1, cfg)
        p1 = _proj("proj_qkvz", h, w_in_t, l, 0, 0, w1_n, cfg)
        bg = _bg_proj(h, w_in_t, l, w1_n, nbg, cfg)
        p2 = _proj("proj_sc_gates", h, w_in_t, l, w1_n, nbg, n2, cfg)
        o_ctx, st = _deltanet(p1, bg, conv_qkv[l], a_log[l], dt_bias[l], onorm_g[l], None, cfg, latent=False)
        o_lat, = _deltanet(p1, bg, conv_qkv[l], a_log[l], dt_bias[l], onorm_g[l], state_dn[:, l], cfg, latent=True)
        states.append(st)
        y_sc = _sc_mixer(p2, conv_sc[l], cfg)
        merged = _merge((o_ctx, o_lat), y_sc, w_pa_b, w_pb_b, l, p2, 3 * cfg.w_sc, cfg)
        xs = (_resid(merged, w_o_b, l, xs, mod3, 2, d, cfg, "out_proj"),)
        h = _norm_mod(xs, norm2_g[l], mod3, 3, 4, cfg)
        f = _wres_matmul(
            "ffn_up", [h], [wc._replace(layer=l) for wc in ff_weights],
            [(conv_ff, (None, 3, ff_tn), lambda j, i, l=l: (l, 0, j))],
            functools.partial(_ffn_up_epilogue, cfg=cfg), cfg.ff_pad, BF16, cfg.tm, ff_w, ff_tn, cfg.kc)
        xs = (_resid(f, w_down_b, l, xs, mod3, 5, cfg.ff_tk, cfg, "ffn_down"),)
    x, = xs
    y_prompt = _final_norm(x, final_g, 0, cfg.tok_ctx, cfg).reshape(x_prompt.shape)
    y_sample = _final_norm(x, final_g, cfg.tok_ctx, cfg.tok - cfg.tok_ctx, cfg).reshape(x_sample.shape)
    return y_prompt, y_sample, jnp.stack(states, axis=1)


def kernel(x_prompt, x_sample, state_dn, c, c_ctx, norm1_g, norm2_g, w_ada, b_ada, w_in, conv_qkv, a_log, dt_bias, onorm_g, conv_sc, w_pa, w_pb, w_o, w_up, conv_ff, w_down, final_g):
    return _forward(CFG, x_prompt, x_sample, state_dn, c, c_ctx, norm1_g, norm2_g, w_ada, b_ada, w_in,
                    conv_qkv, a_log, dt_bias, onorm_g, conv_sc, w_pa, w_pb, w_o, w_up, conv_ff,
                    w_down, final_g)
```

```python
import functools
from typing import NamedTuple

import jax
import jax.numpy as jnp
from jax import lax
from jax.experimental import pallas as pl
from jax.experimental.pallas import tpu as pltpu

F32 = jnp.float32
BF16 = jnp.bfloat16
EPS = 1e-6
CHUNK = 64
LANES = 128
MOD_ROWS = 8
V7X_VMEM_BYTES = 64 * 1024 * 1024
VMEM_LIMIT_BYTES = V7X_VMEM_BYTES - 6 * 1024 * 1024


class Cfg(NamedTuple):
    d: int
    batch: int
    seq: int
    depth: int
    dec_batch: int
    dec_seq: int
    grid_w: int
    heads: int
    w_sc: int
    d_ff: int
    tm: int
    tn: int
    ff_tn: int
    ff_tk: int
    dn_heads: int
    kc: int

    @property
    def w_hd(self):
        return self.heads * LANES

    @property
    def tok_ctx(self):
        return self.batch * self.seq

    @property
    def tok(self):
        return self.batch * self.seq + self.dec_batch * self.dec_seq

    @property
    def ff_pad(self):
        return -(-self.d_ff // self.ff_tk) * self.ff_tk


CFG = Cfg(d=4096, batch=32, seq=256, depth=2, dec_batch=2, dec_seq=1024, grid_w=64, heads=16,
          w_sc=2048, d_ff=11008, tm=1024, tn=1024, ff_tn=1024, ff_tk=2816, dn_heads=4, kc=512)
FFN_SUBTILES = 4
CAST_PAD_ROWS = 256
MERGE_SUBTILES = 2


def _params(*sem):
    return pltpu.CompilerParams(dimension_semantics=sem, vmem_limit_bytes=VMEM_LIMIT_BYTES)


def _mod_row(i, cfg, tm):
    nct = cfg.tok_ctx // tm
    return jnp.where(i < nct, 0, 1 + (i - nct) // (cfg.dec_seq // tm))


def _silu(x):
    return x * jax.nn.sigmoid(x)


def _softplus(x):
    return jnp.maximum(x, 0.0) + jnp.log1p(jnp.exp(-jnp.abs(x)))


def _dot(a, b):
    return jnp.dot(a, b, preferred_element_type=F32)


def _dotb(a, b):
    return _dot(a.astype(BF16), b.astype(BF16))


def _dot_nt(a, b):
    return lax.dot_general(a, b, (((1,), (1,)), ((), ())), preferred_element_type=F32)


def _dot_tn(a, b):
    return lax.dot_general(a, b, (((0,), (0,)), ((), ())), preferred_element_type=F32)


def _ada_body(c_ref, w_ref, b_ref, o_ref):
    s = _silu(c_ref[...]).astype(BF16)
    o_ref[...] = _dot(s, w_ref[...].astype(BF16)) + b_ref[...]


def _ada(cvec, w_ada, b_ada):
    depth, d, n = w_ada.shape
    tn = 512
    return pl.pallas_call(
        _ada_body,
        grid=(depth, n // tn),
        in_specs=[pl.BlockSpec((MOD_ROWS, d), lambda l, j: (0, 0)),
                  pl.BlockSpec((None, d, tn), lambda l, j: (l, 0, j)),
                  pl.BlockSpec((None, 1, tn), lambda l, j: (l, 0, j))],
        out_specs=pl.BlockSpec((None, MOD_ROWS, tn), lambda l, j: (l, 0, j)),
        out_shape=jax.ShapeDtypeStruct((depth, MOD_ROWS, n), F32),
        compiler_params=_params("arbitrary", "arbitrary"),
        name="ada",
    )(cvec, w_ada, b_ada.reshape(depth, 1, n))


def _row_sources(srcs, tm, cols, index_fn):
    if len(srcs) == 1:
        return [pl.BlockSpec((tm, cols), index_fn)]
    na = srcs[0].shape[0] // tm

    def first(*g):
        i, j = index_fn(*g)
        return jnp.minimum(i, na - 1), j

    def second(*g):
        i, j = index_fn(*g)
        return jnp.maximum(i - na, 0), j

    return [pl.BlockSpec((tm, cols), first), pl.BlockSpec((tm, cols), second)]


def _pick_rows(refs, i, na):
    if len(refs) == 1:
        return refs[0][...]
    return jnp.where(i < na, refs[0][...], refs[1][...])


def _norm_mod_body(*refs, n_src, na):
    xs, (g_ref, sh_ref, sc_ref, o_ref) = refs[:n_src], refs[n_src:]
    x = _pick_rows(xs, pl.program_id(0), na)
    y = x * lax.rsqrt(jnp.mean(x * x, axis=-1, keepdims=True) + EPS) * g_ref[...]
    o_ref[...] = (y * (1.0 + sc_ref[...]) + sh_ref[...]).astype(o_ref.dtype)


def _norm_mod(xs, g, mod3, sh_idx, sc_idx, cfg):
    tok, d = sum(x.shape[0] for x in xs), xs[0].shape[1]
    tm = cfg.seq
    spec_mod = lambda idx: pl.BlockSpec((None, 1, d), lambda i: (_mod_row(i, cfg, tm) * 6 + idx, 0, 0))
    return pl.pallas_call(
        functools.partial(_norm_mod_body, n_src=len(xs), na=xs[0].shape[0] // tm),
        grid=(tok // tm,),
        in_specs=_row_sources(xs, tm, d, lambda i: (i, 0)) + [
            pl.BlockSpec((1, d), lambda i: (0, 0)), spec_mod(sh_idx), spec_mod(sc_idx)],
        out_specs=pl.BlockSpec((tm, d), lambda i: (i, 0)),
        out_shape=jax.ShapeDtypeStruct((tok, d), BF16),
        compiler_params=_params("arbitrary"),
        name="norm_mod",
    )(*xs, g.reshape(1, d), mod3, mod3)


def _final_norm_body(x_ref, g_ref, o_ref):
    x = x_ref[...]
    o_ref[...] = x * lax.rsqrt(jnp.mean(x * x, axis=-1, keepdims=True) + EPS) * g_ref[...]


def _final_norm(x, g, row0, rows, cfg):
    d = x.shape[1]
    tm = cfg.seq
    off = row0 // tm
    return pl.pallas_call(
        _final_norm_body,
        grid=(rows // tm,),
        in_specs=[pl.BlockSpec((tm, d), lambda i: (i + off, 0)),
                  pl.BlockSpec((1, d), lambda i: (0, 0))],
        out_specs=pl.BlockSpec((tm, d), lambda i: (i, 0)),
        out_shape=jax.ShapeDtypeStruct((rows, d), F32),
        compiler_params=_params("arbitrary"),
        name="final_norm",
    )(x, g.reshape(1, d))


class WCols(NamedTuple):
    w: jax.Array
    layer: int
    col0: int
    act: int
    stride: int


def _wres_body(*refs, n_act, weights, n_extra, epilogue, kc, n_cast):
    acts = refs[:n_act]
    w_refs = refs[n_act:n_act + len(weights)]
    pos = n_act + len(weights)
    extras = refs[pos:pos + n_extra]
    o_ref = refs[pos + n_extra]
    scrs = refs[pos + n_extra + 1:]
    s = pl.program_id(1)

    @pl.when(s < n_cast)
    def _():
        rows = pl.ds(pl.multiple_of(s * kc, kc), kc)
        for w_ref, scr in zip(w_refs, scrs):
            scr[rows, :] = w_ref[...].astype(BF16)

    @pl.when(s >= n_cast)
    def _():
        prods = [_dot(acts[wc.act][...], scr[...]) for wc, scr in zip(weights, scrs)]
        epilogue(prods, extras, o_ref, s - n_cast)


def _wres_matmul(name, acts, weights, extras, epilogue, n_out, out_dtype, tm, tn, tn_out, kc):
    tok = acts[0].shape[0]
    kdim = acts[0].shape[1]
    n_cast = kdim // kc
    assert kdim % kc == 0 and all(a.shape[1] == kdim for a in acts)
    row = lambda s: jnp.maximum(s - n_cast, 0)
    chunk = lambda s: jnp.minimum(s, n_cast - 1)
    in_specs = [pl.BlockSpec((tm, kdim), lambda j, s: (row(s), 0)) for a in acts]
    args = list(acts)
    scratch = []
    for wc in weights:
        assert wc.col0 % tn == 0 and wc.w.shape[1] == kdim
        c0, layer, stride = wc.col0 // tn, wc.layer, wc.stride
        last = pl.cdiv(wc.w.shape[2], tn) - 1
        in_specs.append(pl.BlockSpec(
            (None, kc, tn),
            lambda j, s, c0=c0, layer=layer, stride=stride, last=last:
            (layer, chunk(s), jnp.minimum(c0 + j * stride, last))))
        args.append(wc.w)
        scratch.append(pltpu.VMEM((kdim, tn), BF16))
    for arr, block, index_map in extras:
        in_specs.append(pl.BlockSpec(block, lambda j, s, index_map=index_map: index_map(j, row(s))))
        args.append(arr)
    return pl.pallas_call(
        functools.partial(_wres_body, n_act=len(acts), weights=tuple(wc._replace(w=None) for wc in weights),
                          n_extra=len(extras), epilogue=epilogue, kc=kc, n_cast=n_cast),
        grid=(n_out // tn_out, n_cast + tok // tm),
        in_specs=in_specs,
        out_specs=pl.BlockSpec((tm, tn_out), lambda j, s: (row(s), j)),
        out_shape=jax.ShapeDtypeStruct((tok, n_out), out_dtype),
        scratch_shapes=scratch,
        compiler_params=_params("arbitrary", "arbitrary"),
        name=name,
    )(*args)


def _store_epilogue(prods, extras, o_ref, i):
    o_ref[...] = prods[0].astype(o_ref.dtype)


def _proj_body(*refs, kc, n_cast, shift):
    if shift:
        a_ref, w_ref, nxt_ref, o_ref, scr = refs
    else:
        a_ref, w_ref, o_ref, scr = refs
    sweep, s = pl.program_id(0), pl.program_id(1)
    slot = sweep % 2

    def cast_chunk():
        rows = pl.ds(pl.multiple_of(jnp.minimum(s, n_cast - 1) * kc, kc), kc)
        w = w_ref[...]
        if shift:
            w = jnp.concatenate([w[shift:], nxt_ref[...]], axis=0)
        scr[slot, rows, :] = w.T.astype(BF16)

    @pl.when(sweep == 0)
    def _():
        cast_chunk()

    @pl.when(sweep > 0)
    def _():
        cast_chunk()
        o_ref[...] = _dot(a_ref[...], scr[1 - slot]).astype(o_ref.dtype)


def _proj(name, a, w_t, layer, row0, shift, n_out, cfg):
    tok, kdim = a.shape
    tm, tn, kc = cfg.tm, cfg.tn, cfg.kc
    n_cast, ni, nj = kdim // kc, tok // tm, n_out // tn
    assert kdim % kc == 0 and n_cast <= ni and row0 % tn == 0 and n_out % tn == 0
    c0 = row0 // tn
    tile = lambda t: jnp.minimum(t, nj - 1)
    chunk = lambda t, s: jnp.where(t == nj, n_cast - 1, jnp.minimum(s, n_cast - 1))
    row = lambda t, s: jnp.where(t == 0, 0, s)
    in_specs = [pl.BlockSpec((tm, kdim), lambda t, s: (row(t, s), 0)),
                pl.BlockSpec((None, tn, kc), lambda t, s: (layer, c0 + tile(t), chunk(t, s)))]
    args = [a, w_t]
    if shift:
        assert tn % shift == 0 and shift % 8 == 0
        per = tn // shift
        in_specs.append(pl.BlockSpec((None, shift, kc),
                                     lambda t, s: (layer, (c0 + tile(t) + 1) * per, chunk(t, s))))
        args.append(w_t)
    return pl.pallas_call(
        functools.partial(_proj_body, kc=kc, n_cast=n_cast, shift=shift),
        grid=(nj + 1, ni),
        in_specs=in_specs,
        out_specs=pl.BlockSpec((tm, tn), lambda t, s: (row(t, s), jnp.maximum(t - 1, 0))),
        out_shape=jax.ShapeDtypeStruct((tok, n_out), F32),
        scratch_shapes=[pltpu.VMEM((2, kdim, tn), BF16)],
        compiler_params=_params("arbitrary", "arbitrary"),
        name=name,
    )(*args)


def _bg_body(a_ref, w_ref, o_ref):
    o_ref[...] = _dot_nt(a_ref[...], w_ref[...].astype(BF16))


def _bg_proj(h, w_t, layer, row0, nbg, cfg):
    tok, k = h.shape
    tm = cfg.tm
    assert row0 % nbg == 0
    return pl.pallas_call(
        _bg_body,
        grid=(tok // tm,),
        in_specs=[pl.BlockSpec((tm, k), lambda i: (i, 0)),
                  pl.BlockSpec((None, nbg, k), lambda i: (layer, row0 // nbg, 0))],
        out_specs=pl.BlockSpec((tm, nbg), lambda i: (i, 0)),
        out_shape=jax.ShapeDtypeStruct((tok, nbg), F32),
        compiler_params=_params("arbitrary"),
        name="proj_bg",
    )(h, w_t)


def _merge_body(*refs, n_src, na):
    os, (y_ref, wa_ref, wb_ref, ga_ref, gb_ref, out_ref) = refs[:n_src], refs[n_src:]
    o, y = _pick_rows(os, pl.program_id(0), na), y_ref[...]
    tn = out_ref.shape[1] // MERGE_SUBTILES
    for u in range(MERGE_SUBTILES):
        cols = slice(u * tn, (u + 1) * tn)
        pa = _dot(o, wa_ref[:, cols])
        pb = _dot(y, wb_ref[:, cols])
        out_ref[:, cols] = (jax.nn.sigmoid(ga_ref[:, cols]) * pa
                            + jax.nn.sigmoid(gb_ref[:, cols]) * pb).astype(out_ref.dtype)


def _merge(os, y, w_pa, w_pb, layer, p2, gate_col0, cfg):
    tok = y.shape[0]
    d = w_pa.shape[2]
    tm, tn = cfg.tm, min(cfg.tn // 2, d)
    ga0 = gate_col0 // tn
    gb0 = (gate_col0 + d) // tn
    return pl.pallas_call(
        functools.partial(_merge_body, n_src=len(os), na=os[0].shape[0] // tm),
        grid=(tok // tm, d // tn),
        in_specs=_row_sources(os, tm, os[0].shape[1], lambda i, j: (i, 0)) + [
            pl.BlockSpec((tm, y.shape[1]), lambda i, j: (i, 0)),
            pl.BlockSpec((None, w_pa.shape[1], tn), lambda i, j: (layer, 0, j)),
            pl.BlockSpec((None, w_pb.shape[1], tn), lambda i, j: (layer, 0, j)),
            pl.BlockSpec((tm, tn), lambda i, j: (i, ga0 + j)),
            pl.BlockSpec((tm, tn), lambda i, j: (i, gb0 + j))],
        out_specs=pl.BlockSpec((tm, tn), lambda i, j: (i, j)),
        out_shape=jax.ShapeDtypeStruct((tok, d), BF16),
        compiler_params=_params("arbitrary", "arbitrary"),
        name="merge",
    )(*os, y, w_pa, w_pb, p2, p2)


def _ffn_up_epilogue(prods, extras, o_ref, i, *, cfg):
    cw_ref, = extras
    tm, tn_out = o_ref.shape
    nsub = len(prods) // 2
    tn = tn_out // nsub
    j = pl.program_id(0)
    is_ctx = i < cfg.tok_ctx // tm
    gw = cfg.grid_w
    lane = lax.broadcasted_iota(jnp.int32, (tm, tn), 1)
    t = lax.broadcasted_iota(jnp.int32, (tm, tn), 0) & (cfg.seq - 1)
    zeros = jnp.zeros((gw, tn), F32)
    for u in range(nsub):
        g, v = prods[2 * u], prods[2 * u + 1]
        cols = slice(u * tn, (u + 1) * tn)
        w = cw_ref[:, cols]
        gm = jnp.where(is_ctx, jnp.where(t == 0, 0.0, pltpu.roll(g, 1, 0)),
                       jnp.concatenate([zeros, g[:tm - gw]], axis=0))
        gp = jnp.where(is_ctx, jnp.where(t == cfg.seq - 1, 0.0, pltpu.roll(g, tm - 1, 0)),
                       jnp.concatenate([g[gw:], zeros], axis=0))
        y = gm * w[0:1] + g * w[1:2] + gp * w[2:3]
        valid = lane + (j * tn_out + u * tn) < cfg.d_ff
        o_ref[:, cols] = jnp.where(valid, _silu(y) * v, 0.0).astype(o_ref.dtype)


def _resid_body(*refs, nk, n_src, na):
    a_ref, w_ref = refs[:2]
    xs, (gt_ref, o_ref) = refs[2:2 + n_src], refs[2 + n_src:]
    prod = lambda: _dot(a_ref[...], w_ref[...])
    finish = lambda acc: _pick_rows(xs, pl.program_id(0), na) + gt_ref[...] * acc
    if nk == 1:
        o_ref[...] = finish(prod())
        return
    k = pl.program_id(2)

    @pl.when(k == 0)
    def _():
        o_ref[...] = prod()

    @pl.when(jnp.logical_and(k > 0, k < nk - 1))
    def _():
        o_ref[...] += prod()

    @pl.when(k == nk - 1)
    def _():
        o_ref[...] = finish(o_ref[...] + prod())


def _resid(a, w, layer, xs, mod3, gt_idx, tk, cfg, name):
    tok, kdim = a.shape
    d = w.shape[2]
    tm, tn = cfg.tm, min(cfg.tn // len(xs), d)
    nk = kdim // tk
    return pl.pallas_call(
        functools.partial(_resid_body, nk=nk, n_src=len(xs), na=xs[0].shape[0] // tm),
        grid=(tok // tm, d // tn, nk),
        in_specs=[pl.BlockSpec((tm, tk), lambda i, j, k: (i, k)),
                  pl.BlockSpec((None, tk, tn), lambda i, j, k: (layer, k, j))]
        + _row_sources(xs, tm, tn, lambda i, j, k: (i, j))
        + [pl.BlockSpec((None, 1, tn), lambda i, j, k: (_mod_row(i, cfg, tm) * 6 + gt_idx, 0, j))],
        out_specs=pl.BlockSpec((tm, tn), lambda i, j, k: (i, j)),
        out_shape=jax.ShapeDtypeStruct((tok, d), F32),
        compiler_params=_params("arbitrary", "arbitrary", "arbitrary"),
        name=name,
    )(a, w, *xs, mod3)


def _cast_pad_body(w_ref, o_ref, *, rows_in):
    tr = o_ref.shape[0]
    row = lax.broadcasted_iota(jnp.int32, o_ref.shape, 0) + pl.program_id(1) * tr
    o_ref[...] = jnp.where(row < rows_in, w_ref[...], 0.0).astype(o_ref.dtype)


def _cast_pad_rows(w, rows_out, tr):
    depth, rows_in, n = w.shape
    last = pl.cdiv(rows_in, tr) - 1
    return pl.pallas_call(
        functools.partial(_cast_pad_body, rows_in=rows_in),
        grid=(depth, rows_out // tr),
        in_specs=[pl.BlockSpec((None, tr, n), lambda l, i: (l, jnp.minimum(i, last), 0))],
        out_specs=pl.BlockSpec((None, tr, n), lambda l, i: (l, i, 0)),
        out_shape=jax.ShapeDtypeStruct((depth, rows_out, n), BF16),
        compiler_params=_params("arbitrary", "arbitrary"),
        name="cast_pad",
    )(w)


def _sc_body(xs_ref, bs_ref, cs_ref, w_ref, o_ref, *, cfg):
    tm, tn = o_ref.shape
    i = pl.program_id(0)
    per = jnp.where(i < cfg.tok_ctx // tm, cfg.seq, cfg.grid_w)
    u = cs_ref[...] * xs_ref[...]
    t = lax.broadcasted_iota(jnp.int32, (tm, tn), 0) & (per - 1)
    um = jnp.where(t == 0, 0.0, pltpu.roll(u, 1, 0))
    up = jnp.where(t == per - 1, 0.0, pltpu.roll(u, tm - 1, 0))
    w = w_ref[...]
    o_ref[...] = (bs_ref[...] * (um * w[0:1] + u * w[1:2] + up * w[2:3])).astype(o_ref.dtype)


def _sc_mixer(p2, cw, cfg):
    tok = p2.shape[0]
    ws = cfg.w_sc
    tm, tn = cfg.tm, min(512, ws)
    nb = ws // tn
    return pl.pallas_call(
        functools.partial(_sc_body, cfg=cfg),
        grid=(tok // tm, nb),
        in_specs=[pl.BlockSpec((tm, tn), lambda i, j: (i, j)),
                  pl.BlockSpec((tm, tn), lambda i, j: (i, nb + j)),
                  pl.BlockSpec((tm, tn), lambda i, j: (i, 2 * nb + j)),
                  pl.BlockSpec((3, tn), lambda i, j: (0, j))],
        out_specs=pl.BlockSpec((tm, tn), lambda i, j: (i, j)),
        out_shape=jax.ShapeDtypeStruct((tok, ws), BF16),
        compiler_params=_params("arbitrary", "arbitrary"),
        name="sc_mixer",
    )(p2, p2, p2, cw)


def _dn_body(*refs, n, per, heads, hb, has_s0, emit_state, group):
    it = iter(refs)
    q_ref, k_ref, v_ref, z_ref = next(it), next(it), next(it), next(it)
    wq_ref, wk_ref, wv_ref = next(it), next(it), next(it)
    bgc_ref, bgt_ref, a_row_ref, dtb_row_ref, a_col_ref, dtb_col_ref, og_ref = (next(it) for _ in range(7))
    s0_ref = next(it) if has_s0 else None
    o_ref = next(it)
    st_ref = next(it) if emit_state else None
    qs, ks, vs, of_s, ob_s, qw_s, n_s, oc_s, eg_s, gcol_s, grow_s = it

    nc = n // CHUNK
    width = hb * LANES
    hblk = pl.program_id(1)
    head_cols = [slice(b * LANES, (b + 1) * LANES) for b in range(hb)]

    tok = lax.broadcasted_iota(jnp.int32, (n, width), 0) & (per - 1)
    first, last = tok == 0, tok == per - 1

    def conv_silu(x_ref, w_ref):
        x, w = x_ref[...], w_ref[...]
        xm = jnp.where(first, 0.0, pltpu.roll(x, 1, 0))
        xp = jnp.where(last, 0.0, pltpu.roll(x, n - 1, 0))
        return _silu(xm * w[0:1] + x * w[1:2] + xp * w[2:3])

    q = conv_silu(q_ref, wq_ref)
    k = conv_silu(k_ref, wk_ref)
    for sl in head_cols:
        qh, kh = q[:, sl], k[:, sl]
        qs[:, sl] = qh * lax.rsqrt(jnp.sum(qh * qh, axis=-1, keepdims=True) + EPS) * (LANES ** -0.5)
        ks[:, sl] = kh * lax.rsqrt(jnp.sum(kh * kh, axis=-1, keepdims=True) + EPS)
    vs[...] = conv_silu(v_ref, wv_ref)

    nbg = bgc_ref.shape[-1]
    lane_bg = lax.broadcasted_iota(jnp.int32, (CHUNK, nbg), 1)
    is_beta = lane_bg < 2 * heads
    neg_a_row, dtb_row = -jnp.exp(a_row_ref[...]), dtb_row_ref[...]
    neg_a_col, dtb_col = -jnp.exp(a_col_ref[...]), dtb_col_ref[...]
    fwd_heads = lax.broadcasted_iota(jnp.int32, (heads, 2 * CHUNK), 1) < CHUNK

    def gates(c, carry):
        bgc = bgc_ref[c]
        gcol_s[c] = jnp.where(is_beta, jax.nn.sigmoid(bgc), neg_a_row * _softplus(bgc + dtb_row))
        gt = neg_a_col * _softplus(bgt_ref[c] + dtb_col)
        grow_s[c] = jnp.where(fwd_heads, gt[2 * heads:3 * heads], gt[3 * heads:])
        return carry

    if nc == group:
        for c in range(nc):
            gates(c, 0)
    else:
        lax.fori_loop(0, nc, gates, 0)

    sub = lax.broadcasted_iota(jnp.int32, (CHUNK, 2 * CHUNK), 0)
    lane = lax.broadcasted_iota(jnp.int32, (CHUNK, 2 * CHUNK), 1)
    fwd = lane < CHUNK
    colx = lane & (CHUNK - 1)
    bwd = jnp.logical_not(fwd)
    vis_f, vis_b = jnp.logical_and(fwd, colx <= sub), jnp.logical_and(bwd, colx >= sub)
    vis = jnp.logical_or(vis_f, vis_b)
    vis_t = jnp.logical_or(jnp.logical_and(fwd, sub <= colx), jnp.logical_and(bwd, sub >= colx))
    strict = jnp.logical_and(vis, colx != sub)
    eye = (sub == colx).astype(F32)
    blk16 = (sub // 16) == (colx // 16)
    blk32 = (sub // 32) == (colx // 32)
    off32 = jnp.logical_and(blk32, jnp.logical_not(blk16))
    fwd_row = lax.broadcasted_iota(jnp.int32, (1, 2 * CHUNK), 1) < CHUNK
    zeros_x = jnp.zeros((CHUNK, 2 * LANES), BF16)

    def bd(y):
        return jnp.concatenate([jnp.where(fwd, y, 0.0).astype(BF16), jnp.where(fwd, 0.0, y).astype(BF16)], axis=0)

    def bd_x(xf, xb):
        return jnp.concatenate([jnp.concatenate([xf, zeros_x], axis=1),
                                jnp.concatenate([zeros_x, xb], axis=1)], axis=0)

    def stage_group(gi, carry):
        units = []
        for b in range(hb):
            for j in range(group):
                c = gi * group + j
                r0 = c * CHUNK
                if not isinstance(r0, int):
                    r0 = pl.multiple_of(r0, CHUNK)
                rows = pl.ds(r0, CHUNK)
                qc, kc, vc = qs[rows, head_cols[b]], ks[rows, head_cols[b]], vs[rows, head_cols[b]]
                kb = kc.astype(BF16)
                kb2 = jnp.concatenate([kb, kb], axis=0)
                kk = _dot_nt(kb, kb2)
                qk = _dot_nt(qc.astype(BF16), kb2)
                gall = gcol_s[c]
                hd = hblk * hb + b
                col = lambda g: jnp.sum(jnp.where(lane_bg == g * heads + hd, gall, 0.0), axis=1, keepdims=True)
                beta = (col(0), col(1))
                g_col = jnp.where(fwd, col(2), col(3))
                g_row = grow_s[c, pl.ds(hd, 1), :]
                g_rows = jnp.broadcast_to(g_row, (CHUNK, 2 * CHUNK))
                gc = (jnp.sum(jnp.where(vis_f, g_rows, 0.0), axis=1, keepdims=True),
                      jnp.sum(jnp.where(vis_b, g_rows, 0.0), axis=1, keepdims=True))
                gc_col = jnp.where(fwd, gc[0], gc[1])
                gc_row = jnp.sum(jnp.where(vis_t, g_col, 0.0), axis=0, keepdims=True)
                g_last = (jnp.sum(jnp.where(fwd_row, g_row, 0.0), axis=1, keepdims=True),
                          jnp.sum(jnp.where(fwd_row, 0.0, g_row), axis=1, keepdims=True))
                decay = jnp.where(vis, jnp.exp(jnp.where(vis, gc_col - gc_row, 0.0)), 0.0)
                e_gc = (jnp.exp(gc[0]), jnp.exp(gc[1]))
                units.append(dict(
                    b=b, c=c,
                    a=jnp.where(strict, kk * decay * jnp.where(fwd, beta[0], beta[1]), 0.0),
                    x=[jnp.concatenate([vc * beta[d], kc * (beta[d] * e_gc[d])], axis=1).astype(BF16)
                       for d in range(2)],
                    intra=jnp.where(vis, qk * decay, 0.0).astype(BF16),
                    k_dec=[(kc * jnp.exp(g_last[d] - gc[d])).astype(BF16) for d in range(2)],
                    q_dec=[qc * e_gc[d] for d in range(2)],
                    eg=[jnp.broadcast_to(jnp.exp(g_last[d]), (1, LANES)) for d in range(2)]))
        pmm = lambda x, y_bd: _dot(x.astype(BF16), y_bd)
        a_d = [jnp.where(blk16, u["a"], 0.0) for u in units]
        p2 = [pmm(x, bd(x)) for x in a_d]
        p2_bd = [bd(p) for p in p2]
        t = [eye - x for x in a_d]
        t = [x + pmm(x, p) for x, p in zip(t, p2_bd)]
        p4 = [pmm(p, pb) for p, pb in zip(p2, p2_bd)]
        p4_bd = [bd(p) for p in p4]
        t = [x + pmm(x, p) for x, p in zip(t, p4_bd)]
        p8 = [pmm(p, pb) for p, pb in zip(p4, p4_bd)]
        t = [x + pmm(x, bd(p)) for x, p in zip(t, p8)]
        mm = [pmm(x, bd(jnp.where(off32, u["a"], 0.0))) for x, u in zip(t, units)]
        t = [x - pmm(y, bd(x)) for x, y in zip(t, mm)]
        mm = [pmm(x, bd(jnp.where(blk32, 0.0, u["a"]))) for x, u in zip(t, units)]
        t = [x - pmm(y, bd(x)) for x, y in zip(t, mm)]
        xb = [pmm(x, bd_x(*u["x"])).astype(BF16) for x, u in zip(t, units)]
        xd = [[x[:, :2 * LANES], x[:, 2 * LANES:]] for x in xb]
        iw = [_dot(u["intra"], bd_x(*x)) for u, x in zip(units, xd)]
        kt = [[_dot_tn(u["k_dec"][d], x[d]) for d in range(2)] for u, x in zip(units, xd)]
        for u, iwu, ktu in zip(units, iw, kt):
            b, c = u["b"], u["c"]
            for d in range(2):
                iwd = iwu[:, 2 * LANES * d:2 * LANES * (d + 1)]
                oc_s[b, d, c] = iwd[:, :LANES]
                qw_s[b, d, c, 0:CHUNK, :] = (u["q_dec"][d] - iwd[:, LANES:]).astype(BF16)
                qw_s[b, d, c, CHUNK:, :] = ktu[d][:, LANES:].astype(BF16)
                n_s[b, d, c] = ktu[d][:, :LANES]
                eg_s[b, d, c] = u["eg"][d]
        return carry

    if nc == group:
        stage_group(0, 0)
    else:
        lax.fori_loop(0, nc // group, stage_group, 0)

    chains = [(b, d) for b in range(hb) for d in range(2)]
    if has_s0:
        s_init = tuple(s0_ref[d, b] for b, d in chains)
    else:
        s_init = tuple(jnp.zeros((LANES, LANES), F32) for _ in chains)

    def scan_step(t, carry):
        res = []
        for (b, d), s in zip(chains, carry):
            c = t if d == 0 else nc - 1 - t
            res.append(_dot(qw_s[b, d, c], s.astype(BF16)))
        out = []
        for (b, d), s, r in zip(chains, carry, res):
            c = t if d == 0 else nc - 1 - t
            r0 = c * CHUNK
            if not isinstance(r0, int):
                r0 = pl.multiple_of(r0, CHUNK)
            acc = of_s if d == 0 else ob_s
            acc[pl.ds(r0, CHUNK), head_cols[b]] = r[:CHUNK] + oc_s[b, d, c]
            out.append(s * eg_s[b, d, c] - r[CHUNK:] + n_s[b, d, c])
        return tuple(out)

    if nc == group:
        s_fin = s_init
        for t in range(nc):
            s_fin = scan_step(t, s_fin)
    else:
        s_fin = lax.fori_loop(0, nc, scan_step, s_init)

    o = of_s[...] + ob_s[...]
    z = _silu(z_ref[...])
    og = og_ref[...]
    for sl in head_cols:
        oh = o[:, sl]
        y = oh * lax.rsqrt(jnp.mean(oh * oh, axis=-1, keepdims=True) + EPS) * og
        o_ref[:, sl] = (y * z[:, sl]).astype(o_ref.dtype)
    if emit_state:
        for (b, d), s in zip(chains, s_fin):
            st_ref[d, b] = s


def _deltanet(p1, bg, conv_qkv, a_log, dt_bias, onorm_g, s0, cfg, *, latent):
    hh, hb = cfg.heads, cfg.dn_heads
    tok, nbg = bg.shape
    n = cfg.dec_seq if latent else cfg.seq
    nseq = cfg.dec_batch if latent else cfg.batch
    blk0 = cfg.tok_ctx // n if latent else 0
    per = cfg.grid_w if latent else cfg.seq
    nc = n // CHUNK
    group = min(4, nc)
    width = hb * LANES
    nhb = hh // hb
    assert nbg == 4 * hh
    bgc = bg.reshape(tok // CHUNK, CHUNK, nbg)
    bgt = jnp.tile(bgc.transpose(0, 2, 1), (1, 1, 2))
    a_row = jnp.concatenate([jnp.zeros((2 * hh,), F32), a_log.reshape(-1)]).reshape(1, nbg)
    dtb_row = jnp.concatenate([jnp.zeros((2 * hh,), F32), dt_bias.reshape(-1)]).reshape(1, nbg)

    col = lambda g: pl.BlockSpec((n, width), lambda s, h: (blk0 + s, g * nhb + h))
    cw = lambda g: pl.BlockSpec((3, width), lambda s, h: (0, g * nhb + h))
    whole = lambda shape: pl.BlockSpec(shape, lambda s, h: (0,) * len(shape))
    in_specs = [col(0), col(1), col(2), col(3), cw(0), cw(1), cw(2),
                pl.BlockSpec((nc, CHUNK, nbg), lambda s, h: (blk0 + s, 0, 0)),
                pl.BlockSpec((nc, nbg, 2 * CHUNK), lambda s, h: (blk0 + s, 0, 0)),
                whole((1, nbg)), whole((1, nbg)), whole((nbg, 1)), whole((nbg, 1)), whole((1, LANES))]
    args = [p1, p1, p1, p1, conv_qkv, conv_qkv, conv_qkv, bgc, bgt, a_row, dtb_row,
            a_row.reshape(nbg, 1), dtb_row.reshape(nbg, 1), onorm_g.reshape(1, LANES)]
    st_spec = pl.BlockSpec((None, 2, hb, LANES, LANES), lambda s, h: (s, 0, h, 0, 0))
    if latent:
        in_specs.append(st_spec)
        args.append(s0)
    out_specs = [pl.BlockSpec((n, width), lambda s, h: (s, h))]
    out_shape = [jax.ShapeDtypeStruct((nseq * n, hh * LANES), BF16)]
    if not latent:
        out_specs.append(st_spec)
        out_shape.append(jax.ShapeDtypeStruct((nseq, 2, hh, LANES, LANES), F32))
    scratch = [pltpu.VMEM((n, width), F32)] * 5 + [
        pltpu.VMEM((hb, 2, nc, CHUNK + LANES, LANES), BF16), pltpu.VMEM((hb, 2, nc, LANES, LANES), F32),
        pltpu.VMEM((hb, 2, nc, CHUNK, LANES), F32), pltpu.VMEM((hb, 2, nc, 1, LANES), F32),
        pltpu.VMEM((nc, CHUNK, nbg), F32), pltpu.VMEM((nc, hh, 2 * CHUNK), F32)]
    return pl.pallas_call(
        functools.partial(_dn_body, n=n, per=per, heads=hh, hb=hb, has_s0=latent, emit_state=not latent,
                          group=group),
        grid=(nseq, nhb),
        in_specs=in_specs,
        out_specs=out_specs,
        out_shape=out_shape,
        scratch_shapes=scratch,
        compiler_params=_params("arbitrary", "arbitrary"),
        name="deltanet_lat" if latent else "deltanet_ctx",
    )(*args)


def _forward(cfg, x_prompt, x_sample, state_dn, c, c_ctx, norm1_g, norm2_g, w_ada, b_ada, w_in,
             conv_qkv, a_log, dt_bias, onorm_g, conv_sc, w_pa, w_pb, w_o, w_up, conv_ff,
             w_down, final_g):
    d, hh = cfg.d, cfg.heads
    w1_n = 4 * cfg.w_hd
    nbg = 4 * hh
    xs = (x_prompt.reshape(cfg.tok_ctx, d), x_sample.reshape(-1, d))
    cvec = jnp.zeros((MOD_ROWS, d), F32).at[0].set(c_ctx).at[1:1 + cfg.dec_batch].set(c)
    mod = _ada(cvec, w_ada, b_ada)
    ff_extra = cfg.ff_pad - cfg.d_ff
    w_in_t = jnp.swapaxes(w_in, 1, 2)
    w_pa_b, w_pb_b, w_o_b = w_pa.astype(BF16), w_pb.astype(BF16), w_o.astype(BF16)
    w_down_b = _cast_pad_rows(w_down, cfg.ff_pad, CAST_PAD_ROWS)
    states = []
    n2 = 3 * cfg.w_sc + 2 * d
    ff_tn, ff_sub = cfg.ff_tn, FFN_SUBTILES
    ff_w = ff_tn // ff_sub
    assert cfg.tm == cfg.dec_seq and cfg.tm % cfg.seq == 0 and cfg.d_ff % ff_w == 0 and cfg.ff_pad % ff_tn == 0
    ff_weights = [WCols(w_up, 0, half * cfg.d_ff + u * ff_w, 0, ff_sub)
                  for u in range(ff_sub) for half in range(2)]
    for l in range(cfg.depth):
        mod3 = mod[l].reshape(MOD_ROWS * 6, 1, d)
        h = _norm_mod(xs, norm1_g[l], mod3, 0, 1, cfg)
        p1 = _proj("proj_qkvz", h, w_in_t, l, 0, 0, w1_n, cfg)
        bg = _bg_proj(h, w_in_t, l, w1_n, nbg, cfg)
        p2 = _proj("proj_sc_gates", h, w_in_t, l, w1_n, nbg, n2, cfg)
        o_ctx, st = _deltanet(p1, bg, conv_qkv[l], a_log[l], dt_bias[l], onorm_g[l], None, cfg, latent=False)
        o_lat, = _deltanet(p1, bg, conv_qkv[l], a_log[l], dt_bias[l], onorm_g[l], state_dn[:, l], cfg, latent=True)
        states.append(st)
        y_sc = _sc_mixer(p2, conv_sc[l], cfg)
        merged = _merge((o_ctx, o_lat), y_sc, w_pa_b, w_pb_b, l, p2, 3 * cfg.w_sc, cfg)
        xs = (_resid(merged, w_o_b, l, xs, mod3, 2, d, cfg, "out_proj"),)
        h = _norm_mod(xs, norm2_g[l], mod3, 3, 4, cfg)
        f = _wres_matmul(
            "ffn_up", [h], [wc._replace(layer=l) for wc in ff_weights],
            [(conv_ff, (None, 3, ff_tn), lambda j, i, l=l: (l, 0, j))],
            functools.partial(_ffn_up_epilogue, cfg=cfg), cfg.ff_pad, BF16, cfg.tm, ff_w, ff_tn, cfg.kc)
        xs = (_resid(f, w_down_b, l, xs, mod3, 5, cfg.ff_tk, cfg, "ffn_down"),)
    x, = xs
    y_prompt = _final_norm(x, final_g, 0, cfg.tok_ctx, cfg).reshape(x_prompt.shape)
    y_sample = _final_norm(x, final_g, cfg.tok_ctx, cfg.tok - cfg.tok_ctx, cfg).reshape(x_sample.shape)
    return y_prompt, y_sample, jnp.stack(states, axis=1)


def kernel(x_prompt, x_sample, state_dn, c, c_ctx, norm1_g, norm2_g, w_ada, b_ada, w_in, conv_qkv, a_log, dt_bias, onorm_g, conv_sc, w_pa, w_pb, w_o, w_up, conv_ff, w_down, final_g):
    return _forward(CFG, x_prompt, x_sample, state_dn, c, c_ctx, norm1_g, norm2_g, w_ada, b_ada, w_in,
                    conv_qkv, a_log, dt_bias, onorm_g, conv_sc, w_pa, w_pb, w_o, w_up, conv_ff,
                    w_down, final_g)
```

```python
import functools
from typing import NamedTuple

import jax
import jax.numpy as jnp
from jax import lax
from jax.experimental import pallas as pl
from jax.experimental.pallas import tpu as pltpu

F32 = jnp.float32
BF16 = jnp.bfloat16
EPS = 1e-6
CHUNK = 64
LANES = 128
MOD_ROWS = 8
V7X_VMEM_BYTES = 64 * 1024 * 1024
VMEM_LIMIT_BYTES = V7X_VMEM_BYTES - 6 * 1024 * 1024


class Cfg(NamedTuple):
    d: int
    batch: int
    seq: int
    depth: int
    dec_batch: int
    dec_seq: int
    grid_w: int
    heads: int
    w_sc: int
    d_ff: int
    tm: int
    tn: int
    ff_tn: int
    ff_tk: int
    dn_heads: int
    kc: int
    dn_heads_lat: int = 4

    @property
    def w_hd(self):
        return self.heads * LANES

    @property
    def tok_ctx(self):
        return self.batch * self.seq

    @property
    def tok(self):
        return self.batch * self.seq + self.dec_batch * self.dec_seq

    @property
    def ff_pad(self):
        return -(-self.d_ff // self.ff_tk) * self.ff_tk


CFG = Cfg(d=4096, batch=32, seq=256, depth=2, dec_batch=2, dec_seq=1024, grid_w=64, heads=16,
          w_sc=2048, d_ff=11008, tm=1024, tn=1024, ff_tn=1024, ff_tk=2816, dn_heads=8, kc=512)
FFN_SUBTILES = 4
CAST_PAD_ROWS = 256


def _params(*sem):
    return pltpu.CompilerParams(dimension_semantics=sem, vmem_limit_bytes=VMEM_LIMIT_BYTES)


def _mod_row(i, cfg, tm):
    nct = cfg.tok_ctx // tm
    return jnp.where(i < nct, 0, 1 + (i - nct) // (cfg.dec_seq // tm))


def _silu(x):
    return x * jax.nn.sigmoid(x)


def _softplus(x):
    return jnp.maximum(x, 0.0) + jnp.log1p(jnp.exp(-jnp.abs(x)))


def _dot(a, b):
    return jnp.dot(a, b, preferred_element_type=F32)


def _dotb(a, b):
    return _dot(a.astype(BF16), b.astype(BF16))


def _dot_nt(a, b):
    return lax.dot_general(a, b, (((1,), (1,)), ((), ())), preferred_element_type=F32)


def _dot_tn(a, b):
    return lax.dot_general(a, b, (((0,), (0,)), ((), ())), preferred_element_type=F32)


def _ada_body(c_ref, w_ref, b_ref, o_ref):
    s = _silu(c_ref[...]).astype(BF16)
    o_ref[...] = _dot(s, w_ref[...].astype(BF16)) + b_ref[...]


def _ada(cvec, w_ada, b_ada):
    depth, d, n = w_ada.shape
    tn = 512
    return pl.pallas_call(
        _ada_body,
        grid=(depth, n // tn),
        in_specs=[pl.BlockSpec((MOD_ROWS, d), lambda l, j: (0, 0)),
                  pl.BlockSpec((None, d, tn), lambda l, j: (l, 0, j)),
                  pl.BlockSpec((None, 1, tn), lambda l, j: (l, 0, j))],
        out_specs=pl.BlockSpec((None, MOD_ROWS, tn), lambda l, j: (l, 0, j)),
        out_shape=jax.ShapeDtypeStruct((depth, MOD_ROWS, n), F32),
        compiler_params=_params("arbitrary", "arbitrary"),
        name="ada",
    )(cvec, w_ada, b_ada.reshape(depth, 1, n))


def _row_sources(srcs, tm, cols, index_fn):
    if len(srcs) == 1:
        return [pl.BlockSpec((tm, cols), index_fn)]
    na = srcs[0].shape[0] // tm

    def first(*g):
        i, j = index_fn(*g)
        return jnp.minimum(i, na - 1), j

    def second(*g):
        i, j = index_fn(*g)
        return jnp.maximum(i - na, 0), j

    return [pl.BlockSpec((tm, cols), first), pl.BlockSpec((tm, cols), second)]


def _pick_rows(refs, i, na):
    if len(refs) == 1:
        return refs[0][...]
    return jnp.where(i < na, refs[0][...], refs[1][...])


def _norm_mod_body(*refs, n_src, na):
    xs, (g_ref, sh_ref, sc_ref, o_ref) = refs[:n_src], refs[n_src:]
    x = _pick_rows(xs, pl.program_id(0), na)
    y = x * lax.rsqrt(jnp.mean(x * x, axis=-1, keepdims=True) + EPS) * g_ref[...]
    o_ref[...] = (y * (1.0 + sc_ref[...]) + sh_ref[...]).astype(o_ref.dtype)


def _norm_mod(xs, g, mod3, sh_idx, sc_idx, cfg):
    tok, d = sum(x.shape[0] for x in xs), xs[0].shape[1]
    tm = cfg.tm // 2
    spec_mod = lambda idx: pl.BlockSpec((None, 1, d), lambda i: (_mod_row(i, cfg, tm) * 6 + idx, 0, 0))
    return pl.pallas_call(
        functools.partial(_norm_mod_body, n_src=len(xs), na=xs[0].shape[0] // tm),
        grid=(tok // tm,),
        in_specs=_row_sources(xs, tm, d, lambda i: (i, 0)) + [
            pl.BlockSpec((1, d), lambda i: (0, 0)), spec_mod(sh_idx), spec_mod(sc_idx)],
        out_specs=pl.BlockSpec((tm, d), lambda i: (i, 0)),
        out_shape=jax.ShapeDtypeStruct((tok, d), BF16),
        compiler_params=_params("arbitrary"),
        name="norm_mod",
    )(*xs, g.reshape(1, d), mod3, mod3)


def _final_norm_body(x_ref, g_ref, o_ref):
    x = x_ref[...]
    o_ref[...] = x * lax.rsqrt(jnp.mean(x * x, axis=-1, keepdims=True) + EPS) * g_ref[...]


def _final_norm(x, g, row0, rows, cfg):
    d = x.shape[1]
    tm = cfg.tm // 2
    off = row0 // tm
    return pl.pallas_call(
        _final_norm_body,
        grid=(rows // tm,),
        in_specs=[pl.BlockSpec((tm, d), lambda i: (i + off, 0)),
                  pl.BlockSpec((1, d), lambda i: (0, 0))],
        out_specs=pl.BlockSpec((tm, d), lambda i: (i, 0)),
        out_shape=jax.ShapeDtypeStruct((rows, d), F32),
        compiler_params=_params("arbitrary"),
        name="final_norm",
    )(x, g.reshape(1, d))


class WCols(NamedTuple):
    w: jax.Array
    layer: int
    col0: int
    act: int
    stride: int


def _wres_body(*refs, n_act, weights, fuse, n_extra, epilogue, kc, n_cast):
    acts = refs[:n_act]
    w_refs = refs[n_act:n_act + len(weights)]
    pos = n_act + len(weights)
    extras = refs[pos:pos + n_extra]
    o_ref = refs[pos + n_extra]
    scrs = refs[pos + n_extra + 1:]
    s = pl.program_id(1)

    @pl.when(s < n_cast)
    def _():
        rows = pl.ds(pl.multiple_of(s * kc, kc), kc)
        for k, w_ref in enumerate(w_refs):
            tn = w_ref.shape[1]
            scrs[k // fuse][rows, (k % fuse) * tn:(k % fuse + 1) * tn] = w_ref[...].astype(BF16)

    @pl.when(s >= n_cast)
    def _():
        prods = [_dot(acts[weights[g * fuse].act][...], scr[...]) for g, scr in enumerate(scrs)]
        epilogue(prods, extras, o_ref, s - n_cast)


def _wres_matmul(name, acts, weights, fuse, extras, epilogue, n_out, out_dtype, tm, tn, tn_out, kc):
    tok = acts[0].shape[0]
    kdim = acts[0].shape[1]
    n_cast = kdim // kc
    assert kdim % kc == 0 and all(a.shape[1] == kdim for a in acts)
    row = lambda s: jnp.maximum(s - n_cast, 0)
    chunk = lambda s: jnp.minimum(s, n_cast - 1)
    in_specs = [pl.BlockSpec((tm, kdim), lambda j, s: (row(s), 0)) for a in acts]
    args = list(acts)
    for wc in weights:
        assert wc.col0 % tn == 0 and wc.w.shape[1] == kdim
        c0, layer, stride = wc.col0 // tn, wc.layer, wc.stride
        last = pl.cdiv(wc.w.shape[2], tn) - 1
        in_specs.append(pl.BlockSpec(
            (None, kc, tn),
            lambda j, s, c0=c0, layer=layer, stride=stride, last=last:
            (layer, chunk(s), jnp.minimum(c0 + j * stride, last))))
        args.append(wc.w)
    assert len(weights) % fuse == 0
    scratch = [pltpu.VMEM((kdim, fuse * tn), BF16)] * (len(weights) // fuse)
    for arr, block, index_map in extras:
        in_specs.append(pl.BlockSpec(block, lambda j, s, index_map=index_map: index_map(j, row(s))))
        args.append(arr)
    return pl.pallas_call(
        functools.partial(_wres_body, n_act=len(acts), weights=tuple(wc._replace(w=None) for wc in weights),
                          fuse=fuse, n_extra=len(extras), epilogue=epilogue, kc=kc, n_cast=n_cast),
        grid=(n_out // tn_out, n_cast + tok // tm),
        in_specs=in_specs,
        out_specs=pl.BlockSpec((tm, tn_out), lambda j, s: (row(s), j)),
        out_shape=jax.ShapeDtypeStruct((tok, n_out), out_dtype),
        scratch_shapes=scratch,
        compiler_params=_params("arbitrary", "arbitrary"),
        name=name,
    )(*args)


def _proj_body(*refs, kc, n_cast, shift):
    if shift:
        a_ref, w_ref, nxt_ref, o_ref, scr = refs
    else:
        a_ref, w_ref, o_ref, scr = refs
    sweep, s = pl.program_id(0), pl.program_id(1)
    slot = sweep % 2

    def cast_chunk():
        rows = pl.ds(pl.multiple_of(jnp.minimum(s, n_cast - 1) * kc, kc), kc)
        w = w_ref[...]
        if shift:
            w = jnp.concatenate([w[shift:], nxt_ref[...]], axis=0)
        scr[slot, rows, :] = w.T.astype(BF16)

    @pl.when(sweep == 0)
    def _():
        cast_chunk()

    @pl.when(sweep > 0)
    def _():
        cast_chunk()
        o_ref[...] = _dot(a_ref[...], scr[1 - slot]).astype(o_ref.dtype)


def _proj(name, a, w_t, layer, row0, shift, n_out, cfg):
    tok, kdim = a.shape
    tm, tn, kc = cfg.tm, cfg.tn, cfg.kc
    n_cast, ni, nj = kdim // kc, tok // tm, n_out // tn
    assert kdim % kc == 0 and n_cast <= ni and row0 % tn == 0 and n_out % tn == 0
    c0 = row0 // tn
    tile = lambda t: jnp.minimum(t, nj - 1)
    chunk = lambda t, s: jnp.where(t == nj, n_cast - 1, jnp.minimum(s, n_cast - 1))
    row = lambda t, s: jnp.where(t == 0, 0, s)
    in_specs = [pl.BlockSpec((tm, kdim), lambda t, s: (row(t, s), 0)),
                pl.BlockSpec((None, tn, kc), lambda t, s: (layer, c0 + tile(t), chunk(t, s)))]
    args = [a, w_t]
    if shift:
        assert tn % shift == 0 and shift % 8 == 0
        per = tn // shift
        in_specs.append(pl.BlockSpec((None, shift, kc),
                                     lambda t, s: (layer, (c0 + tile(t) + 1) * per, chunk(t, s))))
        args.append(w_t)
    return pl.pallas_call(
        functools.partial(_proj_body, kc=kc, n_cast=n_cast, shift=shift),
        grid=(nj + 1, ni),
        in_specs=in_specs,
        out_specs=pl.BlockSpec((tm, tn), lambda t, s: (row(t, s), jnp.maximum(t - 1, 0))),
        out_shape=jax.ShapeDtypeStruct((tok, n_out), F32),
        scratch_shapes=[pltpu.VMEM((2, kdim, tn), BF16)],
        compiler_params=_params("arbitrary", "arbitrary"),
        name=name,
    )(*args)


def _bg_body(a_ref, w_ref, o_ref):
    o_ref[...] = _dot_nt(a_ref[...], w_ref[...].astype(BF16))


def _bg_proj(h, w_t, layer, row0, nbg, cfg):
    tok, k = h.shape
    tm = cfg.tm
    assert row0 % nbg == 0
    return pl.pallas_call(
        _bg_body,
        grid=(tok // tm,),
        in_specs=[pl.BlockSpec((tm, k), lambda i: (i, 0)),
                  pl.BlockSpec((None, nbg, k), lambda i: (layer, row0 // nbg, 0))],
        out_specs=pl.BlockSpec((tm, nbg), lambda i: (i, 0)),
        out_shape=jax.ShapeDtypeStruct((tok, nbg), F32),
        compiler_params=_params("arbitrary"),
        name="proj_bg",
    )(h, w_t)


def _merge_body(*refs, n_src, na):
    os, (y_ref, wa_ref, wb_ref, ga_ref, gb_ref, out_ref) = refs[:n_src], refs[n_src:]
    pa = _dot(_pick_rows(os, pl.program_id(0), na), wa_ref[...])
    pb = _dot(y_ref[...], wb_ref[...])
    out_ref[...] = (jax.nn.sigmoid(ga_ref[...]) * pa + jax.nn.sigmoid(gb_ref[...]) * pb).astype(out_ref.dtype)


def _merge(os, y, w_pa, w_pb, layer, p2, gate_col0, cfg):
    tok = y.shape[0]
    d = w_pa.shape[2]
    tm, tn = cfg.tm, min(cfg.tn // 2, d)
    ga0 = gate_col0 // tn
    gb0 = (gate_col0 + d) // tn
    return pl.pallas_call(
        functools.partial(_merge_body, n_src=len(os), na=os[0].shape[0] // tm),
        grid=(tok // tm, d // tn),
        in_specs=_row_sources(os, tm, os[0].shape[1], lambda i, j: (i, 0)) + [
            pl.BlockSpec((tm, y.shape[1]), lambda i, j: (i, 0)),
            pl.BlockSpec((None, w_pa.shape[1], tn), lambda i, j: (layer, 0, j)),
            pl.BlockSpec((None, w_pb.shape[1], tn), lambda i, j: (layer, 0, j)),
            pl.BlockSpec((tm, tn), lambda i, j: (i, ga0 + j)),
            pl.BlockSpec((tm, tn), lambda i, j: (i, gb0 + j))],
        out_specs=pl.BlockSpec((tm, tn), lambda i, j: (i, j)),
        out_shape=jax.ShapeDtypeStruct((tok, d), BF16),
        compiler_params=_params("arbitrary", "arbitrary"),
        name="merge",
    )(*os, y, w_pa, w_pb, p2, p2)


def _ffn_up_epilogue(prods, extras, o_ref, i, *, cfg):
    cw_ref, = extras
    tm, tn_out = o_ref.shape
    nsub = len(prods)
    tn = tn_out // nsub
    j = pl.program_id(0)
    is_ctx = i < cfg.tok_ctx // tm
    gw = cfg.grid_w
    lane = lax.broadcasted_iota(jnp.int32, (tm, tn), 1)
    t = lax.broadcasted_iota(jnp.int32, (tm, tn), 0) & (cfg.seq - 1)
    zeros = jnp.zeros((gw, tn), F32)
    for u in range(nsub):
        g, v = prods[u][:, :tn], prods[u][:, tn:]
        cols = slice(u * tn, (u + 1) * tn)
        w = cw_ref[:, cols]
        gm = jnp.where(is_ctx, jnp.where(t == 0, 0.0, pltpu.roll(g, 1, 0)),
                       jnp.concatenate([zeros, g[:tm - gw]], axis=0))
        gp = jnp.where(is_ctx, jnp.where(t == cfg.seq - 1, 0.0, pltpu.roll(g, tm - 1, 0)),
                       jnp.concatenate([g[gw:], zeros], axis=0))
        y = gm * w[0:1] + g * w[1:2] + gp * w[2:3]
        valid = lane + (j * tn_out + u * tn) < cfg.d_ff
        o_ref[:, cols] = jnp.where(valid, _silu(y) * v, 0.0).astype(o_ref.dtype)


def _resid_body(*refs, nk, n_src, na):
    a_ref, w_ref = refs[:2]
    xs, (gt_ref, o_ref) = refs[2:2 + n_src], refs[2 + n_src:]
    prod = lambda: _dot(a_ref[...], w_ref[...])
    finish = lambda acc: _pick_rows(xs, pl.program_id(0), na) + gt_ref[...] * acc
    if nk == 1:
        o_ref[...] = finish(prod())
        return
    k = pl.program_id(2)

    @pl.when(k == 0)
    def _():
        o_ref[...] = prod()

    @pl.when(jnp.logical_and(k > 0, k < nk - 1))
    def _():
        o_ref[...] += prod()

    @pl.when(k == nk - 1)
    def _():
        o_ref[...] = finish(o_ref[...] + prod())


def _resid(a, w, layer, xs, mod3, gt_idx, tk, cfg, name):
    tok, kdim = a.shape
    d = w.shape[2]
    tm, tn = cfg.tm, min(cfg.tn // len(xs), d)
    nk = kdim // tk
    return pl.pallas_call(
        functools.partial(_resid_body, nk=nk, n_src=len(xs), na=xs[0].shape[0] // tm),
        grid=(tok // tm, d // tn, nk),
        in_specs=[pl.BlockSpec((tm, tk), lambda i, j, k: (i, k)),
                  pl.BlockSpec((None, tk, tn), lambda i, j, k: (layer, k, j))]
        + _row_sources(xs, tm, tn, lambda i, j, k: (i, j))
        + [pl.BlockSpec((None, 1, tn), lambda i, j, k: (_mod_row(i, cfg, tm) * 6 + gt_idx, 0, j))],
        out_specs=pl.BlockSpec((tm, tn), lambda i, j, k: (i, j)),
        out_shape=jax.ShapeDtypeStruct((tok, d), F32),
        compiler_params=_params("arbitrary", "arbitrary", "arbitrary"),
        name=name,
    )(a, w, *xs, mod3)


def _cast_pad_body(w_ref, o_ref, *, rows_in):
    tr = o_ref.shape[0]
    row = lax.broadcasted_iota(jnp.int32, o_ref.shape, 0) + pl.program_id(1) * tr
    o_ref[...] = jnp.where(row < rows_in, w_ref[...], 0.0).astype(o_ref.dtype)


def _cast_pad_rows(w, rows_out, tr):
    depth, rows_in, n = w.shape
    last = pl.cdiv(rows_in, tr) - 1
    return pl.pallas_call(
        functools.partial(_cast_pad_body, rows_in=rows_in),
        grid=(depth, rows_out // tr),
        in_specs=[pl.BlockSpec((None, tr, n), lambda l, i: (l, jnp.minimum(i, last), 0))],
        out_specs=pl.BlockSpec((None, tr, n), lambda l, i: (l, i, 0)),
        out_shape=jax.ShapeDtypeStruct((depth, rows_out, n), BF16),
        compiler_params=_params("arbitrary", "arbitrary"),
        name="cast_pad",
    )(w)


def _sc_body(xs_ref, bs_ref, cs_ref, w_ref, o_ref, *, cfg):
    tm, tn = o_ref.shape
    i = pl.program_id(0)
    per = jnp.where(i < cfg.tok_ctx // tm, cfg.seq, cfg.grid_w)
    u = cs_ref[...] * xs_ref[...]
    t = lax.broadcasted_iota(jnp.int32, (tm, tn), 0) & (per - 1)
    um = jnp.where(t == 0, 0.0, pltpu.roll(u, 1, 0))
    up = jnp.where(t == per - 1, 0.0, pltpu.roll(u, tm - 1, 0))
    w = w_ref[...]
    o_ref[...] = (bs_ref[...] * (um * w[0:1] + u * w[1:2] + up * w[2:3])).astype(o_ref.dtype)


def _sc_mixer(p2, cw, cfg):
    tok = p2.shape[0]
    ws = cfg.w_sc
    tm, tn = cfg.tm, min(512, ws)
    nb = ws // tn
    return pl.pallas_call(
        functools.partial(_sc_body, cfg=cfg),
        grid=(tok // tm, nb),
        in_specs=[pl.BlockSpec((tm, tn), lambda i, j: (i, j)),
                  pl.BlockSpec((tm, tn), lambda i, j: (i, nb + j)),
                  pl.BlockSpec((tm, tn), lambda i, j: (i, 2 * nb + j)),
                  pl.BlockSpec((3, tn), lambda i, j: (0, j))],
        out_specs=pl.BlockSpec((tm, tn), lambda i, j: (i, j)),
        out_shape=jax.ShapeDtypeStruct((tok, ws), BF16),
        compiler_params=_params("arbitrary", "arbitrary"),
        name="sc_mixer",
    )(p2, p2, p2, cw)


def _dn_body(*refs, n, per, heads, hb, has_s0, emit_state, group):
    it = iter(refs)
    q_ref, k_ref, v_ref, z_ref = next(it), next(it), next(it), next(it)
    wq_ref, wk_ref, wv_ref = next(it), next(it), next(it)
    bgc_ref, bgt_ref, a_row_ref, dtb_row_ref, a_col_ref, dtb_col_ref, og_ref = (next(it) for _ in range(7))
    s0_ref = next(it) if has_s0 else None
    o_ref = next(it)
    st_ref = next(it) if emit_state else None
    qs, ks, vs, of_s, ob_s, qw_s, n_s, oc_s, eg_s, gcol_s, grow_s = it

    nc = n // CHUNK
    width = hb * LANES
    hblk = pl.program_id(1)
    head_cols = [slice(b * LANES, (b + 1) * LANES) for b in range(hb)]

    tok = lax.broadcasted_iota(jnp.int32, (n, width), 0) & (per - 1)
    first, last = tok == 0, tok == per - 1

    def conv_silu(x_ref, w_ref):
        x, w = x_ref[...], w_ref[...]
        xm = jnp.where(first, 0.0, pltpu.roll(x, 1, 0))
        xp = jnp.where(last, 0.0, pltpu.roll(x, n - 1, 0))
        return _silu(xm * w[0:1] + x * w[1:2] + xp * w[2:3])

    q = conv_silu(q_ref, wq_ref)
    k = conv_silu(k_ref, wk_ref)
    for sl in head_cols:
        qh, kh = q[:, sl], k[:, sl]
        qs[:, sl] = qh * lax.rsqrt(jnp.sum(qh * qh, axis=-1, keepdims=True) + EPS) * (LANES ** -0.5)
        ks[:, sl] = kh * lax.rsqrt(jnp.sum(kh * kh, axis=-1, keepdims=True) + EPS)
    vs[...] = conv_silu(v_ref, wv_ref)

    nbg = bgc_ref.shape[-1]
    lane_bg = lax.broadcasted_iota(jnp.int32, (CHUNK, nbg), 1)
    is_beta = lane_bg < 2 * heads
    neg_a_row, dtb_row = -jnp.exp(a_row_ref[...]), dtb_row_ref[...]
    neg_a_col, dtb_col = -jnp.exp(a_col_ref[...]), dtb_col_ref[...]
    fwd_heads = lax.broadcasted_iota(jnp.int32, (heads, 2 * CHUNK), 1) < CHUNK

    def gates(c, carry):
        bgc = bgc_ref[c]
        gcol_s[c] = jnp.where(is_beta, jax.nn.sigmoid(bgc), neg_a_row * _softplus(bgc + dtb_row))
        gt = neg_a_col * _softplus(bgt_ref[c] + dtb_col)
        grow_s[c] = jnp.where(fwd_heads, gt[2 * heads:3 * heads], gt[3 * heads:])
        return carry

    if nc == group:
        for c in range(nc):
            gates(c, 0)
    else:
        lax.fori_loop(0, nc, gates, 0)

    sub = lax.broadcasted_iota(jnp.int32, (CHUNK, 2 * CHUNK), 0)
    lane = lax.broadcasted_iota(jnp.int32, (CHUNK, 2 * CHUNK), 1)
    fwd = lane < CHUNK
    colx = lane & (CHUNK - 1)
    bwd = jnp.logical_not(fwd)
    vis_f, vis_b = jnp.logical_and(fwd, colx <= sub), jnp.logical_and(bwd, colx >= sub)
    vis = jnp.logical_or(vis_f, vis_b)
    vis_t = jnp.logical_or(jnp.logical_and(fwd, sub <= colx), jnp.logical_and(bwd, sub >= colx))
    strict = jnp.logical_and(vis, colx != sub)
    eye = (sub == colx).astype(F32)
    blk16 = (sub // 16) == (colx // 16)
    blk32 = (sub // 32) == (colx // 32)
    off32 = jnp.logical_and(blk32, jnp.logical_not(blk16))
    fwd_row = lax.broadcasted_iota(jnp.int32, (1, 2 * CHUNK), 1) < CHUNK
    zeros_x = jnp.zeros((CHUNK, 2 * LANES), BF16)

    def bd(y):
        return jnp.concatenate([jnp.where(fwd, y, 0.0).astype(BF16), jnp.where(fwd, 0.0, y).astype(BF16)], axis=0)

    def bd_x(xf, xb):
        return jnp.concatenate([jnp.concatenate([xf, zeros_x], axis=1),
                                jnp.concatenate([zeros_x, xb], axis=1)], axis=0)

    def stage_group(gi, carry):
        units = []
        for b in range(hb):
            for j in range(group):
                c = gi * group + j
                r0 = c * CHUNK
                if not isinstance(r0, int):
                    r0 = pl.multiple_of(r0, CHUNK)
                rows = pl.ds(r0, CHUNK)
                qc, kc, vc = qs[rows, head_cols[b]], ks[rows, head_cols[b]], vs[rows, head_cols[b]]
                kb = kc.astype(BF16)
                kb2 = jnp.concatenate([kb, kb], axis=0)
                kk = _dot_nt(kb, kb2)
                qk = _dot_nt(qc.astype(BF16), kb2)
                gall = gcol_s[c]
                hd = hblk * hb + b
                col = lambda g: jnp.sum(jnp.where(lane_bg == g * heads + hd, gall, 0.0), axis=1, keepdims=True)
                beta = (col(0), col(1))
                g_col = jnp.where(fwd, col(2), col(3))
                g_row = grow_s[c, pl.ds(hd, 1), :]
                g_rows = jnp.broadcast_to(g_row, (CHUNK, 2 * CHUNK))
                gc = (jnp.sum(jnp.where(vis_f, g_rows, 0.0), axis=1, keepdims=True),
                      jnp.sum(jnp.where(vis_b, g_rows, 0.0), axis=1, keepdims=True))
                gc_col = jnp.where(fwd, gc[0], gc[1])
                gc_row = jnp.sum(jnp.where(vis_t, g_col, 0.0), axis=0, keepdims=True)
                g_last = (jnp.sum(jnp.where(fwd_row, g_row, 0.0), axis=1, keepdims=True),
                          jnp.sum(jnp.where(fwd_row, 0.0, g_row), axis=1, keepdims=True))
                decay = jnp.where(vis, jnp.exp(jnp.where(vis, gc_col - gc_row, 0.0)), 0.0)
                e_gc = (jnp.exp(gc[0]), jnp.exp(gc[1]))
                units.append(dict(
                    b=b, c=c,
                    a=jnp.where(strict, kk * decay * jnp.where(fwd, beta[0], beta[1]), 0.0),
                    x=[jnp.concatenate([vc * beta[d], kc * (beta[d] * e_gc[d])], axis=1).astype(BF16)
                       for d in range(2)],
                    intra=jnp.where(vis, qk * decay, 0.0).astype(BF16),
                    k_dec=[(kc * jnp.exp(g_last[d] - gc[d])).astype(BF16) for d in range(2)],
                    q_dec=[qc * e_gc[d] for d in range(2)],
                    eg=[jnp.broadcast_to(jnp.exp(g_last[d]), (1, LANES)) for d in range(2)]))
        pmm = lambda x, y_bd: _dot(x.astype(BF16), y_bd)
        a_d = [jnp.where(blk16, u["a"], 0.0) for u in units]
        p2 = [pmm(x, bd(x)) for x in a_d]
        p2_bd = [bd(p) for p in p2]
        t = [eye - x for x in a_d]
        t = [x + pmm(x, p) for x, p in zip(t, p2_bd)]
        p4 = [pmm(p, pb) for p, pb in zip(p2, p2_bd)]
        p4_bd = [bd(p) for p in p4]
        t = [x + pmm(x, p) for x, p in zip(t, p4_bd)]
        p8 = [pmm(p, pb) for p, pb in zip(p4, p4_bd)]
        t = [x + pmm(x, bd(p)) for x, p in zip(t, p8)]
        mm = [pmm(x, bd(jnp.where(off32, u["a"], 0.0))) for x, u in zip(t, units)]
        t = [x - pmm(y, bd(x)) for x, y in zip(t, mm)]
        mm = [pmm(x, bd(jnp.where(blk32, 0.0, u["a"]))) for x, u in zip(t, units)]
        t = [x - pmm(y, bd(x)) for x, y in zip(t, mm)]
        xb = [pmm(x, bd_x(*u["x"])).astype(BF16) for x, u in zip(t, units)]
        xd = [[x[:, :2 * LANES], x[:, 2 * LANES:]] for x in xb]
        iw = [_dot(u["intra"], bd_x(*x)) for u, x in zip(units, xd)]
        kt = [[_dot_tn(u["k_dec"][d], x[d]) for d in range(2)] for u, x in zip(units, xd)]
        for u, iwu, ktu in zip(units, iw, kt):
            b, c = u["b"], u["c"]
            for d in range(2):
                iwd = iwu[:, 2 * LANES * d:2 * LANES * (d + 1)]
                oc_s[b, d, c] = iwd[:, :LANES]
                qw_s[b, d, c, 0:CHUNK, :] = (u["q_dec"][d] - iwd[:, LANES:]).astype(BF16)
                qw_s[b, d, c, CHUNK:, :] = ktu[d][:, LANES:].astype(BF16)
                n_s[b, d, c] = ktu[d][:, :LANES]
                eg_s[b, d, c] = u["eg"][d]
        return carry

    if nc == group:
        stage_group(0, 0)
    else:
        lax.fori_loop(0, nc // group, stage_group, 0)

    chains = [(b, d) for b in range(hb) for d in range(2)]
    if has_s0:
        s_init = tuple(s0_ref[d, b] for b, d in chains)
    else:
        s_init = tuple(jnp.zeros((LANES, LANES), F32) for _ in chains)

    def scan_step(t, carry):
        res = []
        for (b, d), s in zip(chains, carry):
            c = t if d == 0 else nc - 1 - t
            res.append(_dot(qw_s[b, d, c], s.astype(BF16)))
        out = []
        for (b, d), s, r in zip(chains, carry, res):
            c = t if d == 0 else nc - 1 - t
            r0 = c * CHUNK
            if not isinstance(r0, int):
                r0 = pl.multiple_of(r0, CHUNK)
            acc = of_s if d == 0 else ob_s
            acc[pl.ds(r0, CHUNK), head_cols[b]] = r[:CHUNK] + oc_s[b, d, c]
            out.append(s * eg_s[b, d, c] - r[CHUNK:] + n_s[b, d, c])
        return tuple(out)

    if nc == group:
        s_fin = s_init
        for t in range(nc):
            s_fin = scan_step(t, s_fin)
    else:
        s_fin = lax.fori_loop(0, nc, scan_step, s_init)

    o = of_s[...] + ob_s[...]
    z = _silu(z_ref[...])
    og = og_ref[...]
    for sl in head_cols:
        oh = o[:, sl]
        y = oh * lax.rsqrt(jnp.mean(oh * oh, axis=-1, keepdims=True) + EPS) * og
        o_ref[:, sl] = (y * z[:, sl]).astype(o_ref.dtype)
    if emit_state:
        for (b, d), s in zip(chains, s_fin):
            st_ref[d, b] = s


def _deltanet(p1, bg, conv_qkv, a_log, dt_bias, onorm_g, s0, cfg, *, latent):
    hh, hb = cfg.heads, (cfg.dn_heads_lat if latent else cfg.dn_heads)
    tok, nbg = bg.shape
    n = cfg.dec_seq if latent else cfg.seq
    nseq = cfg.dec_batch if latent else cfg.batch
    blk0 = cfg.tok_ctx // n if latent else 0
    per = cfg.grid_w if latent else cfg.seq
    nc = n // CHUNK
    group = min(4, nc)
    width = hb * LANES
    nhb = hh // hb
    assert nbg == 4 * hh
    bgc = bg.reshape(tok // CHUNK, CHUNK, nbg)
    bgt = jnp.tile(bgc.transpose(0, 2, 1), (1, 1, 2))
    a_row = jnp.concatenate([jnp.zeros((2 * hh,), F32), a_log.reshape(-1)]).reshape(1, nbg)
    dtb_row = jnp.concatenate([jnp.zeros((2 * hh,), F32), dt_bias.reshape(-1)]).reshape(1, nbg)

    col = lambda g: pl.BlockSpec((n, width), lambda s, h: (blk0 + s, g * nhb + h))
    cw = lambda g: pl.BlockSpec((3, width), lambda s, h: (0, g * nhb + h))
    whole = lambda shape: pl.BlockSpec(shape, lambda s, h: (0,) * len(shape))
    in_specs = [col(0), col(1), col(2), col(3), cw(0), cw(1), cw(2),
                pl.BlockSpec((nc, CHUNK, nbg), lambda s, h: (blk0 + s, 0, 0)),
                pl.BlockSpec((nc, nbg, 2 * CHUNK), lambda s, h: (blk0 + s, 0, 0)),
                whole((1, nbg)), whole((1, nbg)), whole((nbg, 1)), whole((nbg, 1)), whole((1, LANES))]
    args = [p1, p1, p1, p1, conv_qkv, conv_qkv, conv_qkv, bgc, bgt, a_row, dtb_row,
            a_row.reshape(nbg, 1), dtb_row.reshape(nbg, 1), onorm_g.reshape(1, LANES)]
    st_spec = pl.BlockSpec((None, 2, hb, LANES, LANES), lambda s, h: (s, 0, h, 0, 0))
    if latent:
        in_specs.append(st_spec)
        args.append(s0)
    out_specs = [pl.BlockSpec((n, width), lambda s, h: (s, h))]
    out_shape = [jax.ShapeDtypeStruct((nseq * n, hh * LANES), BF16)]
    if not latent:
        out_specs.append(st_spec)
        out_shape.append(jax.ShapeDtypeStruct((nseq, 2, hh, LANES, LANES), F32))
    scratch = [pltpu.VMEM((n, width), F32)] * 5 + [
        pltpu.VMEM((hb, 2, nc, CHUNK + LANES, LANES), BF16), pltpu.VMEM((hb, 2, nc, LANES, LANES), F32),
        pltpu.VMEM((hb, 2, nc, CHUNK, LANES), F32), pltpu.VMEM((hb, 2, nc, 1, LANES), F32),
        pltpu.VMEM((nc, CHUNK, nbg), F32), pltpu.VMEM((nc, hh, 2 * CHUNK), F32)]
    return pl.pallas_call(
        functools.partial(_dn_body, n=n, per=per, heads=hh, hb=hb, has_s0=latent, emit_state=not latent,
                          group=group),
        grid=(nseq, nhb),
        in_specs=in_specs,
        out_specs=out_specs,
        out_shape=out_shape,
        scratch_shapes=scratch,
        compiler_params=_params("arbitrary", "arbitrary"),
        name="deltanet_lat" if latent else "deltanet_ctx",
    )(*args)


def _forward(cfg, x_prompt, x_sample, state_dn, c, c_ctx, norm1_g, norm2_g, w_ada, b_ada, w_in,
             conv_qkv, a_log, dt_bias, onorm_g, conv_sc, w_pa, w_pb, w_o, w_up, conv_ff,
             w_down, final_g):
    d, hh = cfg.d, cfg.heads
    w1_n = 4 * cfg.w_hd
    nbg = 4 * hh
    xs = (x_prompt.reshape(cfg.tok_ctx, d), x_sample.reshape(-1, d))
    cvec = jnp.zeros((MOD_ROWS, d), F32).at[0].set(c_ctx).at[1:1 + cfg.dec_batch].set(c)
    mod = _ada(cvec, w_ada, b_ada)
    w_in_t = jnp.swapaxes(w_in, 1, 2)
    w_pa_b, w_pb_b, w_o_b = w_pa.astype(BF16), w_pb.astype(BF16), w_o.astype(BF16)
    w_down_b = _cast_pad_rows(w_down, cfg.ff_pad, CAST_PAD_ROWS)
    states = []
    n2 = 3 * cfg.w_sc + 2 * d
    ff_tn, ff_sub = cfg.ff_tn, FFN_SUBTILES
    ff_w = ff_tn // ff_sub
    assert cfg.tm == cfg.dec_seq and cfg.tm % cfg.seq == 0 and cfg.d_ff % ff_w == 0 and cfg.ff_pad % ff_tn == 0
    ff_weights = [WCols(w_up, 0, half * cfg.d_ff + u * ff_w, 0, ff_sub)
                  for u in range(ff_sub) for half in range(2)]
    for l in range(cfg.depth):
        mod3 = mod[l].reshape(MOD_ROWS * 6, 1, d)
        h = _norm_mod(xs, norm1_g[l], mod3, 0, 1, cfg)
        p1 = _proj("proj_qkvz", h, w_in_t, l, 0, 0, w1_n, cfg)
        bg = _bg_proj(h, w_in_t, l, w1_n, nbg, cfg)
        p2 = _proj("proj_sc_gates", h, w_in_t, l, w1_n, nbg, n2, cfg)
        o_ctx, st = _deltanet(p1, bg, conv_qkv[l], a_log[l], dt_bias[l], onorm_g[l], None, cfg, latent=False)
        o_lat, = _deltanet(p1, bg, conv_qkv[l], a_log[l], dt_bias[l], onorm_g[l], state_dn[:, l], cfg, latent=True)
        states.append(st)
        y_sc = _sc_mixer(p2, conv_sc[l], cfg)
        merged = _merge((o_ctx, o_lat), y_sc, w_pa_b, w_pb_b, l, p2, 3 * cfg.w_sc, cfg)
        xs = (_resid(merged, w_o_b, l, xs, mod3, 2, d, cfg, "out_proj"),)
        h = _norm_mod(xs, norm2_g[l], mod3, 3, 4, cfg)
        f = _wres_matmul(
            "ffn_up", [h], [wc._replace(layer=l) for wc in ff_weights], 2,
            [(conv_ff, (None, 3, ff_tn), lambda j, i, l=l: (l, 0, j))],
            functools.partial(_ffn_up_epilogue, cfg=cfg), cfg.ff_pad, BF16, cfg.tm, ff_w, ff_tn, cfg.kc)
        xs = (_resid(f, w_down_b, l, xs, mod3, 5, cfg.ff_tk, cfg, "ffn_down"),)
    x, = xs
    y_prompt = _final_norm(x, final_g, 0, cfg.tok_ctx, cfg).reshape(x_prompt.shape)
    y_sample = _final_norm(x, final_g, cfg.tok_ctx, cfg.tok - cfg.tok_ctx, cfg).reshape(x_sample.shape)
    return y_prompt, y_sample, jnp.stack(states, axis=1)


def kernel(x_prompt, x_sample, state_dn, c, c_ctx, norm1_g, norm2_g, w_ada, b_ada, w_in, conv_qkv, a_log, dt_bias, onorm_g, conv_sc, w_pa, w_pb, w_o, w_up, conv_ff, w_down, final_g):
    return _forward(CFG, x_prompt, x_sample, state_dn, c, c_ctx, norm1_g, norm2_g, w_ada, b_ada, w_in,
                    conv_qkv, a_log, dt_bias, onorm_g, conv_sc, w_pa, w_pb, w_o, w_up, conv_ff,
                    w_down, final_g)
```

```python
import functools
from typing import NamedTuple

import jax
import jax.numpy as jnp
from jax import lax
from jax.experimental import pallas as pl
from jax.experimental.pallas import tpu as pltpu

F32 = jnp.float32
BF16 = jnp.bfloat16
EPS = 1e-6
CHUNK = 64
LANES = 128
MOD_ROWS = 8
V7X_VMEM_BYTES = 64 * 1024 * 1024
VMEM_LIMIT_BYTES = V7X_VMEM_BYTES - 6 * 1024 * 1024


class Cfg(NamedTuple):
    d: int
    batch: int
    seq: int
    depth: int
    dec_batch: int
    dec_seq: int
    grid_w: int
    heads: int
    w_sc: int
    d_ff: int
    tm: int
    tn: int
    ff_tn: int
    ff_tk: int
    dn_heads: int
    kc: int
    dn_heads_lat: int = 4

    @property
    def w_hd(self):
        return self.heads * LANES

    @property
    def tok_ctx(self):
        return self.batch * self.seq

    @property
    def tok(self):
        return self.batch * self.seq + self.dec_batch * self.dec_seq

    @property
    def ff_pad(self):
        return -(-self.d_ff // self.ff_tk) * self.ff_tk


CFG = Cfg(d=4096, batch=32, seq=256, depth=2, dec_batch=2, dec_seq=1024, grid_w=64, heads=16,
          w_sc=2048, d_ff=11008, tm=1024, tn=1024, ff_tn=1024, ff_tk=2816, dn_heads=8, kc=512)
FFN_SUBTILES = 4
CAST_PAD_ROWS = 256
DN_WAVE_HEADS = 4


def _params(*sem):
    return pltpu.CompilerParams(dimension_semantics=sem, vmem_limit_bytes=VMEM_LIMIT_BYTES)


def _mod_row(i, cfg, tm):
    nct = cfg.tok_ctx // tm
    return jnp.where(i < nct, 0, 1 + (i - nct) // (cfg.dec_seq // tm))


def _silu(x):
    return x * jax.nn.sigmoid(x)


def _softplus(x):
    return jnp.maximum(x, 0.0) + jnp.log1p(jnp.exp(-jnp.abs(x)))


def _dot(a, b):
    return jnp.dot(a, b, preferred_element_type=F32)


def _dotb(a, b):
    return _dot(a.astype(BF16), b.astype(BF16))


def _dot_nt(a, b):
    return lax.dot_general(a, b, (((1,), (1,)), ((), ())), preferred_element_type=F32)


def _dot_tn(a, b):
    return lax.dot_general(a, b, (((0,), (0,)), ((), ())), preferred_element_type=F32)


def _ada_body(c_ref, w_ref, b_ref, o_ref):
    s = _silu(c_ref[...]).astype(BF16)
    o_ref[...] = _dot(s, w_ref[...].astype(BF16)) + b_ref[...]


def _ada(cvec, w_ada, b_ada):
    depth, d, n = w_ada.shape
    tn = 512
    return pl.pallas_call(
        _ada_body,
        grid=(depth, n // tn),
        in_specs=[pl.BlockSpec((MOD_ROWS, d), lambda l, j: (0, 0)),
                  pl.BlockSpec((None, d, tn), lambda l, j: (l, 0, j)),
                  pl.BlockSpec((None, 1, tn), lambda l, j: (l, 0, j))],
        out_specs=pl.BlockSpec((None, MOD_ROWS, tn), lambda l, j: (l, 0, j)),
        out_shape=jax.ShapeDtypeStruct((depth, MOD_ROWS, n), F32),
        compiler_params=_params("arbitrary", "arbitrary"),
        name="ada",
    )(cvec, w_ada, b_ada.reshape(depth, 1, n))


def _row_sources(srcs, tm, cols, index_fn):
    if len(srcs) == 1:
        return [pl.BlockSpec((tm, cols), index_fn)]
    na = srcs[0].shape[0] // tm

    def first(*g):
        i, j = index_fn(*g)
        return jnp.minimum(i, na - 1), j

    def second(*g):
        i, j = index_fn(*g)
        return jnp.maximum(i - na, 0), j

    return [pl.BlockSpec((tm, cols), first), pl.BlockSpec((tm, cols), second)]


def _pick_rows(refs, i, na):
    if len(refs) == 1:
        return refs[0][...]
    return jnp.where(i < na, refs[0][...], refs[1][...])


def _norm_mod_body(*refs, n_src, na):
    xs, (g_ref, sh_ref, sc_ref, o_ref) = refs[:n_src], refs[n_src:]
    x = _pick_rows(xs, pl.program_id(0), na)
    y = x * lax.rsqrt(jnp.mean(x * x, axis=-1, keepdims=True) + EPS) * g_ref[...]
    o_ref[...] = (y * (1.0 + sc_ref[...]) + sh_ref[...]).astype(o_ref.dtype)


def _norm_mod(xs, g, mod3, sh_idx, sc_idx, cfg):
    tok, d = sum(x.shape[0] for x in xs), xs[0].shape[1]
    tm = cfg.tm // 2
    spec_mod = lambda idx: pl.BlockSpec((None, 1, d), lambda i: (_mod_row(i, cfg, tm) * 6 + idx, 0, 0))
    return pl.pallas_call(
        functools.partial(_norm_mod_body, n_src=len(xs), na=xs[0].shape[0] // tm),
        grid=(tok // tm,),
        in_specs=_row_sources(xs, tm, d, lambda i: (i, 0)) + [
            pl.BlockSpec((1, d), lambda i: (0, 0)), spec_mod(sh_idx), spec_mod(sc_idx)],
        out_specs=pl.BlockSpec((tm, d), lambda i: (i, 0)),
        out_shape=jax.ShapeDtypeStruct((tok, d), BF16),
        compiler_params=_params("arbitrary"),
        name="norm_mod",
    )(*xs, g.reshape(1, d), mod3, mod3)


def _final_norm_body(x_ref, g_ref, o_ref):
    x = x_ref[...]
    o_ref[...] = x * lax.rsqrt(jnp.mean(x * x, axis=-1, keepdims=True) + EPS) * g_ref[...]


def _final_norm(x, g, row0, rows, cfg):
    d = x.shape[1]
    tm = cfg.tm // 2
    off = row0 // tm
    return pl.pallas_call(
        _final_norm_body,
        grid=(rows // tm,),
        in_specs=[pl.BlockSpec((tm, d), lambda i: (i + off, 0)),
                  pl.BlockSpec((1, d), lambda i: (0, 0))],
        out_specs=pl.BlockSpec((tm, d), lambda i: (i, 0)),
        out_shape=jax.ShapeDtypeStruct((rows, d), F32),
        compiler_params=_params("arbitrary"),
        name="final_norm",
    )(x, g.reshape(1, d))


class WCols(NamedTuple):
    w: jax.Array
    layer: int
    col0: int
    act: int
    stride: int


def _wres_body(*refs, n_act, weights, fuse, n_extra, epilogue, kc, n_cast):
    acts = refs[:n_act]
    w_refs = refs[n_act:n_act + len(weights)]
    pos = n_act + len(weights)
    extras = refs[pos:pos + n_extra]
    o_ref = refs[pos + n_extra]
    scrs = refs[pos + n_extra + 1:]
    s = pl.program_id(1)

    @pl.when(s < n_cast)
    def _():
        rows = pl.ds(pl.multiple_of(s * kc, kc), kc)
        for k, w_ref in enumerate(w_refs):
            tn = w_ref.shape[1]
            scrs[k // fuse][rows, (k % fuse) * tn:(k % fuse + 1) * tn] = w_ref[...].astype(BF16)

    @pl.when(s >= n_cast)
    def _():
        prods = [_dot(acts[weights[g * fuse].act][...], scr[...]) for g, scr in enumerate(scrs)]
        epilogue(prods, extras, o_ref, s - n_cast)


def _wres_matmul(name, acts, weights, fuse, extras, epilogue, n_out, out_dtype, tm, tn, tn_out, kc):
    tok = acts[0].shape[0]
    kdim = acts[0].shape[1]
    n_cast = kdim // kc
    assert kdim % kc == 0 and all(a.shape[1] == kdim for a in acts)
    row = lambda s: jnp.maximum(s - n_cast, 0)
    chunk = lambda s: jnp.minimum(s, n_cast - 1)
    in_specs = [pl.BlockSpec((tm, kdim), lambda j, s: (row(s), 0)) for a in acts]
    args = list(acts)
    for wc in weights:
        assert wc.col0 % tn == 0 and wc.w.shape[1] == kdim
        c0, layer, stride = wc.col0 // tn, wc.layer, wc.stride
        last = pl.cdiv(wc.w.shape[2], tn) - 1
        in_specs.append(pl.BlockSpec(
            (None, kc, tn),
            lambda j, s, c0=c0, layer=layer, stride=stride, last=last:
            (layer, chunk(s), jnp.minimum(c0 + j * stride, last))))
        args.append(wc.w)
    assert len(weights) % fuse == 0
    scratch = [pltpu.VMEM((kdim, fuse * tn), BF16)] * (len(weights) // fuse)
    for arr, block, index_map in extras:
        in_specs.append(pl.BlockSpec(block, lambda j, s, index_map=index_map: index_map(j, row(s))))
        args.append(arr)
    return pl.pallas_call(
        functools.partial(_wres_body, n_act=len(acts), weights=tuple(wc._replace(w=None) for wc in weights),
                          fuse=fuse, n_extra=len(extras), epilogue=epilogue, kc=kc, n_cast=n_cast),
        grid=(n_out // tn_out, n_cast + tok // tm),
        in_specs=in_specs,
        out_specs=pl.BlockSpec((tm, tn_out), lambda j, s: (row(s), j)),
        out_shape=jax.ShapeDtypeStruct((tok, n_out), out_dtype),
        scratch_shapes=scratch,
        compiler_params=_params("arbitrary", "arbitrary"),
        name=name,
    )(*args)


def _proj_body(*refs, kc, n_cast, shift):
    if shift:
        a_ref, w_ref, nxt_ref, o_ref, scr = refs
    else:
        a_ref, w_ref, o_ref, scr = refs
    sweep, s = pl.program_id(0), pl.program_id(1)
    slot = sweep % 2

    def cast_chunk():
        rows = pl.ds(pl.multiple_of(jnp.minimum(s, n_cast - 1) * kc, kc), kc)
        w = w_ref[...]
        if shift:
            w = jnp.concatenate([w[shift:], nxt_ref[...]], axis=0)
        scr[slot, rows, :] = w.T.astype(BF16)

    @pl.when(sweep == 0)
    def _():
        cast_chunk()

    @pl.when(sweep > 0)
    def _():
        cast_chunk()
        o_ref[...] = _dot(a_ref[...], scr[1 - slot]).astype(o_ref.dtype)


def _proj(name, a, w_t, layer, row0, shift, n_out, cfg):
    tok, kdim = a.shape
    tm, tn, kc = cfg.tm, cfg.tn, cfg.kc
    n_cast, ni, nj = kdim // kc, tok // tm, n_out // tn
    assert kdim % kc == 0 and n_cast <= ni and row0 % tn == 0 and n_out % tn == 0
    c0 = row0 // tn
    tile = lambda t: jnp.minimum(t, nj - 1)
    chunk = lambda t, s: jnp.where(t == nj, n_cast - 1, jnp.minimum(s, n_cast - 1))
    row = lambda t, s: jnp.where(t == 0, 0, s)
    in_specs = [pl.BlockSpec((tm, kdim), lambda t, s: (row(t, s), 0)),
                pl.BlockSpec((None, tn, kc), lambda t, s: (layer, c0 + tile(t), chunk(t, s)))]
    args = [a, w_t]
    if shift:
        assert tn % shift == 0 and shift % 8 == 0
        per = tn // shift
        in_specs.append(pl.BlockSpec((None, shift, kc),
                                     lambda t, s: (layer, (c0 + tile(t) + 1) * per, chunk(t, s))))
        args.append(w_t)
    return pl.pallas_call(
        functools.partial(_proj_body, kc=kc, n_cast=n_cast, shift=shift),
        grid=(nj + 1, ni),
        in_specs=in_specs,
        out_specs=pl.BlockSpec((tm, tn), lambda t, s: (row(t, s), jnp.maximum(t - 1, 0))),
        out_shape=jax.ShapeDtypeStruct((tok, n_out), F32),
        scratch_shapes=[pltpu.VMEM((2, kdim, tn), BF16)],
        compiler_params=_params("arbitrary", "arbitrary"),
        name=name,
    )(*args)


def _bg_body(a_ref, w_ref, o_ref):
    o_ref[...] = _dot_nt(a_ref[...], w_ref[...].astype(BF16))


def _bg_proj(h, w_t, layer, row0, nbg, cfg):
    tok, k = h.shape
    tm = cfg.tm
    assert row0 % nbg == 0
    return pl.pallas_call(
        _bg_body,
        grid=(tok // tm,),
        in_specs=[pl.BlockSpec((tm, k), lambda i: (i, 0)),
                  pl.BlockSpec((None, nbg, k), lambda i: (layer, row0 // nbg, 0))],
        out_specs=pl.BlockSpec((tm, nbg), lambda i: (i, 0)),
        out_shape=jax.ShapeDtypeStruct((tok, nbg), F32),
        compiler_params=_params("arbitrary"),
        name="proj_bg",
    )(h, w_t)


def _merge_body(*refs, n_src, na):
    os, (y_ref, wa_ref, wb_ref, ga_ref, gb_ref, out_ref) = refs[:n_src], refs[n_src:]
    pa = _dot(_pick_rows(os, pl.program_id(0), na), wa_ref[...])
    pb = _dot(y_ref[...], wb_ref[...])
    out_ref[...] = (jax.nn.sigmoid(ga_ref[...]) * pa + jax.nn.sigmoid(gb_ref[...]) * pb).astype(out_ref.dtype)


def _merge(os, y, w_pa, w_pb, layer, p2, gate_col0, cfg):
    tok = y.shape[0]
    d = w_pa.shape[2]
    tm, tn = cfg.tm, min(cfg.tn // 2, d)
    ga0 = gate_col0 // tn
    gb0 = (gate_col0 + d) // tn
    return pl.pallas_call(
        functools.partial(_merge_body, n_src=len(os), na=os[0].shape[0] // tm),
        grid=(tok // tm, d // tn),
        in_specs=_row_sources(os, tm, os[0].shape[1], lambda i, j: (i, 0)) + [
            pl.BlockSpec((tm, y.shape[1]), lambda i, j: (i, 0)),
            pl.BlockSpec((None, w_pa.shape[1], tn), lambda i, j: (layer, 0, j)),
            pl.BlockSpec((None, w_pb.shape[1], tn), lambda i, j: (layer, 0, j)),
            pl.BlockSpec((tm, tn), lambda i, j: (i, ga0 + j)),
            pl.BlockSpec((tm, tn), lambda i, j: (i, gb0 + j))],
        out_specs=pl.BlockSpec((tm, tn), lambda i, j: (i, j)),
        out_shape=jax.ShapeDtypeStruct((tok, d), BF16),
        compiler_params=_params("arbitrary", "arbitrary"),
        name="merge",
    )(*os, y, w_pa, w_pb, p2, p2)


def _ffn_up_epilogue(prods, extras, o_ref, i, *, cfg):
    cw_ref, = extras
    tm, tn_out = o_ref.shape
    nsub = len(prods)
    tn = tn_out // nsub
    j = pl.program_id(0)
    is_ctx = i < cfg.tok_ctx // tm
    gw = cfg.grid_w
    lane = lax.broadcasted_iota(jnp.int32, (tm, tn), 1)
    t = lax.broadcasted_iota(jnp.int32, (tm, tn), 0) & (cfg.seq - 1)
    zeros = jnp.zeros((gw, tn), F32)
    for u in range(nsub):
        g, v = prods[u][:, :tn], prods[u][:, tn:]
        cols = slice(u * tn, (u + 1) * tn)
        w = cw_ref[:, cols]
        gm = jnp.where(is_ctx, jnp.where(t == 0, 0.0, pltpu.roll(g, 1, 0)),
                       jnp.concatenate([zeros, g[:tm - gw]], axis=0))
        gp = jnp.where(is_ctx, jnp.where(t == cfg.seq - 1, 0.0, pltpu.roll(g, tm - 1, 0)),
                       jnp.concatenate([g[gw:], zeros], axis=0))
        y = gm * w[0:1] + g * w[1:2] + gp * w[2:3]
        valid = lane + (j * tn_out + u * tn) < cfg.d_ff
        o_ref[:, cols] = jnp.where(valid, _silu(y) * v, 0.0).astype(o_ref.dtype)


def _resid_body(*refs, nk, n_src, na):
    a_ref, w_ref = refs[:2]
    xs, (gt_ref, o_ref) = refs[2:2 + n_src], refs[2 + n_src:]
    prod = lambda: _dot(a_ref[...], w_ref[...])
    finish = lambda acc: _pick_rows(xs, pl.program_id(0), na) + gt_ref[...] * acc
    if nk == 1:
        o_ref[...] = finish(prod())
        return
    k = pl.program_id(2)

    @pl.when(k == 0)
    def _():
        o_ref[...] = prod()

    @pl.when(jnp.logical_and(k > 0, k < nk - 1))
    def _():
        o_ref[...] += prod()

    @pl.when(k == nk - 1)
    def _():
        o_ref[...] = finish(o_ref[...] + prod())


def _resid(a, w, layer, xs, mod3, gt_idx, tk, cfg, name):
    tok, kdim = a.shape
    d = w.shape[2]
    tm, tn = cfg.tm, min(cfg.tn // len(xs), d)
    nk = kdim // tk
    return pl.pallas_call(
        functools.partial(_resid_body, nk=nk, n_src=len(xs), na=xs[0].shape[0] // tm),
        grid=(tok // tm, d // tn, nk),
        in_specs=[pl.BlockSpec((tm, tk), lambda i, j, k: (i, k)),
                  pl.BlockSpec((None, tk, tn), lambda i, j, k: (layer, k, j))]
        + _row_sources(xs, tm, tn, lambda i, j, k: (i, j))
        + [pl.BlockSpec((None, 1, tn), lambda i, j, k: (_mod_row(i, cfg, tm) * 6 + gt_idx, 0, j))],
        out_specs=pl.BlockSpec((tm, tn), lambda i, j, k: (i, j)),
        out_shape=jax.ShapeDtypeStruct((tok, d), F32),
        compiler_params=_params("arbitrary", "arbitrary", "arbitrary"),
        name=name,
    )(a, w, *xs, mod3)


def _cast_pad_body(w_ref, o_ref, *, rows_in):
    tr = o_ref.shape[0]
    row = lax.broadcasted_iota(jnp.int32, o_ref.shape, 0) + pl.program_id(1) * tr
    o_ref[...] = jnp.where(row < rows_in, w_ref[...], 0.0).astype(o_ref.dtype)


def _cast_pad_rows(w, rows_out, tr):
    depth, rows_in, n = w.shape
    last = pl.cdiv(rows_in, tr) - 1
    return pl.pallas_call(
        functools.partial(_cast_pad_body, rows_in=rows_in),
        grid=(depth, rows_out // tr),
        in_specs=[pl.BlockSpec((None, tr, n), lambda l, i: (l, jnp.minimum(i, last), 0))],
        out_specs=pl.BlockSpec((None, tr, n), lambda l, i: (l, i, 0)),
        out_shape=jax.ShapeDtypeStruct((depth, rows_out, n), BF16),
        compiler_params=_params("arbitrary", "arbitrary"),
        name="cast_pad",
    )(w)


def _sc_body(xs_ref, bs_ref, cs_ref, w_ref, o_ref, *, cfg):
    tm, tn = o_ref.shape
    i = pl.program_id(0)
    per = jnp.where(i < cfg.tok_ctx // tm, cfg.seq, cfg.grid_w)
    u = cs_ref[...] * xs_ref[...]
    t = lax.broadcasted_iota(jnp.int32, (tm, tn), 0) & (per - 1)
    um = jnp.where(t == 0, 0.0, pltpu.roll(u, 1, 0))
    up = jnp.where(t == per - 1, 0.0, pltpu.roll(u, tm - 1, 0))
    w = w_ref[...]
    o_ref[...] = (bs_ref[...] * (um * w[0:1] + u * w[1:2] + up * w[2:3])).astype(o_ref.dtype)


def _sc_mixer(p2, cw, cfg):
    tok = p2.shape[0]
    ws = cfg.w_sc
    tm, tn = cfg.tm, min(512, ws)
    nb = ws // tn
    return pl.pallas_call(
        functools.partial(_sc_body, cfg=cfg),
        grid=(tok // tm, nb),
        in_specs=[pl.BlockSpec((tm, tn), lambda i, j: (i, j)),
                  pl.BlockSpec((tm, tn), lambda i, j: (i, nb + j)),
                  pl.BlockSpec((tm, tn), lambda i, j: (i, 2 * nb + j)),
                  pl.BlockSpec((3, tn), lambda i, j: (0, j))],
        out_specs=pl.BlockSpec((tm, tn), lambda i, j: (i, j)),
        out_shape=jax.ShapeDtypeStruct((tok, ws), BF16),
        compiler_params=_params("arbitrary", "arbitrary"),
        name="sc_mixer",
    )(p2, p2, p2, cw)


def _dn_body(*refs, n, per, heads, hb, has_s0, emit_state, n_prev, group):
    it = iter(refs)
    q_ref, k_ref, v_ref, z_ref = next(it), next(it), next(it), next(it)
    wq_ref, wk_ref, wv_ref = next(it), next(it), next(it)
    bgc_ref, bgt_ref, a_row_ref, dtb_row_ref, a_col_ref, dtb_col_ref, og_ref = (next(it) for _ in range(7))
    s0_ref = next(it) if has_s0 else None
    prev_ref = next(it) if n_prev else None
    o_ref = next(it)
    st_ref = next(it) if emit_state else None
    qs, ks, vs, of_s, ob_s, qw_s, n_s, oc_s, eg_s, gcol_s, grow_s = it

    nc = n // CHUNK
    width = hb * LANES
    hblk = pl.program_id(1)
    head_cols = [slice(b * LANES, (b + 1) * LANES) for b in range(hb)]

    tok = lax.broadcasted_iota(jnp.int32, (n, width), 0) & (per - 1)
    first, last = tok == 0, tok == per - 1

    def conv_silu(x_ref, w_ref):
        x, w = x_ref[...], w_ref[...]
        xm = jnp.where(first, 0.0, pltpu.roll(x, 1, 0))
        xp = jnp.where(last, 0.0, pltpu.roll(x, n - 1, 0))
        return _silu(xm * w[0:1] + x * w[1:2] + xp * w[2:3])

    q = conv_silu(q_ref, wq_ref)
    k = conv_silu(k_ref, wk_ref)
    for sl in head_cols:
        qh, kh = q[:, sl], k[:, sl]
        qs[:, sl] = qh * lax.rsqrt(jnp.sum(qh * qh, axis=-1, keepdims=True) + EPS) * (LANES ** -0.5)
        ks[:, sl] = kh * lax.rsqrt(jnp.sum(kh * kh, axis=-1, keepdims=True) + EPS)
    vs[...] = conv_silu(v_ref, wv_ref)

    nbg = bgc_ref.shape[-1]
    lane_bg = lax.broadcasted_iota(jnp.int32, (CHUNK, nbg), 1)
    is_beta = lane_bg < 2 * heads
    neg_a_row, dtb_row = -jnp.exp(a_row_ref[...]), dtb_row_ref[...]
    neg_a_col, dtb_col = -jnp.exp(a_col_ref[...]), dtb_col_ref[...]
    fwd_heads = lax.broadcasted_iota(jnp.int32, (heads, 2 * CHUNK), 1) < CHUNK

    def gates(c, carry):
        bgc = bgc_ref[c]
        gcol_s[c] = jnp.where(is_beta, jax.nn.sigmoid(bgc), neg_a_row * _softplus(bgc + dtb_row))
        gt = neg_a_col * _softplus(bgt_ref[c] + dtb_col)
        grow_s[c] = jnp.where(fwd_heads, gt[2 * heads:3 * heads], gt[3 * heads:])
        return carry

    if nc == group:
        for c in range(nc):
            gates(c, 0)
    else:
        lax.fori_loop(0, nc, gates, 0)

    sub = lax.broadcasted_iota(jnp.int32, (CHUNK, 2 * CHUNK), 0)
    lane = lax.broadcasted_iota(jnp.int32, (CHUNK, 2 * CHUNK), 1)
    fwd = lane < CHUNK
    colx = lane & (CHUNK - 1)
    bwd = jnp.logical_not(fwd)
    vis_f, vis_b = jnp.logical_and(fwd, colx <= sub), jnp.logical_and(bwd, colx >= sub)
    vis = jnp.logical_or(vis_f, vis_b)
    vis_t = jnp.logical_or(jnp.logical_and(fwd, sub <= colx), jnp.logical_and(bwd, sub >= colx))
    strict = jnp.logical_and(vis, colx != sub)
    eye = (sub == colx).astype(F32)
    blk16 = (sub // 16) == (colx // 16)
    blk32 = (sub // 32) == (colx // 32)
    off32 = jnp.logical_and(blk32, jnp.logical_not(blk16))
    fwd_row = lax.broadcasted_iota(jnp.int32, (1, 2 * CHUNK), 1) < CHUNK
    zeros_x = jnp.zeros((CHUNK, 2 * LANES), BF16)

    def bd(y):
        return jnp.concatenate([jnp.where(fwd, y, 0.0).astype(BF16), jnp.where(fwd, 0.0, y).astype(BF16)], axis=0)

    def bd_x(xf, xb):
        return jnp.concatenate([jnp.concatenate([xf, zeros_x], axis=1),
                                jnp.concatenate([zeros_x, xb], axis=1)], axis=0)

    def stage_wave(gi, wave_heads):
        units = []
        for b in wave_heads:
            for j in range(group):
                c = gi * group + j
                r0 = c * CHUNK
                if not isinstance(r0, int):
                    r0 = pl.multiple_of(r0, CHUNK)
                rows = pl.ds(r0, CHUNK)
                qc, kc, vc = qs[rows, head_cols[b]], ks[rows, head_cols[b]], vs[rows, head_cols[b]]
                kb = kc.astype(BF16)
                kb2 = jnp.concatenate([kb, kb], axis=0)
                kk = _dot_nt(kb, kb2)
                qk = _dot_nt(qc.astype(BF16), kb2)
                gall = gcol_s[c]
                hd = hblk * hb + b
                col = lambda g: jnp.sum(jnp.where(lane_bg == g * heads + hd, gall, 0.0), axis=1, keepdims=True)
                beta = (col(0), col(1))
                g_col = jnp.where(fwd, col(2), col(3))
                g_row = grow_s[c, pl.ds(hd, 1), :]
                g_rows = jnp.broadcast_to(g_row, (CHUNK, 2 * CHUNK))
                gc = (jnp.sum(jnp.where(vis_f, g_rows, 0.0), axis=1, keepdims=True),
                      jnp.sum(jnp.where(vis_b, g_rows, 0.0), axis=1, keepdims=True))
                gc_col = jnp.where(fwd, gc[0], gc[1])
                gc_row = jnp.sum(jnp.where(vis_t, g_col, 0.0), axis=0, keepdims=True)
                g_last = (jnp.sum(jnp.where(fwd_row, g_row, 0.0), axis=1, keepdims=True),
                          jnp.sum(jnp.where(fwd_row, 0.0, g_row), axis=1, keepdims=True))
                decay = jnp.where(vis, jnp.exp(jnp.where(vis, gc_col - gc_row, 0.0)), 0.0)
                e_gc = (jnp.exp(gc[0]), jnp.exp(gc[1]))
                units.append(dict(
                    b=b, c=c,
                    a=jnp.where(strict, kk * decay * jnp.where(fwd, beta[0], beta[1]), 0.0),
                    x=[jnp.concatenate([vc * beta[d], kc * (beta[d] * e_gc[d])], axis=1).astype(BF16)
                       for d in range(2)],
                    intra=jnp.where(vis, qk * decay, 0.0).astype(BF16),
                    k_dec=[(kc * jnp.exp(g_last[d] - gc[d])).astype(BF16) for d in range(2)],
                    q_dec=[qc * e_gc[d] for d in range(2)],
                    eg=[jnp.broadcast_to(jnp.exp(g_last[d]), (1, LANES)) for d in range(2)]))
        pmm = lambda x, y_bd: _dot(x.astype(BF16), y_bd)
        a_d = [jnp.where(blk16, u["a"], 0.0) for u in units]
        p2 = [pmm(x, bd(x)) for x in a_d]
        p2_bd = [bd(p) for p in p2]
        t = [eye - x for x in a_d]
        t = [x + pmm(x, p) for x, p in zip(t, p2_bd)]
        p4 = [pmm(p, pb) for p, pb in zip(p2, p2_bd)]
        p4_bd = [bd(p) for p in p4]
        t = [x + pmm(x, p) for x, p in zip(t, p4_bd)]
        p8 = [pmm(p, pb) for p, pb in zip(p4, p4_bd)]
        t = [x + pmm(x, bd(p)) for x, p in zip(t, p8)]
        mm = [pmm(x, bd(jnp.where(off32, u["a"], 0.0))) for x, u in zip(t, units)]
        t = [x - pmm(y, bd(x)) for x, y in zip(t, mm)]
        mm = [pmm(x, bd(jnp.where(blk32, 0.0, u["a"]))) for x, u in zip(t, units)]
        t = [x - pmm(y, bd(x)) for x, y in zip(t, mm)]
        xb = [pmm(x, bd_x(*u["x"])).astype(BF16) for x, u in zip(t, units)]
        xd = [[x[:, :2 * LANES], x[:, 2 * LANES:]] for x in xb]
        iw = [_dot(u["intra"], bd_x(*x)) for u, x in zip(units, xd)]
        kt = [[_dot_tn(u["k_dec"][d], x[d]) for d in range(2)] for u, x in zip(units, xd)]
        for u, iwu, ktu in zip(units, iw, kt):
            b, c = u["b"], u["c"]
            for d in range(2):
                iwd = iwu[:, 2 * LANES * d:2 * LANES * (d + 1)]
                oc_s[b, d, c] = iwd[:, :LANES]
                qw_s[b, d, c, 0:CHUNK, :] = (u["q_dec"][d] - iwd[:, LANES:]).astype(BF16)
                qw_s[b, d, c, CHUNK:, :] = ktu[d][:, LANES:].astype(BF16)
                n_s[b, d, c] = ktu[d][:, :LANES]
                eg_s[b, d, c] = u["eg"][d]

    def stage_group(gi, carry):
        for h0 in range(0, hb, DN_WAVE_HEADS):
            stage_wave(gi, range(h0, min(h0 + DN_WAVE_HEADS, hb)))
        return carry

    if nc == group:
        stage_group(0, 0)
    else:
        lax.fori_loop(0, nc // group, stage_group, 0)

    chains = [(b, d) for b in range(hb) for d in range(2)]
    if has_s0:
        s_init = tuple(s0_ref[d, b] for b, d in chains)
    else:
        s_init = tuple(jnp.zeros((LANES, LANES), F32) for _ in chains)

    def scan_step(t, carry):
        res = []
        for (b, d), s in zip(chains, carry):
            c = t if d == 0 else nc - 1 - t
            res.append(_dot(qw_s[b, d, c], s.astype(BF16)))
        out = []
        for (b, d), s, r in zip(chains, carry, res):
            c = t if d == 0 else nc - 1 - t
            r0 = c * CHUNK
            if not isinstance(r0, int):
                r0 = pl.multiple_of(r0, CHUNK)
            acc = of_s if d == 0 else ob_s
            acc[pl.ds(r0, CHUNK), head_cols[b]] = r[:CHUNK] + oc_s[b, d, c]
            out.append(s * eg_s[b, d, c] - r[CHUNK:] + n_s[b, d, c])
        return tuple(out)

    if nc == group:
        s_fin = s_init
        for t in range(nc):
            s_fin = scan_step(t, s_fin)
    else:
        s_fin = lax.fori_loop(0, nc, scan_step, s_init)

    o = of_s[...] + ob_s[...]
    z = _silu(z_ref[...])
    og = og_ref[...]
    for sl in head_cols:
        oh = o[:, sl]
        y = oh * lax.rsqrt(jnp.mean(oh * oh, axis=-1, keepdims=True) + EPS) * og
        o_ref[:, sl] = (y * z[:, sl]).astype(o_ref.dtype)
    if emit_state:
        if n_prev:
            st_ref[0:n_prev] = prev_ref[...]
        for (b, d), s in zip(chains, s_fin):
            st_ref[n_prev, d, b] = s


def _deltanet(p1, bg, conv_qkv, a_log, dt_bias, onorm_g, s0, cfg, *, latent, prev_states=None):
    hh, hb = cfg.heads, (cfg.dn_heads_lat if latent else cfg.dn_heads)
    tok, nbg = bg.shape
    n = cfg.dec_seq if latent else cfg.seq
    nseq = cfg.dec_batch if latent else cfg.batch
    blk0 = cfg.tok_ctx // n if latent else 0
    per = cfg.grid_w if latent else cfg.seq
    nc = n // CHUNK
    group = min(4, nc)
    width = hb * LANES
    nhb = hh // hb
    assert nbg == 4 * hh
    bgc = bg.reshape(tok // CHUNK, CHUNK, nbg)
    bgt = jnp.tile(bgc.transpose(0, 2, 1), (1, 1, 2))
    a_row = jnp.concatenate([jnp.zeros((2 * hh,), F32), a_log.reshape(-1)]).reshape(1, nbg)
    dtb_row = jnp.concatenate([jnp.zeros((2 * hh,), F32), dt_bias.reshape(-1)]).reshape(1, nbg)

    col = lambda g: pl.BlockSpec((n, width), lambda s, h: (blk0 + s, g * nhb + h))
    cw = lambda g: pl.BlockSpec((3, width), lambda s, h: (0, g * nhb + h))
    whole = lambda shape: pl.BlockSpec(shape, lambda s, h: (0,) * len(shape))
    in_specs = [col(0), col(1), col(2), col(3), cw(0), cw(1), cw(2),
                pl.BlockSpec((nc, CHUNK, nbg), lambda s, h: (blk0 + s, 0, 0)),
                pl.BlockSpec((nc, nbg, 2 * CHUNK), lambda s, h: (blk0 + s, 0, 0)),
                whole((1, nbg)), whole((1, nbg)), whole((nbg, 1)), whole((nbg, 1)), whole((1, LANES))]
    args = [p1, p1, p1, p1, conv_qkv, conv_qkv, conv_qkv, bgc, bgt, a_row, dtb_row,
            a_row.reshape(nbg, 1), dtb_row.reshape(nbg, 1), onorm_g.reshape(1, LANES)]
    st_spec = pl.BlockSpec((None, 2, hb, LANES, LANES), lambda s, h: (s, 0, h, 0, 0))
    if latent:
        in_specs.append(st_spec)
        args.append(s0)
    n_prev = 0 if prev_states is None else prev_states.shape[1]
    layers_spec = lambda nl: pl.BlockSpec((None, nl, 2, hb, LANES, LANES), lambda s, h: (s, 0, 0, h, 0, 0))
    if n_prev:
        in_specs.append(layers_spec(n_prev))
        args.append(prev_states)
    out_specs = [pl.BlockSpec((n, width), lambda s, h: (s, h))]
    out_shape = [jax.ShapeDtypeStruct((nseq * n, hh * LANES), BF16)]
    if not latent:
        out_specs.append(layers_spec(n_prev + 1))
        out_shape.append(jax.ShapeDtypeStruct((nseq, n_prev + 1, 2, hh, LANES, LANES), F32))
    scratch = [pltpu.VMEM((n, width), F32)] * 5 + [
        pltpu.VMEM((hb, 2, nc, CHUNK + LANES, LANES), BF16), pltpu.VMEM((hb, 2, nc, LANES, LANES), F32),
        pltpu.VMEM((hb, 2, nc, CHUNK, LANES), F32), pltpu.VMEM((hb, 2, nc, 1, LANES), F32),
        pltpu.VMEM((nc, CHUNK, nbg), F32), pltpu.VMEM((nc, hh, 2 * CHUNK), F32)]
    return pl.pallas_call(
        functools.partial(_dn_body, n=n, per=per, heads=hh, hb=hb, has_s0=latent, emit_state=not latent,
                          n_prev=n_prev, group=group),
        grid=(nseq, nhb),
        in_specs=in_specs,
        out_specs=out_specs,
        out_shape=out_shape,
        scratch_shapes=scratch,
        compiler_params=_params("arbitrary", "arbitrary"),
        name="deltanet_lat" if latent else "deltanet_ctx",
    )(*args)


def _forward(cfg, x_prompt, x_sample, state_dn, c, c_ctx, norm1_g, norm2_g, w_ada, b_ada, w_in,
             conv_qkv, a_log, dt_bias, onorm_g, conv_sc, w_pa, w_pb, w_o, w_up, conv_ff,
             w_down, final_g):
    d, hh = cfg.d, cfg.heads
    w1_n = 4 * cfg.w_hd
    nbg = 4 * hh
    xs = (x_prompt.reshape(cfg.tok_ctx, d), x_sample.reshape(-1, d))
    cvec = jnp.zeros((MOD_ROWS, d), F32).at[0].set(c_ctx).at[1:1 + cfg.dec_batch].set(c)
    mod = _ada(cvec, w_ada, b_ada)
    w_in_t = jnp.swapaxes(w_in, 1, 2)
    w_pa_b, w_pb_b, w_o_b = w_pa.astype(BF16), w_pb.astype(BF16), w_o.astype(BF16)
    w_down_b = _cast_pad_rows(w_down, cfg.ff_pad, CAST_PAD_ROWS)
    states = None
    n2 = 3 * cfg.w_sc + 2 * d
    ff_tn, ff_sub = cfg.ff_tn, FFN_SUBTILES
    ff_w = ff_tn // ff_sub
    assert cfg.tm == cfg.dec_seq and cfg.tm % cfg.seq == 0 and cfg.d_ff % ff_w == 0 and cfg.ff_pad % ff_tn == 0
    ff_weights = [WCols(w_up, 0, half * cfg.d_ff + u * ff_w, 0, ff_sub)
                  for u in range(ff_sub) for half in range(2)]
    for l in range(cfg.depth):
        mod3 = mod[l].reshape(MOD_ROWS * 6, 1, d)
        h = _norm_mod(xs, norm1_g[l], mod3, 0, 1, cfg)
        p1 = _proj("proj_qkvz", h, w_in_t, l, 0, 0, w1_n, cfg)
        bg = _bg_proj(h, w_in_t, l, w1_n, nbg, cfg)
        p2 = _proj("proj_sc_gates", h, w_in_t, l, w1_n, nbg, n2, cfg)
        o_ctx, states = _deltanet(p1, bg, conv_qkv[l], a_log[l], dt_bias[l], onorm_g[l], None, cfg,
                                  latent=False, prev_states=states)
        o_lat, = _deltanet(p1, bg, conv_qkv[l], a_log[l], dt_bias[l], onorm_g[l], state_dn[:, l], cfg, latent=True)
        y_sc = _sc_mixer(p2, conv_sc[l], cfg)
        merged = _merge((o_ctx, o_lat), y_sc, w_pa_b, w_pb_b, l, p2, 3 * cfg.w_sc, cfg)
        xs = (_resid(merged, w_o_b, l, xs, mod3, 2, d, cfg, "out_proj"),)
        h = _norm_mod(xs, norm2_g[l], mod3, 3, 4, cfg)
        f = _wres_matmul(
            "ffn_up", [h], [wc._replace(layer=l) for wc in ff_weights], 2,
            [(conv_ff, (None, 3, ff_tn), lambda j, i, l=l: (l, 0, j))],
            functools.partial(_ffn_up_epilogue, cfg=cfg), cfg.ff_pad, BF16, cfg.tm, ff_w, ff_tn, cfg.kc)
        xs = (_resid(f, w_down_b, l, xs, mod3, 5, cfg.ff_tk, cfg, "ffn_down"),)
    x, = xs
    y_prompt = _final_norm(x, final_g, 0, cfg.tok_ctx, cfg).reshape(x_prompt.shape)
    y_sample = _final_norm(x, final_g, cfg.tok_ctx, cfg.tok - cfg.tok_ctx, cfg).reshape(x_sample.shape)
    return y_prompt, y_sample, states


def kernel(x_prompt, x_sample, state_dn, c, c_ctx, norm1_g, norm2_g, w_ada, b_ada, w_in, conv_qkv, a_log, dt_bias, onorm_g, conv_sc, w_pa, w_pb, w_o, w_up, conv_ff, w_down, final_g):
    return _forward(CFG, x_prompt, x_sample, state_dn, c, c_ctx, norm1_g, norm2_g, w_ada, b_ada, w_in,
                    conv_qkv, a_log, dt_bias, onorm_g, conv_sc, w_pa, w_pb, w_o, w_up, conv_ff,
                    w_down, final_g)
```

```python
import functools
from typing import NamedTuple

import jax
import jax.numpy as jnp
from jax import lax
from jax.experimental import pallas as pl
from jax.experimental.pallas import tpu as pltpu

F32 = jnp.float32
BF16 = jnp.bfloat16
EPS = 1e-6
CHUNK = 64
LANES = 128
MOD_ROWS = 8
V7X_VMEM_BYTES = 64 * 1024 * 1024
VMEM_LIMIT_BYTES = V7X_VMEM_BYTES - 6 * 1024 * 1024


class Cfg(NamedTuple):
    d: int
    batch: int
    seq: int
    depth: int
    dec_batch: int
    dec_seq: int
    grid_w: int
    heads: int
    w_sc: int
    d_ff: int
    tm: int
    tn: int
    ff_tn: int
    ff_tk: int
    dn_heads: int
    kc: int
    dn_heads_lat: int = 4

    @property
    def w_hd(self):
        return self.heads * LANES

    @property
    def tok_ctx(self):
        return self.batch * self.seq

    @property
    def tok(self):
        return self.batch * self.seq + self.dec_batch * self.dec_seq

    @property
    def ff_pad(self):
        return -(-self.d_ff // self.ff_tk) * self.ff_tk


CFG = Cfg(d=4096, batch=32, seq=256, depth=2, dec_batch=2, dec_seq=1024, grid_w=64, heads=16,
          w_sc=2048, d_ff=11008, tm=1024, tn=1024, ff_tn=1024, ff_tk=2816, dn_heads=8, kc=512)
FFN_SUBTILES = 4
CAST_PAD_ROWS = 256
DN_WAVE_HEADS = 4


def _params(*sem):
    return pltpu.CompilerParams(dimension_semantics=sem, vmem_limit_bytes=VMEM_LIMIT_BYTES)


def _mod_row(i, cfg, tm):
    nct = cfg.tok_ctx // tm
    return jnp.where(i < nct, 0, 1 + (i - nct) // (cfg.dec_seq // tm))


def _silu(x):
    return x * jax.nn.sigmoid(x)


def _softplus(x):
    return jnp.maximum(x, 0.0) + jnp.log1p(jnp.exp(-jnp.abs(x)))


def _dot(a, b):
    return jnp.dot(a, b, preferred_element_type=F32)


def _dotb(a, b):
    return _dot(a.astype(BF16), b.astype(BF16))


def _dot_nt(a, b):
    return lax.dot_general(a, b, (((1,), (1,)), ((), ())), preferred_element_type=F32)


def _dot_tn(a, b):
    return lax.dot_general(a, b, (((0,), (0,)), ((), ())), preferred_element_type=F32)


def _ada_body(c_ref, w_ref, b_ref, o_ref):
    s = _silu(c_ref[...]).astype(BF16)
    o_ref[...] = _dot(s, w_ref[...].astype(BF16)) + b_ref[...]


def _ada(cvec, w_ada, b_ada):
    depth, d, n = w_ada.shape
    tn = 512
    return pl.pallas_call(
        _ada_body,
        grid=(depth, n // tn),
        in_specs=[pl.BlockSpec((MOD_ROWS, d), lambda l, j: (0, 0)),
                  pl.BlockSpec((None, d, tn), lambda l, j: (l, 0, j)),
                  pl.BlockSpec((None, 1, tn), lambda l, j: (l, 0, j))],
        out_specs=pl.BlockSpec((None, MOD_ROWS, tn), lambda l, j: (l, 0, j)),
        out_shape=jax.ShapeDtypeStruct((depth, MOD_ROWS, n), F32),
        compiler_params=_params("arbitrary", "arbitrary"),
        name="ada",
    )(cvec, w_ada, b_ada.reshape(depth, 1, n))


def _row_sources(srcs, tm, cols, index_fn):
    if len(srcs) == 1:
        return [pl.BlockSpec((tm, cols), index_fn)]
    na = srcs[0].shape[0] // tm

    def first(*g):
        i, j = index_fn(*g)
        return jnp.minimum(i, na - 1), j

    def second(*g):
        i, j = index_fn(*g)
        return jnp.maximum(i - na, 0), j

    return [pl.BlockSpec((tm, cols), first), pl.BlockSpec((tm, cols), second)]


def _pick_rows(refs, i, na):
    if len(refs) == 1:
        return refs[0][...]
    return jnp.where(i < na, refs[0][...], refs[1][...])


def _norm_mod_body(*refs, n_src, na):
    xs, (g_ref, sh_ref, sc_ref, o_ref) = refs[:n_src], refs[n_src:]
    x = _pick_rows(xs, pl.program_id(0), na)
    y = x * lax.rsqrt(jnp.mean(x * x, axis=-1, keepdims=True) + EPS) * g_ref[...]
    o_ref[...] = (y * (1.0 + sc_ref[...]) + sh_ref[...]).astype(o_ref.dtype)


def _norm_mod(xs, g, mod3, sh_idx, sc_idx, cfg):
    tok, d = sum(x.shape[0] for x in xs), xs[0].shape[1]
    tm = cfg.tm // 2
    spec_mod = lambda idx: pl.BlockSpec((None, 1, d), lambda i: (_mod_row(i, cfg, tm) * 6 + idx, 0, 0))
    return pl.pallas_call(
        functools.partial(_norm_mod_body, n_src=len(xs), na=xs[0].shape[0] // tm),
        grid=(tok // tm,),
        in_specs=_row_sources(xs, tm, d, lambda i: (i, 0)) + [
            pl.BlockSpec((1, d), lambda i: (0, 0)), spec_mod(sh_idx), spec_mod(sc_idx)],
        out_specs=pl.BlockSpec((tm, d), lambda i: (i, 0)),
        out_shape=jax.ShapeDtypeStruct((tok, d), BF16),
        compiler_params=_params("arbitrary"),
        name="norm_mod",
    )(*xs, g.reshape(1, d), mod3, mod3)


def _final_norm_body(x_ref, g_ref, o_ref):
    x = x_ref[...]
    o_ref[...] = x * lax.rsqrt(jnp.mean(x * x, axis=-1, keepdims=True) + EPS) * g_ref[...]


def _final_norm(x, g, row0, rows, cfg):
    d = x.shape[1]
    tm = cfg.tm // 2
    off = row0 // tm
    return pl.pallas_call(
        _final_norm_body,
        grid=(rows // tm,),
        in_specs=[pl.BlockSpec((tm, d), lambda i: (i + off, 0)),
                  pl.BlockSpec((1, d), lambda i: (0, 0))],
        out_specs=pl.BlockSpec((tm, d), lambda i: (i, 0)),
        out_shape=jax.ShapeDtypeStruct((rows, d), F32),
        compiler_params=_params("arbitrary"),
        name="final_norm",
    )(x, g.reshape(1, d))


class WCols(NamedTuple):
    w: jax.Array
    layer: int
    col0: int
    act: int
    stride: int


def _wres_body(*refs, n_act, weights, fuse, n_extra, epilogue, kc, n_cast):
    acts = refs[:n_act]
    w_refs = refs[n_act:n_act + len(weights)]
    pos = n_act + len(weights)
    extras = refs[pos:pos + n_extra]
    o_ref = refs[pos + n_extra]
    scrs = refs[pos + n_extra + 1:]
    s = pl.program_id(1)

    @pl.when(s < n_cast)
    def _():
        rows = pl.ds(pl.multiple_of(s * kc, kc), kc)
        for k, w_ref in enumerate(w_refs):
            tn = w_ref.shape[1]
            scrs[k // fuse][rows, (k % fuse) * tn:(k % fuse + 1) * tn] = w_ref[...].astype(BF16)

    @pl.when(s >= n_cast)
    def _():
        prods = [_dot(acts[weights[g * fuse].act][...], scr[...]) for g, scr in enumerate(scrs)]
        epilogue(prods, extras, o_ref, s - n_cast)


def _wres_matmul(name, acts, weights, fuse, extras, epilogue, n_out, out_dtype, tm, tn, tn_out, kc):
    tok = acts[0].shape[0]
    kdim = acts[0].shape[1]
    n_cast = kdim // kc
    assert kdim % kc == 0 and all(a.shape[1] == kdim for a in acts)
    row = lambda s: jnp.maximum(s - n_cast, 0)
    chunk = lambda s: jnp.minimum(s, n_cast - 1)
    in_specs = [pl.BlockSpec((tm, kdim), lambda j, s: (row(s), 0)) for a in acts]
    args = list(acts)
    for wc in weights:
        assert wc.col0 % tn == 0 and wc.w.shape[1] == kdim
        c0, layer, stride = wc.col0 // tn, wc.layer, wc.stride
        last = pl.cdiv(wc.w.shape[2], tn) - 1
        in_specs.append(pl.BlockSpec(
            (None, kc, tn),
            lambda j, s, c0=c0, layer=layer, stride=stride, last=last:
            (layer, chunk(s), jnp.minimum(c0 + j * stride, last))))
        args.append(wc.w)
    assert len(weights) % fuse == 0
    scratch = [pltpu.VMEM((kdim, fuse * tn), BF16)] * (len(weights) // fuse)
    for arr, block, index_map in extras:
        in_specs.append(pl.BlockSpec(block, lambda j, s, index_map=index_map: index_map(j, row(s))))
        args.append(arr)
    return pl.pallas_call(
        functools.partial(_wres_body, n_act=len(acts), weights=tuple(wc._replace(w=None) for wc in weights),
                          fuse=fuse, n_extra=len(extras), epilogue=epilogue, kc=kc, n_cast=n_cast),
        grid=(n_out // tn_out, n_cast + tok // tm),
        in_specs=in_specs,
        out_specs=pl.BlockSpec((tm, tn_out), lambda j, s: (row(s), j)),
        out_shape=jax.ShapeDtypeStruct((tok, n_out), out_dtype),
        scratch_shapes=scratch,
        compiler_params=_params("arbitrary", "arbitrary"),
        name=name,
    )(*args)


def _proj_body(*refs, kc, n_cast, shift):
    if shift:
        a_ref, w_ref, nxt_ref, o_ref, scr = refs
    else:
        a_ref, w_ref, o_ref, scr = refs
    sweep, s = pl.program_id(0), pl.program_id(1)
    slot = sweep % 2

    def cast_chunk():
        cols = pl.ds(pl.multiple_of(jnp.minimum(s, n_cast - 1) * kc, kc), kc)
        w = w_ref[...]
        if shift:
            w = jnp.concatenate([w[shift:], nxt_ref[...]], axis=0)
        scr[slot, :, cols] = w.astype(BF16)

    @pl.when(sweep == 0)
    def _():
        cast_chunk()

    @pl.when(sweep > 0)
    def _():
        cast_chunk()
        o_ref[...] = _dot_nt(a_ref[...], scr[1 - slot]).astype(o_ref.dtype)


def _proj(name, a, w_t, layer, row0, shift, n_out, cfg):
    tok, kdim = a.shape
    tm, tn, kc = cfg.tm, cfg.tn, cfg.kc
    n_cast, ni, nj = kdim // kc, tok // tm, n_out // tn
    assert kdim % kc == 0 and n_cast <= ni and row0 % tn == 0 and n_out % tn == 0
    c0 = row0 // tn
    tile = lambda t: jnp.minimum(t, nj - 1)
    chunk = lambda t, s: jnp.where(t == nj, n_cast - 1, jnp.minimum(s, n_cast - 1))
    row = lambda t, s: jnp.where(t == 0, 0, s)
    in_specs = [pl.BlockSpec((tm, kdim), lambda t, s: (row(t, s), 0)),
                pl.BlockSpec((None, tn, kc), lambda t, s: (layer, c0 + tile(t), chunk(t, s)))]
    args = [a, w_t]
    if shift:
        assert tn % shift == 0 and shift % 8 == 0
        per = tn // shift
        in_specs.append(pl.BlockSpec((None, shift, kc),
                                     lambda t, s: (layer, (c0 + tile(t) + 1) * per, chunk(t, s))))
        args.append(w_t)
    return pl.pallas_call(
        functools.partial(_proj_body, kc=kc, n_cast=n_cast, shift=shift),
        grid=(nj + 1, ni),
        in_specs=in_specs,
        out_specs=pl.BlockSpec((tm, tn), lambda t, s: (row(t, s), jnp.maximum(t - 1, 0))),
        out_shape=jax.ShapeDtypeStruct((tok, n_out), F32),
        scratch_shapes=[pltpu.VMEM((2, tn, kdim), BF16)],
        compiler_params=_params("arbitrary", "arbitrary"),
        name=name,
    )(*args)


def _bg_body(a_ref, w_ref, o_ref):
    o_ref[...] = _dot_nt(a_ref[...], w_ref[...].astype(BF16))


def _bg_proj(h, w_t, layer, row0, nbg, cfg):
    tok, k = h.shape
    tm = cfg.tm
    assert row0 % nbg == 0
    return pl.pallas_call(
        _bg_body,
        grid=(tok // tm,),
        in_specs=[pl.BlockSpec((tm, k), lambda i: (i, 0)),
                  pl.BlockSpec((None, nbg, k), lambda i: (layer, row0 // nbg, 0))],
        out_specs=pl.BlockSpec((tm, nbg), lambda i: (i, 0)),
        out_shape=jax.ShapeDtypeStruct((tok, nbg), F32),
        compiler_params=_params("arbitrary"),
        name="proj_bg",
    )(h, w_t)


def _merge_body(*refs, n_src, na):
    os, (y_ref, wa_ref, wb_ref, ga_ref, gb_ref, out_ref) = refs[:n_src], refs[n_src:]
    pa = _dot(_pick_rows(os, pl.program_id(0), na), wa_ref[...])
    pb = _dot(y_ref[...], wb_ref[...])
    out_ref[...] = (jax.nn.sigmoid(ga_ref[...]) * pa + jax.nn.sigmoid(gb_ref[...]) * pb).astype(out_ref.dtype)


def _merge(os, y, w_pa, w_pb, layer, p2, gate_col0, cfg):
    tok = y.shape[0]
    d = w_pa.shape[2]
    tm, tn = cfg.tm, min(cfg.tn // 2, d)
    ga0 = gate_col0 // tn
    gb0 = (gate_col0 + d) // tn
    return pl.pallas_call(
        functools.partial(_merge_body, n_src=len(os), na=os[0].shape[0] // tm),
        grid=(tok // tm, d // tn),
        in_specs=_row_sources(os, tm, os[0].shape[1], lambda i, j: (i, 0)) + [
            pl.BlockSpec((tm, y.shape[1]), lambda i, j: (i, 0)),
            pl.BlockSpec((None, w_pa.shape[1], tn), lambda i, j: (layer, 0, j)),
            pl.BlockSpec((None, w_pb.shape[1], tn), lambda i, j: (layer, 0, j)),
            pl.BlockSpec((tm, tn), lambda i, j: (i, ga0 + j)),
            pl.BlockSpec((tm, tn), lambda i, j: (i, gb0 + j))],
        out_specs=pl.BlockSpec((tm, tn), lambda i, j: (i, j)),
        out_shape=jax.ShapeDtypeStruct((tok, d), BF16),
        compiler_params=_params("arbitrary", "arbitrary"),
        name="merge",
    )(*os, y, w_pa, w_pb, p2, p2)


def _ffn_up_epilogue(prods, extras, o_ref, i, *, cfg):
    cw_ref, = extras
    tm, tn_out = o_ref.shape
    nsub = len(prods)
    tn = tn_out // nsub
    j = pl.program_id(0)
    is_ctx = i < cfg.tok_ctx // tm
    gw = cfg.grid_w
    lane = lax.broadcasted_iota(jnp.int32, (tm, tn), 1)
    t = lax.broadcasted_iota(jnp.int32, (tm, tn), 0) & (cfg.seq - 1)
    zeros = jnp.zeros((gw, tn), F32)
    for u in range(nsub):
        g, v = prods[u][:, :tn], prods[u][:, tn:]
        cols = slice(u * tn, (u + 1) * tn)
        w = cw_ref[:, cols]
        gm = jnp.where(is_ctx, jnp.where(t == 0, 0.0, pltpu.roll(g, 1, 0)),
                       jnp.concatenate([zeros, g[:tm - gw]], axis=0))
        gp = jnp.where(is_ctx, jnp.where(t == cfg.seq - 1, 0.0, pltpu.roll(g, tm - 1, 0)),
                       jnp.concatenate([g[gw:], zeros], axis=0))
        y = gm * w[0:1] + g * w[1:2] + gp * w[2:3]
        valid = lane + (j * tn_out + u * tn) < cfg.d_ff
        o_ref[:, cols] = jnp.where(valid, _silu(y) * v, 0.0).astype(o_ref.dtype)


def _resid_body(*refs, nk, n_src, na):
    a_ref, w_ref = refs[:2]
    xs, (gt_ref, o_ref) = refs[2:2 + n_src], refs[2 + n_src:]
    prod = lambda: _dot(a_ref[...], w_ref[...])
    finish = lambda acc: _pick_rows(xs, pl.program_id(0), na) + gt_ref[...] * acc
    if nk == 1:
        o_ref[...] = finish(prod())
        return
    k = pl.program_id(2)

    @pl.when(k == 0)
    def _():
        o_ref[...] = prod()

    @pl.when(jnp.logical_and(k > 0, k < nk - 1))
    def _():
        o_ref[...] += prod()

    @pl.when(k == nk - 1)
    def _():
        o_ref[...] = finish(o_ref[...] + prod())


def _resid(a, w, layer, xs, mod3, gt_idx, tk, cfg, name):
    tok, kdim = a.shape
    d = w.shape[2]
    tm, tn = cfg.tm, min(cfg.tn // len(xs), d)
    nk = kdim // tk
    return pl.pallas_call(
        functools.partial(_resid_body, nk=nk, n_src=len(xs), na=xs[0].shape[0] // tm),
        grid=(tok // tm, d // tn, nk),
        in_specs=[pl.BlockSpec((tm, tk), lambda i, j, k: (i, k)),
                  pl.BlockSpec((None, tk, tn), lambda i, j, k: (layer, k, j))]
        + _row_sources(xs, tm, tn, lambda i, j, k: (i, j))
        + [pl.BlockSpec((None, 1, tn), lambda i, j, k: (_mod_row(i, cfg, tm) * 6 + gt_idx, 0, j))],
        out_specs=pl.BlockSpec((tm, tn), lambda i, j, k: (i, j)),
        out_shape=jax.ShapeDtypeStruct((tok, d), F32),
        compiler_params=_params("arbitrary", "arbitrary", "arbitrary"),
        name=name,
    )(a, w, *xs, mod3)


def _cast_pad_body(w_ref, o_ref, *, rows_in):
    tr = o_ref.shape[0]
    row = lax.broadcasted_iota(jnp.int32, o_ref.shape, 0) + pl.program_id(1) * tr
    o_ref[...] = jnp.where(row < rows_in, w_ref[...], 0.0).astype(o_ref.dtype)


def _cast_pad_rows(w, rows_out, tr):
    depth, rows_in, n = w.shape
    last = pl.cdiv(rows_in, tr) - 1
    return pl.pallas_call(
        functools.partial(_cast_pad_body, rows_in=rows_in),
        grid=(depth, rows_out // tr),
        in_specs=[pl.BlockSpec((None, tr, n), lambda l, i: (l, jnp.minimum(i, last), 0))],
        out_specs=pl.BlockSpec((None, tr, n), lambda l, i: (l, i, 0)),
        out_shape=jax.ShapeDtypeStruct((depth, rows_out, n), BF16),
        compiler_params=_params("arbitrary", "arbitrary"),
        name="cast_pad",
    )(w)


def _sc_body(xs_ref, bs_ref, cs_ref, w_ref, o_ref, *, cfg):
    tm, tn = o_ref.shape
    i = pl.program_id(0)
    per = jnp.where(i < cfg.tok_ctx // tm, cfg.seq, cfg.grid_w)
    u = cs_ref[...] * xs_ref[...]
    t = lax.broadcasted_iota(jnp.int32, (tm, tn), 0) & (per - 1)
    um = jnp.where(t == 0, 0.0, pltpu.roll(u, 1, 0))
    up = jnp.where(t == per - 1, 0.0, pltpu.roll(u, tm - 1, 0))
    w = w_ref[...]
    o_ref[...] = (bs_ref[...] * (um * w[0:1] + u * w[1:2] + up * w[2:3])).astype(o_ref.dtype)


def _sc_mixer(p2, cw, cfg):
    tok = p2.shape[0]
    ws = cfg.w_sc
    tm, tn = cfg.tm, min(512, ws)
    nb = ws // tn
    return pl.pallas_call(
        functools.partial(_sc_body, cfg=cfg),
        grid=(tok // tm, nb),
        in_specs=[pl.BlockSpec((tm, tn), lambda i, j: (i, j)),
                  pl.BlockSpec((tm, tn), lambda i, j: (i, nb + j)),
                  pl.BlockSpec((tm, tn), lambda i, j: (i, 2 * nb + j)),
                  pl.BlockSpec((3, tn), lambda i, j: (0, j))],
        out_specs=pl.BlockSpec((tm, tn), lambda i, j: (i, j)),
        out_shape=jax.ShapeDtypeStruct((tok, ws), BF16),
        compiler_params=_params("arbitrary", "arbitrary"),
        name="sc_mixer",
    )(p2, p2, p2, cw)


def _dn_body(*refs, n, per, heads, hb, has_s0, emit_state, n_prev, group):
    it = iter(refs)
    q_ref, k_ref, v_ref, z_ref = next(it), next(it), next(it), next(it)
    wq_ref, wk_ref, wv_ref = next(it), next(it), next(it)
    bgc_ref, bgt_ref, a_row_ref, dtb_row_ref, a_col_ref, dtb_col_ref, og_ref = (next(it) for _ in range(7))
    s0_ref = next(it) if has_s0 else None
    prev_ref = next(it) if n_prev else None
    o_ref = next(it)
    st_ref = next(it) if emit_state else None
    qs, ks, vs, of_s, ob_s, qw_s, n_s, oc_s, eg_s, gcol_s, grow_s = it

    nc = n // CHUNK
    width = hb * LANES
    hblk = pl.program_id(1)
    head_cols = [slice(b * LANES, (b + 1) * LANES) for b in range(hb)]

    tok = lax.broadcasted_iota(jnp.int32, (n, width), 0) & (per - 1)
    first, last = tok == 0, tok == per - 1

    def conv_silu(x_ref, w_ref):
        x, w = x_ref[...], w_ref[...]
        xm = jnp.where(first, 0.0, pltpu.roll(x, 1, 0))
        xp = jnp.where(last, 0.0, pltpu.roll(x, n - 1, 0))
        return _silu(xm * w[0:1] + x * w[1:2] + xp * w[2:3])

    q = conv_silu(q_ref, wq_ref)
    k = conv_silu(k_ref, wk_ref)
    for sl in head_cols:
        qh, kh = q[:, sl], k[:, sl]
        qs[:, sl] = qh * lax.rsqrt(jnp.sum(qh * qh, axis=-1, keepdims=True) + EPS) * (LANES ** -0.5)
        ks[:, sl] = kh * lax.rsqrt(jnp.sum(kh * kh, axis=-1, keepdims=True) + EPS)
    vs[...] = conv_silu(v_ref, wv_ref)

    nbg = bgc_ref.shape[-1]
    lane_bg = lax.broadcasted_iota(jnp.int32, (CHUNK, nbg), 1)
    is_beta = lane_bg < 2 * heads
    neg_a_row, dtb_row = -jnp.exp(a_row_ref[...]), dtb_row_ref[...]
    neg_a_col, dtb_col = -jnp.exp(a_col_ref[...]), dtb_col_ref[...]
    fwd_heads = lax.broadcasted_iota(jnp.int32, (heads, 2 * CHUNK), 1) < CHUNK

    def gates(c, carry):
        bgc = bgc_ref[c]
        gcol_s[c] = jnp.where(is_beta, jax.nn.sigmoid(bgc), neg_a_row * _softplus(bgc + dtb_row))
        gt = neg_a_col * _softplus(bgt_ref[c] + dtb_col)
        grow_s[c] = jnp.where(fwd_heads, gt[2 * heads:3 * heads], gt[3 * heads:])
        return carry

    if nc == group:
        for c in range(nc):
            gates(c, 0)
    else:
        lax.fori_loop(0, nc, gates, 0)

    sub = lax.broadcasted_iota(jnp.int32, (CHUNK, 2 * CHUNK), 0)
    lane = lax.broadcasted_iota(jnp.int32, (CHUNK, 2 * CHUNK), 1)
    fwd = lane < CHUNK
    colx = lane & (CHUNK - 1)
    bwd = jnp.logical_not(fwd)
    vis_f, vis_b = jnp.logical_and(fwd, colx <= sub), jnp.logical_and(bwd, colx >= sub)
    vis = jnp.logical_or(vis_f, vis_b)
    vis_t = jnp.logical_or(jnp.logical_and(fwd, sub <= colx), jnp.logical_and(bwd, sub >= colx))
    strict = jnp.logical_and(vis, colx != sub)
    eye = (sub == colx).astype(F32)
    blk16 = (sub // 16) == (colx // 16)
    blk32 = (sub // 32) == (colx // 32)
    off32 = jnp.logical_and(blk32, jnp.logical_not(blk16))
    fwd_row = lax.broadcasted_iota(jnp.int32, (1, 2 * CHUNK), 1) < CHUNK
    zeros_x = jnp.zeros((CHUNK, 2 * LANES), BF16)

    def bd(y):
        return jnp.concatenate([jnp.where(fwd, y, 0.0).astype(BF16), jnp.where(fwd, 0.0, y).astype(BF16)], axis=0)

    def bd_x(xf, xb):
        return jnp.concatenate([jnp.concatenate([xf, zeros_x], axis=1),
                                jnp.concatenate([zeros_x, xb], axis=1)], axis=0)

    def stage_wave(gi, wave_heads):
        units = []
        for b in wave_heads:
            for j in range(group):
                c = gi * group + j
                r0 = c * CHUNK
                if not isinstance(r0, int):
                    r0 = pl.multiple_of(r0, CHUNK)
                rows = pl.ds(r0, CHUNK)
                qc, kc, vc = qs[rows, head_cols[b]], ks[rows, head_cols[b]], vs[rows, head_cols[b]]
                kb = kc.astype(BF16)
                kb2 = jnp.concatenate([kb, kb], axis=0)
                kk = _dot_nt(kb, kb2)
                qk = _dot_nt(qc.astype(BF16), kb2)
                gall = gcol_s[c]
                hd = hblk * hb + b
                col = lambda g: jnp.sum(jnp.where(lane_bg == g * heads + hd, gall, 0.0), axis=1, keepdims=True)
                beta = (col(0), col(1))
                g_col = jnp.where(fwd, col(2), col(3))
                g_row = grow_s[c, pl.ds(hd, 1), :]
                g_rows = jnp.broadcast_to(g_row, (CHUNK, 2 * CHUNK))
                gc = (jnp.sum(jnp.where(vis_f, g_rows, 0.0), axis=1, keepdims=True),
                      jnp.sum(jnp.where(vis_b, g_rows, 0.0), axis=1, keepdims=True))
                gc_col = jnp.where(fwd, gc[0], gc[1])
                gc_row = jnp.sum(jnp.where(vis_t, g_col, 0.0), axis=0, keepdims=True)
                g_last = (jnp.sum(jnp.where(fwd_row, g_row, 0.0), axis=1, keepdims=True),
                          jnp.sum(jnp.where(fwd_row, 0.0, g_row), axis=1, keepdims=True))
                decay = jnp.where(vis, jnp.exp(jnp.where(vis, gc_col - gc_row, 0.0)), 0.0)
                e_gc = (jnp.exp(gc[0]), jnp.exp(gc[1]))
                units.append(dict(
                    b=b, c=c,
                    a=jnp.where(strict, kk * decay * jnp.where(fwd, beta[0], beta[1]), 0.0),
                    x=[jnp.concatenate([vc * beta[d], kc * (beta[d] * e_gc[d])], axis=1).astype(BF16)
                       for d in range(2)],
                    intra=jnp.where(vis, qk * decay, 0.0).astype(BF16),
                    k_dec=[(kc * jnp.exp(g_last[d] - gc[d])).astype(BF16) for d in range(2)],
                    q_dec=[qc * e_gc[d] for d in range(2)],
                    eg=[jnp.broadcast_to(jnp.exp(g_last[d]), (1, LANES)) for d in range(2)]))
        pmm = lambda x, y_bd: _dot(x.astype(BF16), y_bd)
        a_d = [jnp.where(blk16, u["a"], 0.0) for u in units]
        p2 = [pmm(x, bd(x)) for x in a_d]
        p2_bd = [bd(p) for p in p2]
        t = [eye - x for x in a_d]
        t = [x + pmm(x, p) for x, p in zip(t, p2_bd)]
        p4 = [pmm(p, pb) for p, pb in zip(p2, p2_bd)]
        p4_bd = [bd(p) for p in p4]
        t = [x + pmm(x, p) for x, p in zip(t, p4_bd)]
        p8 = [pmm(p, pb) for p, pb in zip(p4, p4_bd)]
        t = [x + pmm(x, bd(p)) for x, p in zip(t, p8)]
        mm = [pmm(x, bd(jnp.where(off32, u["a"], 0.0))) for x, u in zip(t, units)]
        t = [x - pmm(y, bd(x)) for x, y in zip(t, mm)]
        mm = [pmm(x, bd(jnp.where(blk32, 0.0, u["a"]))) for x, u in zip(t, units)]
        t = [x - pmm(y, bd(x)) for x, y in zip(t, mm)]
        xb = [pmm(x, bd_x(*u["x"])).astype(BF16) for x, u in zip(t, units)]
        xd = [[x[:, :2 * LANES], x[:, 2 * LANES:]] for x in xb]
        iw = [_dot(u["intra"], bd_x(*x)) for u, x in zip(units, xd)]
        kt = [[_dot_tn(u["k_dec"][d], x[d]) for d in range(2)] for u, x in zip(units, xd)]
        for u, iwu, ktu in zip(units, iw, kt):
            b, c = u["b"], u["c"]
            for d in range(2):
                iwd = iwu[:, 2 * LANES * d:2 * LANES * (d + 1)]
                oc_s[b, d, c] = iwd[:, :LANES]
                qw_s[b, d, c, 0:CHUNK, :] = (u["q_dec"][d] - iwd[:, LANES:]).astype(BF16)
                qw_s[b, d, c, CHUNK:, :] = ktu[d][:, LANES:].astype(BF16)
                n_s[b, d, c] = ktu[d][:, :LANES]
                eg_s[b, d, c] = u["eg"][d]

    def stage_group(gi, carry):
        for h0 in range(0, hb, DN_WAVE_HEADS):
            stage_wave(gi, range(h0, min(h0 + DN_WAVE_HEADS, hb)))
        return carry

    if nc == group:
        stage_group(0, 0)
    else:
        lax.fori_loop(0, nc // group, stage_group, 0)

    chains = [(b, d) for b in range(hb) for d in range(2)]
    if has_s0:
        s_init = tuple(s0_ref[d, b] for b, d in chains)
    else:
        s_init = tuple(jnp.zeros((LANES, LANES), F32) for _ in chains)

    def scan_step(t, carry):
        res = []
        for (b, d), s in zip(chains, carry):
            c = t if d == 0 else nc - 1 - t
            res.append(_dot(qw_s[b, d, c], s.astype(BF16)))
        out = []
        for (b, d), s, r in zip(chains, carry, res):
            c = t if d == 0 else nc - 1 - t
            r0 = c * CHUNK
            if not isinstance(r0, int):
                r0 = pl.multiple_of(r0, CHUNK)
            acc = of_s if d == 0 else ob_s
            acc[pl.ds(r0, CHUNK), head_cols[b]] = r[:CHUNK] + oc_s[b, d, c]
            out.append(s * eg_s[b, d, c] - r[CHUNK:] + n_s[b, d, c])
        return tuple(out)

    if nc == group:
        s_fin = s_init
        for t in range(nc):
            s_fin = scan_step(t, s_fin)
    else:
        s_fin = lax.fori_loop(0, nc, scan_step, s_init)

    o = of_s[...] + ob_s[...]
    z = _silu(z_ref[...])
    og = og_ref[...]
    for sl in head_cols:
        oh = o[:, sl]
        y = oh * lax.rsqrt(jnp.mean(oh * oh, axis=-1, keepdims=True) + EPS) * og
        o_ref[:, sl] = (y * z[:, sl]).astype(o_ref.dtype)
    if emit_state:
        if n_prev:
            st_ref[0:n_prev] = prev_ref[...]
        for (b, d), s in zip(chains, s_fin):
            st_ref[n_prev, d, b] = s


def _deltanet(p1, bg, conv_qkv, a_log, dt_bias, onorm_g, s0, cfg, *, latent, prev_states=None):
    hh, hb = cfg.heads, (cfg.dn_heads_lat if latent else cfg.dn_heads)
    tok, nbg = bg.shape
    n = cfg.dec_seq if latent else cfg.seq
    nseq = cfg.dec_batch if latent else cfg.batch
    blk0 = cfg.tok_ctx // n if latent else 0
    per = cfg.grid_w if latent else cfg.seq
    nc = n // CHUNK
    group = min(4, nc)
    width = hb * LANES
    nhb = hh // hb
    assert nbg == 4 * hh
    bgc = bg.reshape(tok // CHUNK, CHUNK, nbg)
    bgt = jnp.tile(bgc.transpose(0, 2, 1), (1, 1, 2))
    a_row = jnp.concatenate([jnp.zeros((2 * hh,), F32), a_log.reshape(-1)]).reshape(1, nbg)
    dtb_row = jnp.concatenate([jnp.zeros((2 * hh,), F32), dt_bias.reshape(-1)]).reshape(1, nbg)

    col = lambda g: pl.BlockSpec((n, width), lambda s, h: (blk0 + s, g * nhb + h))
    cw = lambda g: pl.BlockSpec((3, width), lambda s, h: (0, g * nhb + h))
    whole = lambda shape: pl.BlockSpec(shape, lambda s, h: (0,) * len(shape))
    in_specs = [col(0), col(1), col(2), col(3), cw(0), cw(1), cw(2),
                pl.BlockSpec((nc, CHUNK, nbg), lambda s, h: (blk0 + s, 0, 0)),
                pl.BlockSpec((nc, nbg, 2 * CHUNK), lambda s, h: (blk0 + s, 0, 0)),
                whole((1, nbg)), whole((1, nbg)), whole((nbg, 1)), whole((nbg, 1)), whole((1, LANES))]
    args = [p1, p1, p1, p1, conv_qkv, conv_qkv, conv_qkv, bgc, bgt, a_row, dtb_row,
            a_row.reshape(nbg, 1), dtb_row.reshape(nbg, 1), onorm_g.reshape(1, LANES)]
    st_spec = pl.BlockSpec((None, 2, hb, LANES, LANES), lambda s, h: (s, 0, h, 0, 0))
    if latent:
        in_specs.append(st_spec)
        args.append(s0)
    n_prev = 0 if prev_states is None else prev_states.shape[1]
    layers_spec = lambda nl: pl.BlockSpec((None, nl, 2, hb, LANES, LANES), lambda s, h: (s, 0, 0, h, 0, 0))
    if n_prev:
        in_specs.append(layers_spec(n_prev))
        args.append(prev_states)
    out_specs = [pl.BlockSpec((n, width), lambda s, h: (s, h))]
    out_shape = [jax.ShapeDtypeStruct((nseq * n, hh * LANES), BF16)]
    if not latent:
        out_specs.append(layers_spec(n_prev + 1))
        out_shape.append(jax.ShapeDtypeStruct((nseq, n_prev + 1, 2, hh, LANES, LANES), F32))
    scratch = [pltpu.VMEM((n, width), F32)] * 5 + [
        pltpu.VMEM((hb, 2, nc, CHUNK + LANES, LANES), BF16), pltpu.VMEM((hb, 2, nc, LANES, LANES), F32),
        pltpu.VMEM((hb, 2, nc, CHUNK, LANES), F32), pltpu.VMEM((hb, 2, nc, 1, LANES), F32),
        pltpu.VMEM((nc, CHUNK, nbg), F32), pltpu.VMEM((nc, hh, 2 * CHUNK), F32)]
    return pl.pallas_call(
        functools.partial(_dn_body, n=n, per=per, heads=hh, hb=hb, has_s0=latent, emit_state=not latent,
                          n_prev=n_prev, group=group),
        grid=(nseq, nhb),
        in_specs=in_specs,
        out_specs=out_specs,
        out_shape=out_shape,
        scratch_shapes=scratch,
        compiler_params=_params("arbitrary", "arbitrary"),
        name="deltanet_lat" if latent else "deltanet_ctx",
    )(*args)


def _forward(cfg, x_prompt, x_sample, state_dn, c, c_ctx, norm1_g, norm2_g, w_ada, b_ada, w_in,
             conv_qkv, a_log, dt_bias, onorm_g, conv_sc, w_pa, w_pb, w_o, w_up, conv_ff,
             w_down, final_g):
    d, hh = cfg.d, cfg.heads
    w1_n = 4 * cfg.w_hd
    nbg = 4 * hh
    xs = (x_prompt.reshape(cfg.tok_ctx, d), x_sample.reshape(-1, d))
    cvec = jnp.zeros((MOD_ROWS, d), F32).at[0].set(c_ctx).at[1:1 + cfg.dec_batch].set(c)
    mod = _ada(cvec, w_ada, b_ada)
    w_in_t = jnp.swapaxes(w_in, 1, 2)
    w_pa_b, w_pb_b, w_o_b = w_pa.astype(BF16), w_pb.astype(BF16), w_o.astype(BF16)
    w_down_b = _cast_pad_rows(w_down, cfg.ff_pad, CAST_PAD_ROWS)
    states = None
    n2 = 3 * cfg.w_sc + 2 * d
    ff_tn, ff_sub = cfg.ff_tn, FFN_SUBTILES
    ff_w = ff_tn // ff_sub
    assert cfg.tm == cfg.dec_seq and cfg.tm % cfg.seq == 0 and cfg.d_ff % ff_w == 0 and cfg.ff_pad % ff_tn == 0
    ff_weights = [WCols(w_up, 0, half * cfg.d_ff + u * ff_w, 0, ff_sub)
                  for u in range(ff_sub) for half in range(2)]
    for l in range(cfg.depth):
        mod3 = mod[l].reshape(MOD_ROWS * 6, 1, d)
        h = _norm_mod(xs, norm1_g[l], mod3, 0, 1, cfg)
        p1 = _proj("proj_qkvz", h, w_in_t, l, 0, 0, w1_n, cfg)
        bg = _bg_proj(h, w_in_t, l, w1_n, nbg, cfg)
        p2 = _proj("proj_sc_gates", h, w_in_t, l, w1_n, nbg, n2, cfg)
        o_ctx, states = _deltanet(p1, bg, conv_qkv[l], a_log[l], dt_bias[l], onorm_g[l], None, cfg,
                                  latent=False, prev_states=states)
        o_lat, = _deltanet(p1, bg, conv_qkv[l], a_log[l], dt_bias[l], onorm_g[l], state_dn[:, l], cfg, latent=True)
        y_sc = _sc_mixer(p2, conv_sc[l], cfg)
        merged = _merge((o_ctx, o_lat), y_sc, w_pa_b, w_pb_b, l, p2, 3 * cfg.w_sc, cfg)
        xs = (_resid(merged, w_o_b, l, xs, mod3, 2, d, cfg, "out_proj"),)
        h = _norm_mod(xs, norm2_g[l], mod3, 3, 4, cfg)
        f = _wres_matmul(
            "ffn_up", [h], [wc._replace(layer=l) for wc in ff_weights], 2,
            [(conv_ff, (None, 3, ff_tn), lambda j, i, l=l: (l, 0, j))],
            functools.partial(_ffn_up_epilogue, cfg=cfg), cfg.ff_pad, BF16, cfg.tm, ff_w, ff_tn, cfg.kc)
        xs = (_resid(f, w_down_b, l, xs, mod3, 5, cfg.ff_tk, cfg, "ffn_down"),)
    x, = xs
    y_prompt = _final_norm(x, final_g, 0, cfg.tok_ctx, cfg).reshape(x_prompt.shape)
    y_sample = _final_norm(x, final_g, cfg.tok_ctx, cfg.tok - cfg.tok_ctx, cfg).reshape(x_sample.shape)
    return y_prompt, y_sample, states


def kernel(x_prompt, x_sample, state_dn, c, c_ctx, norm1_g, norm2_g, w_ada, b_ada, w_in, conv_qkv, a_log, dt_bias, onorm_g, conv_sc, w_pa, w_pb, w_o, w_up, conv_ff, w_down, final_g):
    return _forward(CFG, x_prompt, x_sample, state_dn, c, c_ctx, norm1_g, norm2_g, w_ada, b_ada, w_in,
                    conv_qkv, a_log, dt_bias, onorm_g, conv_sc, w_pa, w_pb, w_o, w_up, conv_ff,
                    w_down, final_g)
```

```python
import functools
from typing import NamedTuple

import jax
import jax.numpy as jnp
from jax import lax
from jax.experimental import pallas as pl
from jax.experimental.pallas import tpu as pltpu

F32 = jnp.float32
BF16 = jnp.bfloat16
EPS = 1e-6
CHUNK = 64
LANES = 128
MOD_ROWS = 8
V7X_VMEM_BYTES = 64 * 1024 * 1024
VMEM_LIMIT_BYTES = V7X_VMEM_BYTES - 6 * 1024 * 1024


class Cfg(NamedTuple):
    d: int
    batch: int
    seq: int
    depth: int
    dec_batch: int
    dec_seq: int
    grid_w: int
    heads: int
    w_sc: int
    d_ff: int
    tm: int
    tn: int
    ff_tn: int
    ff_tk: int
    dn_heads: int
    kc: int
    dn_heads_lat: int = 4

    @property
    def w_hd(self):
        return self.heads * LANES

    @property
    def tok_ctx(self):
        return self.batch * self.seq

    @property
    def tok(self):
        return self.batch * self.seq + self.dec_batch * self.dec_seq

    @property
    def ff_pad(self):
        return -(-self.d_ff // self.ff_tk) * self.ff_tk


CFG = Cfg(d=4096, batch=32, seq=256, depth=2, dec_batch=2, dec_seq=1024, grid_w=64, heads=16,
          w_sc=2048, d_ff=11008, tm=1024, tn=1024, ff_tn=1024, ff_tk=2816, dn_heads=8, kc=512)
FFN_SUBTILES = 4
CAST_PAD_ROWS = 256
DN_WAVE_HEADS = 4


def _params(*sem):
    return pltpu.CompilerParams(dimension_semantics=sem, vmem_limit_bytes=VMEM_LIMIT_BYTES)


def _mod_row(i, cfg, tm):
    nct = cfg.tok_ctx // tm
    return jnp.where(i < nct, 0, 1 + (i - nct) // (cfg.dec_seq // tm))


def _silu(x):
    return x * jax.nn.sigmoid(x)


def _softplus(x):
    return jnp.maximum(x, 0.0) + jnp.log1p(jnp.exp(-jnp.abs(x)))


def _dot(a, b):
    return jnp.dot(a, b, preferred_element_type=F32)


def _dotb(a, b):
    return _dot(a.astype(BF16), b.astype(BF16))


def _dot_nt(a, b):
    return lax.dot_general(a, b, (((1,), (1,)), ((), ())), preferred_element_type=F32)


def _dot_tn(a, b):
    return lax.dot_general(a, b, (((0,), (0,)), ((), ())), preferred_element_type=F32)


def _ada_body(c_ref, w_ref, b_ref, o_ref):
    s = _silu(c_ref[...]).astype(BF16)
    o_ref[...] = _dot(s, w_ref[...].astype(BF16)) + b_ref[...]


def _ada(cvec, w_ada, b_ada):
    depth, d, n = w_ada.shape
    tn = 512
    return pl.pallas_call(
        _ada_body,
        grid=(depth, n // tn),
        in_specs=[pl.BlockSpec((MOD_ROWS, d), lambda l, j: (0, 0)),
                  pl.BlockSpec((None, d, tn), lambda l, j: (l, 0, j)),
                  pl.BlockSpec((None, 1, tn), lambda l, j: (l, 0, j))],
        out_specs=pl.BlockSpec((None, MOD_ROWS, tn), lambda l, j: (l, 0, j)),
        out_shape=jax.ShapeDtypeStruct((depth, MOD_ROWS, n), F32),
        compiler_params=_params("arbitrary", "arbitrary"),
        name="ada",
    )(cvec, w_ada, b_ada.reshape(depth, 1, n))


def _row_sources(srcs, tm, cols, index_fn):
    if len(srcs) == 1:
        return [pl.BlockSpec((tm, cols), index_fn)]
    na = srcs[0].shape[0] // tm

    def first(*g):
        i, j = index_fn(*g)
        return jnp.minimum(i, na - 1), j

    def second(*g):
        i, j = index_fn(*g)
        return jnp.maximum(i - na, 0), j

    return [pl.BlockSpec((tm, cols), first), pl.BlockSpec((tm, cols), second)]


def _pick_rows(refs, i, na):
    if len(refs) == 1:
        return refs[0][...]
    return jnp.where(i < na, refs[0][...], refs[1][...])


def _norm_mod_body(*refs, n_src, na):
    xs, (g_ref, sh_ref, sc_ref, o_ref) = refs[:n_src], refs[n_src:]
    x = _pick_rows(xs, pl.program_id(0), na)
    y = x * lax.rsqrt(jnp.mean(x * x, axis=-1, keepdims=True) + EPS) * g_ref[...]
    o_ref[...] = (y * (1.0 + sc_ref[...]) + sh_ref[...]).astype(o_ref.dtype)


def _norm_mod(xs, g, mod3, sh_idx, sc_idx, cfg):
    tok, d = sum(x.shape[0] for x in xs), xs[0].shape[1]
    tm = cfg.tm // 2
    spec_mod = lambda idx: pl.BlockSpec((None, 1, d), lambda i: (_mod_row(i, cfg, tm) * 6 + idx, 0, 0))
    return pl.pallas_call(
        functools.partial(_norm_mod_body, n_src=len(xs), na=xs[0].shape[0] // tm),
        grid=(tok // tm,),
        in_specs=_row_sources(xs, tm, d, lambda i: (i, 0)) + [
            pl.BlockSpec((1, d), lambda i: (0, 0)), spec_mod(sh_idx), spec_mod(sc_idx)],
        out_specs=pl.BlockSpec((tm, d), lambda i: (i, 0)),
        out_shape=jax.ShapeDtypeStruct((tok, d), BF16),
        compiler_params=_params("arbitrary"),
        name="norm_mod",
    )(*xs, g.reshape(1, d), mod3, mod3)


def _final_norm_body(x_ref, g_ref, o_ref):
    x = x_ref[...]
    o_ref[...] = x * lax.rsqrt(jnp.mean(x * x, axis=-1, keepdims=True) + EPS) * g_ref[...]


def _final_norm(x, g, row0, rows, cfg):
    d = x.shape[1]
    tm = cfg.tm // 2
    off = row0 // tm
    return pl.pallas_call(
        _final_norm_body,
        grid=(rows // tm,),
        in_specs=[pl.BlockSpec((tm, d), lambda i: (i + off, 0)),
                  pl.BlockSpec((1, d), lambda i: (0, 0))],
        out_specs=pl.BlockSpec((tm, d), lambda i: (i, 0)),
        out_shape=jax.ShapeDtypeStruct((rows, d), F32),
        compiler_params=_params("arbitrary"),
        name="final_norm",
    )(x, g.reshape(1, d))


class WCols(NamedTuple):
    w: jax.Array
    layer: int
    col0: int
    act: int
    stride: int


def _wres_body(*refs, n_act, weights, fuse, n_extra, epilogue, kc, n_cast):
    acts = refs[:n_act]
    w_refs = refs[n_act:n_act + len(weights)]
    pos = n_act + len(weights)
    extras = refs[pos:pos + n_extra]
    o_ref = refs[pos + n_extra]
    scrs = refs[pos + n_extra + 1:]
    s = pl.program_id(1)

    @pl.when(s < n_cast)
    def _():
        rows = pl.ds(pl.multiple_of(s * kc, kc), kc)
        for k, w_ref in enumerate(w_refs):
            tn = w_ref.shape[1]
            scrs[k // fuse][rows, (k % fuse) * tn:(k % fuse + 1) * tn] = w_ref[...].astype(BF16)

    @pl.when(s >= n_cast)
    def _():
        prods = [_dot(acts[weights[g * fuse].act][...], scr[...]) for g, scr in enumerate(scrs)]
        epilogue(prods, extras, o_ref, s - n_cast)


def _wres_matmul(name, acts, weights, fuse, extras, epilogue, n_out, out_dtype, tm, tn, tn_out, kc):
    tok = acts[0].shape[0]
    kdim = acts[0].shape[1]
    n_cast = kdim // kc
    assert kdim % kc == 0 and all(a.shape[1] == kdim for a in acts)
    row = lambda s: jnp.maximum(s - n_cast, 0)
    chunk = lambda s: jnp.minimum(s, n_cast - 1)
    in_specs = [pl.BlockSpec((tm, kdim), lambda j, s: (row(s), 0)) for a in acts]
    args = list(acts)
    for wc in weights:
        assert wc.col0 % tn == 0 and wc.w.shape[1] == kdim
        c0, layer, stride = wc.col0 // tn, wc.layer, wc.stride
        last = pl.cdiv(wc.w.shape[2], tn) - 1
        in_specs.append(pl.BlockSpec(
            (None, kc, tn),
            lambda j, s, c0=c0, layer=layer, stride=stride, last=last:
            (layer, chunk(s), jnp.minimum(c0 + j * stride, last))))
        args.append(wc.w)
    assert len(weights) % fuse == 0
    scratch = [pltpu.VMEM((kdim, fuse * tn), BF16)] * (len(weights) // fuse)
    for arr, block, index_map in extras:
        in_specs.append(pl.BlockSpec(block, lambda j, s, index_map=index_map: index_map(j, row(s))))
        args.append(arr)
    return pl.pallas_call(
        functools.partial(_wres_body, n_act=len(acts), weights=tuple(wc._replace(w=None) for wc in weights),
                          fuse=fuse, n_extra=len(extras), epilogue=epilogue, kc=kc, n_cast=n_cast),
        grid=(n_out // tn_out, n_cast + tok // tm),
        in_specs=in_specs,
        out_specs=pl.BlockSpec((tm, tn_out), lambda j, s: (row(s), j)),
        out_shape=jax.ShapeDtypeStruct((tok, n_out), out_dtype),
        scratch_shapes=scratch,
        compiler_params=_params("arbitrary", "arbitrary"),
        name=name,
    )(*args)


def _proj_body(a_ref, w_ref, nxt_ref, o_ref, scr, *, kc, n_cast, shift, plain_tiles):
    sweep, s = pl.program_id(0), pl.program_id(1)
    slot = sweep % 2

    def cast_chunk():
        cols = pl.ds(pl.multiple_of(jnp.minimum(s, n_cast - 1) * kc, kc), kc)
        w = w_ref[...]
        w = jnp.where(sweep >= plain_tiles, jnp.concatenate([w[shift:], nxt_ref[...]], axis=0), w)
        scr[slot, :, cols] = w.astype(BF16)

    @pl.when(sweep == 0)
    def _():
        cast_chunk()

    @pl.when(sweep > 0)
    def _():
        cast_chunk()
        o_ref[...] = _dot_nt(a_ref[...], scr[1 - slot]).astype(o_ref.dtype)


def _proj(name, a, w_t, layer, n_plain, shift, n_out, cfg):
    tok, kdim = a.shape
    tm, tn, kc = cfg.tm, cfg.tn, cfg.kc
    n_cast, ni, nj = kdim // kc, tok // tm, n_out // tn
    assert kdim % kc == 0 and n_cast <= ni and n_plain % tn == 0 and n_out % tn == 0
    assert tn % shift == 0 and shift % 8 == 0
    per = tn // shift
    tile = lambda t: jnp.minimum(t, nj - 1)
    chunk = lambda t, s: jnp.where(t == nj, n_cast - 1, jnp.minimum(s, n_cast - 1))
    row = lambda t, s: jnp.where(t == 0, 0, s)
    in_specs = [pl.BlockSpec((tm, kdim), lambda t, s: (row(t, s), 0)),
                pl.BlockSpec((None, tn, kc), lambda t, s: (layer, tile(t), chunk(t, s))),
                pl.BlockSpec((None, shift, kc), lambda t, s: (layer, (tile(t) + 1) * per, chunk(t, s)))]
    args = [a, w_t, w_t]
    return pl.pallas_call(
        functools.partial(_proj_body, kc=kc, n_cast=n_cast, shift=shift, plain_tiles=n_plain // tn),
        grid=(nj + 1, ni),
        in_specs=in_specs,
        out_specs=pl.BlockSpec((tm, tn), lambda t, s: (row(t, s), jnp.maximum(t - 1, 0))),
        out_shape=jax.ShapeDtypeStruct((tok, n_out), F32),
        scratch_shapes=[pltpu.VMEM((2, tn, kdim), BF16)],
        compiler_params=_params("arbitrary", "arbitrary"),
        name=name,
    )(*args)


def _bg_body(a_ref, w_ref, o_ref):
    o_ref[...] = _dot_nt(a_ref[...], w_ref[...].astype(BF16))


def _bg_proj(h, w_t, layer, row0, nbg, cfg):
    tok, k = h.shape
    tm = cfg.tm
    assert row0 % nbg == 0
    return pl.pallas_call(
        _bg_body,
        grid=(tok // tm,),
        in_specs=[pl.BlockSpec((tm, k), lambda i: (i, 0)),
                  pl.BlockSpec((None, nbg, k), lambda i: (layer, row0 // nbg, 0))],
        out_specs=pl.BlockSpec((tm, nbg), lambda i: (i, 0)),
        out_shape=jax.ShapeDtypeStruct((tok, nbg), F32),
        compiler_params=_params("arbitrary"),
        name="proj_bg",
    )(h, w_t)


def _merge_body(*refs, n_src, na):
    os, (y_ref, wa_ref, wb_ref, ga_ref, gb_ref, out_ref) = refs[:n_src], refs[n_src:]
    pa = _dot(_pick_rows(os, pl.program_id(0), na), wa_ref[...])
    pb = _dot(y_ref[...], wb_ref[...])
    out_ref[...] = (jax.nn.sigmoid(ga_ref[...]) * pa + jax.nn.sigmoid(gb_ref[...]) * pb).astype(out_ref.dtype)


def _merge(os, y, w_pa, w_pb, layer, p2, gate_col0, cfg):
    tok = y.shape[0]
    d = w_pa.shape[2]
    tm, tn = cfg.tm, min(cfg.tn // 2, d)
    ga0 = gate_col0 // tn
    gb0 = (gate_col0 + d) // tn
    return pl.pallas_call(
        functools.partial(_merge_body, n_src=len(os), na=os[0].shape[0] // tm),
        grid=(tok // tm, d // tn),
        in_specs=_row_sources(os, tm, os[0].shape[1], lambda i, j: (i, 0)) + [
            pl.BlockSpec((tm, y.shape[1]), lambda i, j: (i, 0)),
            pl.BlockSpec((None, w_pa.shape[1], tn), lambda i, j: (layer, 0, j)),
            pl.BlockSpec((None, w_pb.shape[1], tn), lambda i, j: (layer, 0, j)),
            pl.BlockSpec((tm, tn), lambda i, j: (i, ga0 + j)),
            pl.BlockSpec((tm, tn), lambda i, j: (i, gb0 + j))],
        out_specs=pl.BlockSpec((tm, tn), lambda i, j: (i, j)),
        out_shape=jax.ShapeDtypeStruct((tok, d), BF16),
        compiler_params=_params("arbitrary", "arbitrary"),
        name="merge",
    )(*os, y, w_pa, w_pb, p2, p2)


def _ffn_up_epilogue(prods, extras, o_ref, i, *, cfg):
    cw_ref, = extras
    tm, tn_out = o_ref.shape
    nsub = len(prods)
    tn = tn_out // nsub
    j = pl.program_id(0)
    is_ctx = i < cfg.tok_ctx // tm
    gw = cfg.grid_w
    lane = lax.broadcasted_iota(jnp.int32, (tm, tn), 1)
    t = lax.broadcasted_iota(jnp.int32, (tm, tn), 0) & (cfg.seq - 1)
    zeros = jnp.zeros((gw, tn), F32)
    for u in range(nsub):
        g, v = prods[u][:, :tn], prods[u][:, tn:]
        cols = slice(u * tn, (u + 1) * tn)
        w = cw_ref[:, cols]
        gm = jnp.where(is_ctx, jnp.where(t == 0, 0.0, pltpu.roll(g, 1, 0)),
                       jnp.concatenate([zeros, g[:tm - gw]], axis=0))
        gp = jnp.where(is_ctx, jnp.where(t == cfg.seq - 1, 0.0, pltpu.roll(g, tm - 1, 0)),
                       jnp.concatenate([g[gw:], zeros], axis=0))
        y = gm * w[0:1] + g * w[1:2] + gp * w[2:3]
        valid = lane + (j * tn_out + u * tn) < cfg.d_ff
        o_ref[:, cols] = jnp.where(valid, _silu(y) * v, 0.0).astype(o_ref.dtype)


def _resid_body(*refs, nk, n_src, na):
    a_ref, w_ref = refs[:2]
    xs, (gt_ref, o_ref) = refs[2:2 + n_src], refs[2 + n_src:]
    prod = lambda: _dot(a_ref[...], w_ref[...])
    finish = lambda acc: _pick_rows(xs, pl.program_id(0), na) + gt_ref[...] * acc
    if nk == 1:
        o_ref[...] = finish(prod())
        return
    k = pl.program_id(2)

    @pl.when(k == 0)
    def _():
        o_ref[...] = prod()

    @pl.when(jnp.logical_and(k > 0, k < nk - 1))
    def _():
        o_ref[...] += prod()

    @pl.when(k == nk - 1)
    def _():
        o_ref[...] = finish(o_ref[...] + prod())


def _resid(a, w, layer, xs, mod3, gt_idx, tk, cfg, name):
    tok, kdim = a.shape
    d = w.shape[2]
    tm, tn = cfg.tm, min(cfg.tn // len(xs), d)
    nk = kdim // tk
    return pl.pallas_call(
        functools.partial(_resid_body, nk=nk, n_src=len(xs), na=xs[0].shape[0] // tm),
        grid=(tok // tm, d // tn, nk),
        in_specs=[pl.BlockSpec((tm, tk), lambda i, j, k: (i, k)),
                  pl.BlockSpec((None, tk, tn), lambda i, j, k: (layer, k, j))]
        + _row_sources(xs, tm, tn, lambda i, j, k: (i, j))
        + [pl.BlockSpec((None, 1, tn), lambda i, j, k: (_mod_row(i, cfg, tm) * 6 + gt_idx, 0, j))],
        out_specs=pl.BlockSpec((tm, tn), lambda i, j, k: (i, j)),
        out_shape=jax.ShapeDtypeStruct((tok, d), F32),
        compiler_params=_params("arbitrary", "arbitrary", "arbitrary"),
        name=name,
    )(a, w, *xs, mod3)


def _cast_pad_body(w_ref, o_ref, *, rows_in):
    tr = o_ref.shape[0]
    row = lax.broadcasted_iota(jnp.int32, o_ref.shape, 0) + pl.program_id(1) * tr
    o_ref[...] = jnp.where(row < rows_in, w_ref[...], 0.0).astype(o_ref.dtype)


def _cast_pad_rows(w, rows_out, tr):
    depth, rows_in, n = w.shape
    last = pl.cdiv(rows_in, tr) - 1
    return pl.pallas_call(
        functools.partial(_cast_pad_body, rows_in=rows_in),
        grid=(depth, rows_out // tr),
        in_specs=[pl.BlockSpec((None, tr, n), lambda l, i: (l, jnp.minimum(i, last), 0))],
        out_specs=pl.BlockSpec((None, tr, n), lambda l, i: (l, i, 0)),
        out_shape=jax.ShapeDtypeStruct((depth, rows_out, n), BF16),
        compiler_params=_params("arbitrary", "arbitrary"),
        name="cast_pad",
    )(w)


def _sc_body(xs_ref, bs_ref, cs_ref, w_ref, o_ref, *, cfg):
    tm, tn = o_ref.shape
    i = pl.program_id(0)
    per = jnp.where(i < cfg.tok_ctx // tm, cfg.seq, cfg.grid_w)
    u = cs_ref[...] * xs_ref[...]
    t = lax.broadcasted_iota(jnp.int32, (tm, tn), 0) & (per - 1)
    um = jnp.where(t == 0, 0.0, pltpu.roll(u, 1, 0))
    up = jnp.where(t == per - 1, 0.0, pltpu.roll(u, tm - 1, 0))
    w = w_ref[...]
    o_ref[...] = (bs_ref[...] * (um * w[0:1] + u * w[1:2] + up * w[2:3])).astype(o_ref.dtype)


def _sc_mixer(p2, col0, cw, cfg):
    tok = p2.shape[0]
    ws = cfg.w_sc
    tm, tn = cfg.tm, min(512, ws)
    nb = ws // tn
    c0 = col0 // tn
    assert col0 % tn == 0
    return pl.pallas_call(
        functools.partial(_sc_body, cfg=cfg),
        grid=(tok // tm, nb),
        in_specs=[pl.BlockSpec((tm, tn), lambda i, j: (i, c0 + j)),
                  pl.BlockSpec((tm, tn), lambda i, j: (i, c0 + nb + j)),
                  pl.BlockSpec((tm, tn), lambda i, j: (i, c0 + 2 * nb + j)),
                  pl.BlockSpec((3, tn), lambda i, j: (0, j))],
        out_specs=pl.BlockSpec((tm, tn), lambda i, j: (i, j)),
        out_shape=jax.ShapeDtypeStruct((tok, ws), BF16),
        compiler_params=_params("arbitrary", "arbitrary"),
        name="sc_mixer",
    )(p2, p2, p2, cw)


def _dn_body(*refs, n, per, heads, hb, has_s0, emit_state, n_prev, group):
    it = iter(refs)
    q_ref, k_ref, v_ref, z_ref = next(it), next(it), next(it), next(it)
    wq_ref, wk_ref, wv_ref = next(it), next(it), next(it)
    bgc_ref, bgt_ref, a_row_ref, dtb_row_ref, a_col_ref, dtb_col_ref, og_ref = (next(it) for _ in range(7))
    s0_ref = next(it) if has_s0 else None
    prev_ref = next(it) if n_prev else None
    o_ref = next(it)
    st_ref = next(it) if emit_state else None
    qs, ks, vs, of_s, ob_s, qw_s, n_s, oc_s, eg_s, gcol_s, grow_s = it

    nc = n // CHUNK
    width = hb * LANES
    hblk = pl.program_id(1)
    head_cols = [slice(b * LANES, (b + 1) * LANES) for b in range(hb)]

    tok = lax.broadcasted_iota(jnp.int32, (n, width), 0) & (per - 1)
    first, last = tok == 0, tok == per - 1

    def conv_silu(x_ref, w_ref):
        x, w = x_ref[...], w_ref[...]
        xm = jnp.where(first, 0.0, pltpu.roll(x, 1, 0))
        xp = jnp.where(last, 0.0, pltpu.roll(x, n - 1, 0))
        return _silu(xm * w[0:1] + x * w[1:2] + xp * w[2:3])

    q = conv_silu(q_ref, wq_ref)
    k = conv_silu(k_ref, wk_ref)
    for sl in head_cols:
        qh, kh = q[:, sl], k[:, sl]
        qs[:, sl] = qh * lax.rsqrt(jnp.sum(qh * qh, axis=-1, keepdims=True) + EPS) * (LANES ** -0.5)
        ks[:, sl] = kh * lax.rsqrt(jnp.sum(kh * kh, axis=-1, keepdims=True) + EPS)
    vs[...] = conv_silu(v_ref, wv_ref)

    nbg = bgc_ref.shape[-1]
    lane_bg = lax.broadcasted_iota(jnp.int32, (CHUNK, nbg), 1)
    is_beta = lane_bg < 2 * heads
    neg_a_row, dtb_row = -jnp.exp(a_row_ref[...]), dtb_row_ref[...]
    neg_a_col, dtb_col = -jnp.exp(a_col_ref[...]), dtb_col_ref[...]
    fwd_heads = lax.broadcasted_iota(jnp.int32, (heads, 2 * CHUNK), 1) < CHUNK

    def gates(c, carry):
        bgc = bgc_ref[c]
        gcol_s[c] = jnp.where(is_beta, jax.nn.sigmoid(bgc), neg_a_row * _softplus(bgc + dtb_row))
        gt = neg_a_col * _softplus(bgt_ref[c] + dtb_col)
        grow_s[c] = jnp.where(fwd_heads, gt[2 * heads:3 * heads], gt[3 * heads:])
        return carry

    if nc == group:
        for c in range(nc):
            gates(c, 0)
    else:
        lax.fori_loop(0, nc, gates, 0)

    sub = lax.broadcasted_iota(jnp.int32, (CHUNK, 2 * CHUNK), 0)
    lane = lax.broadcasted_iota(jnp.int32, (CHUNK, 2 * CHUNK), 1)
    fwd = lane < CHUNK
    colx = lane & (CHUNK - 1)
    bwd = jnp.logical_not(fwd)
    vis_f, vis_b = jnp.logical_and(fwd, colx <= sub), jnp.logical_and(bwd, colx >= sub)
    vis = jnp.logical_or(vis_f, vis_b)
    vis_t = jnp.logical_or(jnp.logical_and(fwd, sub <= colx), jnp.logical_and(bwd, sub >= colx))
    strict = jnp.logical_and(vis, colx != sub)
    eye = (sub == colx).astype(F32)
    blk16 = (sub // 16) == (colx // 16)
    blk32 = (sub // 32) == (colx // 32)
    off32 = jnp.logical_and(blk32, jnp.logical_not(blk16))
    fwd_row = lax.broadcasted_iota(jnp.int32, (1, 2 * CHUNK), 1) < CHUNK
    zeros_x = jnp.zeros((CHUNK, 2 * LANES), BF16)

    def bd(y):
        return jnp.concatenate([jnp.where(fwd, y, 0.0).astype(BF16), jnp.where(fwd, 0.0, y).astype(BF16)], axis=0)

    def bd_x(xf, xb):
        return jnp.concatenate([jnp.concatenate([xf, zeros_x], axis=1),
                                jnp.concatenate([zeros_x, xb], axis=1)], axis=0)

    def stage_wave(gi, wave_heads):
        units = []
        for b in wave_heads:
            for j in range(group):
                c = gi * group + j
                r0 = c * CHUNK
                if not isinstance(r0, int):
                    r0 = pl.multiple_of(r0, CHUNK)
                rows = pl.ds(r0, CHUNK)
                qc, kc, vc = qs[rows, head_cols[b]], ks[rows, head_cols[b]], vs[rows, head_cols[b]]
                kb = kc.astype(BF16)
                kb2 = jnp.concatenate([kb, kb], axis=0)
                kk = _dot_nt(kb, kb2)
                qk = _dot_nt(qc.astype(BF16), kb2)
                gall = gcol_s[c]
                hd = hblk * hb + b
                col = lambda g: jnp.sum(jnp.where(lane_bg == g * heads + hd, gall, 0.0), axis=1, keepdims=True)
                beta = (col(0), col(1))
                g_col = jnp.where(fwd, col(2), col(3))
                g_row = grow_s[c, pl.ds(hd, 1), :]
                g_rows = jnp.broadcast_to(g_row, (CHUNK, 2 * CHUNK))
                gc = (jnp.sum(jnp.where(vis_f, g_rows, 0.0), axis=1, keepdims=True),
                      jnp.sum(jnp.where(vis_b, g_rows, 0.0), axis=1, keepdims=True))
                gc_col = jnp.where(fwd, gc[0], gc[1])
                gc_row = jnp.sum(jnp.where(vis_t, g_col, 0.0), axis=0, keepdims=True)
                g_last = (jnp.sum(jnp.where(fwd_row, g_row, 0.0), axis=1, keepdims=True),
                          jnp.sum(jnp.where(fwd_row, 0.0, g_row), axis=1, keepdims=True))
                decay = jnp.where(vis, jnp.exp(jnp.where(vis, gc_col - gc_row, 0.0)), 0.0)
                e_gc = (jnp.exp(gc[0]), jnp.exp(gc[1]))
                units.append(dict(
                    b=b, c=c,
                    a=jnp.where(strict, kk * decay * jnp.where(fwd, beta[0], beta[1]), 0.0),
                    x=[jnp.concatenate([vc * beta[d], kc * (beta[d] * e_gc[d])], axis=1).astype(BF16)
                       for d in range(2)],
                    intra=jnp.where(vis, qk * decay, 0.0).astype(BF16),
                    k_dec=[(kc * jnp.exp(g_last[d] - gc[d])).astype(BF16) for d in range(2)],
                    q_dec=[qc * e_gc[d] for d in range(2)],
                    eg=[jnp.broadcast_to(jnp.exp(g_last[d]), (1, LANES)) for d in range(2)]))
        pmm = lambda x, y_bd: _dot(x.astype(BF16), y_bd)
        a_d = [jnp.where(blk16, u["a"], 0.0) for u in units]
        p2 = [pmm(x, bd(x)) for x in a_d]
        p2_bd = [bd(p) for p in p2]
        t = [eye - x for x in a_d]
        t = [x + pmm(x, p) for x, p in zip(t, p2_bd)]
        p4 = [pmm(p, pb) for p, pb in zip(p2, p2_bd)]
        p4_bd = [bd(p) for p in p4]
        t = [x + pmm(x, p) for x, p in zip(t, p4_bd)]
        p8 = [pmm(p, pb) for p, pb in zip(p4, p4_bd)]
        t = [x + pmm(x, bd(p)) for x, p in zip(t, p8)]
        mm = [pmm(x, bd(jnp.where(off32, u["a"], 0.0))) for x, u in zip(t, units)]
        t = [x - pmm(y, bd(x)) for x, y in zip(t, mm)]
        mm = [pmm(x, bd(jnp.where(blk32, 0.0, u["a"]))) for x, u in zip(t, units)]
        t = [x - pmm(y, bd(x)) for x, y in zip(t, mm)]
        xb = [pmm(x, bd_x(*u["x"])).astype(BF16) for x, u in zip(t, units)]
        xd = [[x[:, :2 * LANES], x[:, 2 * LANES:]] for x in xb]
        iw = [_dot(u["intra"], bd_x(*x)) for u, x in zip(units, xd)]
        kt = [[_dot_tn(u["k_dec"][d], x[d]) for d in range(2)] for u, x in zip(units, xd)]
        for u, iwu, ktu in zip(units, iw, kt):
            b, c = u["b"], u["c"]
            for d in range(2):
                iwd = iwu[:, 2 * LANES * d:2 * LANES * (d + 1)]
                oc_s[b, d, c] = iwd[:, :LANES]
                qw_s[b, d, c, 0:CHUNK, :] = (u["q_dec"][d] - iwd[:, LANES:]).astype(BF16)
                qw_s[b, d, c, CHUNK:, :] = ktu[d][:, LANES:].astype(BF16)
                n_s[b, d, c] = ktu[d][:, :LANES]
                eg_s[b, d, c] = u["eg"][d]

    def stage_group(gi, carry):
        for h0 in range(0, hb, DN_WAVE_HEADS):
            stage_wave(gi, range(h0, min(h0 + DN_WAVE_HEADS, hb)))
        return carry

    if nc == group:
        stage_group(0, 0)
    else:
        lax.fori_loop(0, nc // group, stage_group, 0)

    chains = [(b, d) for b in range(hb) for d in range(2)]
    if has_s0:
        s_init = tuple(s0_ref[d, b] for b, d in chains)
    else:
        s_init = tuple(jnp.zeros((LANES, LANES), F32) for _ in chains)

    def scan_step(t, carry):
        res = []
        for (b, d), s in zip(chains, carry):
            c = t if d == 0 else nc - 1 - t
            res.append(_dot(qw_s[b, d, c], s.astype(BF16)))
        out = []
        for (b, d), s, r in zip(chains, carry, res):
            c = t if d == 0 else nc - 1 - t
            r0 = c * CHUNK
            if not isinstance(r0, int):
                r0 = pl.multiple_of(r0, CHUNK)
            acc = of_s if d == 0 else ob_s
            acc[pl.ds(r0, CHUNK), head_cols[b]] = r[:CHUNK] + oc_s[b, d, c]
            out.append(s * eg_s[b, d, c] - r[CHUNK:] + n_s[b, d, c])
        return tuple(out)

    if nc == group:
        s_fin = s_init
        for t in range(nc):
            s_fin = scan_step(t, s_fin)
    else:
        s_fin = lax.fori_loop(0, nc, scan_step, s_init)

    o = of_s[...] + ob_s[...]
    z = _silu(z_ref[...])
    og = og_ref[...]
    for sl in head_cols:
        oh = o[:, sl]
        y = oh * lax.rsqrt(jnp.mean(oh * oh, axis=-1, keepdims=True) + EPS) * og
        o_ref[:, sl] = (y * z[:, sl]).astype(o_ref.dtype)
    if emit_state:
        if n_prev:
            st_ref[0:n_prev] = prev_ref[...]
        for (b, d), s in zip(chains, s_fin):
            st_ref[n_prev, d, b] = s


def _deltanet(p1, bg, conv_qkv, a_log, dt_bias, onorm_g, s0, cfg, *, latent, prev_states=None):
    hh, hb = cfg.heads, (cfg.dn_heads_lat if latent else cfg.dn_heads)
    tok, nbg = bg.shape
    n = cfg.dec_seq if latent else cfg.seq
    nseq = cfg.dec_batch if latent else cfg.batch
    blk0 = cfg.tok_ctx // n if latent else 0
    per = cfg.grid_w if latent else cfg.seq
    nc = n // CHUNK
    group = min(4, nc)
    width = hb * LANES
    nhb = hh // hb
    assert nbg == 4 * hh
    bgc = bg.reshape(tok // CHUNK, CHUNK, nbg)
    bgt = jnp.tile(bgc.transpose(0, 2, 1), (1, 1, 2))
    a_row = jnp.concatenate([jnp.zeros((2 * hh,), F32), a_log.reshape(-1)]).reshape(1, nbg)
    dtb_row = jnp.concatenate([jnp.zeros((2 * hh,), F32), dt_bias.reshape(-1)]).reshape(1, nbg)

    col = lambda g: pl.BlockSpec((n, width), lambda s, h: (blk0 + s, g * nhb + h))
    cw = lambda g: pl.BlockSpec((3, width), lambda s, h: (0, g * nhb + h))
    whole = lambda shape: pl.BlockSpec(shape, lambda s, h: (0,) * len(shape))
    in_specs = [col(0), col(1), col(2), col(3), cw(0), cw(1), cw(2),
                pl.BlockSpec((nc, CHUNK, nbg), lambda s, h: (blk0 + s, 0, 0)),
                pl.BlockSpec((nc, nbg, 2 * CHUNK), lambda s, h: (blk0 + s, 0, 0)),
                whole((1, nbg)), whole((1, nbg)), whole((nbg, 1)), whole((nbg, 1)), whole((1, LANES))]
    args = [p1, p1, p1, p1, conv_qkv, conv_qkv, conv_qkv, bgc, bgt, a_row, dtb_row,
            a_row.reshape(nbg, 1), dtb_row.reshape(nbg, 1), onorm_g.reshape(1, LANES)]
    st_spec = pl.BlockSpec((None, 2, hb, LANES, LANES), lambda s, h: (s, 0, h, 0, 0))
    if latent:
        in_specs.append(st_spec)
        args.append(s0)
    n_prev = 0 if prev_states is None else prev_states.shape[1]
    layers_spec = lambda nl: pl.BlockSpec((None, nl, 2, hb, LANES, LANES), lambda s, h: (s, 0, 0, h, 0, 0))
    if n_prev:
        in_specs.append(layers_spec(n_prev))
        args.append(prev_states)
    out_specs = [pl.BlockSpec((n, width), lambda s, h: (s, h))]
    out_shape = [jax.ShapeDtypeStruct((nseq * n, hh * LANES), BF16)]
    if not latent:
        out_specs.append(layers_spec(n_prev + 1))
        out_shape.append(jax.ShapeDtypeStruct((nseq, n_prev + 1, 2, hh, LANES, LANES), F32))
    scratch = [pltpu.VMEM((n, width), F32)] * 5 + [
        pltpu.VMEM((hb, 2, nc, CHUNK + LANES, LANES), BF16), pltpu.VMEM((hb, 2, nc, LANES, LANES), F32),
        pltpu.VMEM((hb, 2, nc, CHUNK, LANES), F32), pltpu.VMEM((hb, 2, nc, 1, LANES), F32),
        pltpu.VMEM((nc, CHUNK, nbg), F32), pltpu.VMEM((nc, hh, 2 * CHUNK), F32)]
    return pl.pallas_call(
        functools.partial(_dn_body, n=n, per=per, heads=hh, hb=hb, has_s0=latent, emit_state=not latent,
                          n_prev=n_prev, group=group),
        grid=(nseq, nhb),
        in_specs=in_specs,
        out_specs=out_specs,
        out_shape=out_shape,
        scratch_shapes=scratch,
        compiler_params=_params("arbitrary", "arbitrary"),
        name="deltanet_lat" if latent else "deltanet_ctx",
    )(*args)


def _forward(cfg, x_prompt, x_sample, state_dn, c, c_ctx, norm1_g, norm2_g, w_ada, b_ada, w_in,
             conv_qkv, a_log, dt_bias, onorm_g, conv_sc, w_pa, w_pb, w_o, w_up, conv_ff,
             w_down, final_g):
    d, hh = cfg.d, cfg.heads
    w1_n = 4 * cfg.w_hd
    nbg = 4 * hh
    xs = (x_prompt.reshape(cfg.tok_ctx, d), x_sample.reshape(-1, d))
    cvec = jnp.zeros((MOD_ROWS, d), F32).at[0].set(c_ctx).at[1:1 + cfg.dec_batch].set(c)
    mod = _ada(cvec, w_ada, b_ada)
    w_in_t = jnp.swapaxes(w_in, 1, 2)
    w_pa_b, w_pb_b, w_o_b = w_pa.astype(BF16), w_pb.astype(BF16), w_o.astype(BF16)
    w_down_b = _cast_pad_rows(w_down, cfg.ff_pad, CAST_PAD_ROWS)
    states = None
    n2 = 3 * cfg.w_sc + 2 * d
    ff_tn, ff_sub = cfg.ff_tn, FFN_SUBTILES
    ff_w = ff_tn // ff_sub
    assert cfg.tm == cfg.dec_seq and cfg.tm % cfg.seq == 0 and cfg.d_ff % ff_w == 0 and cfg.ff_pad % ff_tn == 0
    ff_weights = [WCols(w_up, 0, half * cfg.d_ff + u * ff_w, 0, ff_sub)
                  for u in range(ff_sub) for half in range(2)]
    for l in range(cfg.depth):
        mod3 = mod[l].reshape(MOD_ROWS * 6, 1, d)
        h = _norm_mod(xs, norm1_g[l], mod3, 0, 1, cfg)
        p = _proj("proj_in", h, w_in_t, l, w1_n, nbg, w1_n + n2, cfg)
        bg = _bg_proj(h, w_in_t, l, w1_n, nbg, cfg)
        o_ctx, states = _deltanet(p, bg, conv_qkv[l], a_log[l], dt_bias[l], onorm_g[l], None, cfg,
                                  latent=False, prev_states=states)
        o_lat, = _deltanet(p, bg, conv_qkv[l], a_log[l], dt_bias[l], onorm_g[l], state_dn[:, l], cfg, latent=True)
        y_sc = _sc_mixer(p, w1_n, conv_sc[l], cfg)
        merged = _merge((o_ctx, o_lat), y_sc, w_pa_b, w_pb_b, l, p, w1_n + 3 * cfg.w_sc, cfg)
        xs = (_resid(merged, w_o_b, l, xs, mod3, 2, d, cfg, "out_proj"),)
        h = _norm_mod(xs, norm2_g[l], mod3, 3, 4, cfg)
        f = _wres_matmul(
            "ffn_up", [h], [wc._replace(layer=l) for wc in ff_weights], 2,
            [(conv_ff, (None, 3, ff_tn), lambda j, i, l=l: (l, 0, j))],
            functools.partial(_ffn_up_epilogue, cfg=cfg), cfg.ff_pad, BF16, cfg.tm, ff_w, ff_tn, cfg.kc)
        xs = (_resid(f, w_down_b, l, xs, mod3, 5, cfg.ff_tk, cfg, "ffn_down"),)
    x, = xs
    y_prompt = _final_norm(x, final_g, 0, cfg.tok_ctx, cfg).reshape(x_prompt.shape)
    y_sample = _final_norm(x, final_g, cfg.tok_ctx, cfg.tok - cfg.tok_ctx, cfg).reshape(x_sample.shape)
    return y_prompt, y_sample, states


def kernel(x_prompt, x_sample, state_dn, c, c_ctx, norm1_g, norm2_g, w_ada, b_ada, w_in, conv_qkv, a_log, dt_bias, onorm_g, conv_sc, w_pa, w_pb, w_o, w_up, conv_ff, w_down, final_g):
    return _forward(CFG, x_prompt, x_sample, state_dn, c, c_ctx, norm1_g, norm2_g, w_ada, b_ada, w_in,
                    conv_qkv, a_log, dt_bias, onorm_g, conv_sc, w_pa, w_pb, w_o, w_up, conv_ff,
                    w_down, final_g)
```

```python
import functools
from typing import NamedTuple

import jax
import jax.numpy as jnp
from jax import lax
from jax.experimental import pallas as pl
from jax.experimental.pallas import tpu as pltpu

F32 = jnp.float32
BF16 = jnp.bfloat16
EPS = 1e-6
CHUNK = 64
LANES = 128
MOD_ROWS = 8
V7X_VMEM_BYTES = 64 * 1024 * 1024
VMEM_LIMIT_BYTES = V7X_VMEM_BYTES - 6 * 1024 * 1024


class Cfg(NamedTuple):
    d: int
    batch: int
    seq: int
    depth: int
    dec_batch: int
    dec_seq: int
    grid_w: int
    heads: int
    w_sc: int
    d_ff: int
    tm: int
    tn: int
    ff_tn: int
    ff_tk: int
    dn_heads: int
    kc: int
    dn_heads_lat: int = 4

    @property
    def w_hd(self):
        return self.heads * LANES

    @property
    def tok_ctx(self):
        return self.batch * self.seq

    @property
    def tok(self):
        return self.batch * self.seq + self.dec_batch * self.dec_seq

    @property
    def ff_pad(self):
        return -(-self.d_ff // self.ff_tk) * self.ff_tk


CFG = Cfg(d=4096, batch=32, seq=256, depth=2, dec_batch=2, dec_seq=1024, grid_w=64, heads=16,
          w_sc=2048, d_ff=11008, tm=1024, tn=1024, ff_tn=1024, ff_tk=2816, dn_heads=8, kc=512)
FFN_SUBTILES = 4
CAST_PAD_ROWS = 256
DN_WAVE_HEADS = 4


def _params(*sem):
    return pltpu.CompilerParams(dimension_semantics=sem, vmem_limit_bytes=VMEM_LIMIT_BYTES)


def _mod_row(i, cfg, tm):
    nct = cfg.tok_ctx // tm
    return jnp.where(i < nct, 0, 1 + (i - nct) // (cfg.dec_seq // tm))


def _silu(x):
    return x * jax.nn.sigmoid(x)


def _softplus(x):
    return jnp.maximum(x, 0.0) + jnp.log1p(jnp.exp(-jnp.abs(x)))


def _dot(a, b):
    return jnp.dot(a, b, preferred_element_type=F32)


def _dotb(a, b):
    return _dot(a.astype(BF16), b.astype(BF16))


def _dot_nt(a, b):
    return lax.dot_general(a, b, (((1,), (1,)), ((), ())), preferred_element_type=F32)


def _dot_tn(a, b):
    return lax.dot_general(a, b, (((0,), (0,)), ((), ())), preferred_element_type=F32)


def _ada_body(c_ref, w_ref, b_ref, o_ref):
    s = _silu(c_ref[...]).astype(BF16)
    o_ref[...] = _dot(s, w_ref[...].astype(BF16)) + b_ref[...]


def _ada(cvec, w_ada, b_ada):
    depth, d, n = w_ada.shape
    tn = 512
    return pl.pallas_call(
        _ada_body,
        grid=(depth, n // tn),
        in_specs=[pl.BlockSpec((MOD_ROWS, d), lambda l, j: (0, 0)),
                  pl.BlockSpec((None, d, tn), lambda l, j: (l, 0, j)),
                  pl.BlockSpec((None, 1, tn), lambda l, j: (l, 0, j))],
        out_specs=pl.BlockSpec((None, MOD_ROWS, tn), lambda l, j: (l, 0, j)),
        out_shape=jax.ShapeDtypeStruct((depth, MOD_ROWS, n), F32),
        compiler_params=_params("arbitrary", "arbitrary"),
        name="ada",
    )(cvec, w_ada, b_ada.reshape(depth, 1, n))


def _row_sources(srcs, tm, cols, index_fn):
    if len(srcs) == 1:
        return [pl.BlockSpec((tm, cols), index_fn)]
    na = srcs[0].shape[0] // tm

    def first(*g):
        i, j = index_fn(*g)
        return jnp.minimum(i, na - 1), j

    def second(*g):
        i, j = index_fn(*g)
        return jnp.maximum(i - na, 0), j

    return [pl.BlockSpec((tm, cols), first), pl.BlockSpec((tm, cols), second)]


def _pick_rows(refs, i, na):
    if len(refs) == 1:
        return refs[0][...]
    return jnp.where(i < na, refs[0][...], refs[1][...])


def _norm_mod_body(*refs, n_src, na):
    xs, (g_ref, sh_ref, sc_ref, o_ref) = refs[:n_src], refs[n_src:]
    x = _pick_rows(xs, pl.program_id(0), na)
    y = x * lax.rsqrt(jnp.mean(x * x, axis=-1, keepdims=True) + EPS) * g_ref[...]
    o_ref[...] = (y * (1.0 + sc_ref[...]) + sh_ref[...]).astype(o_ref.dtype)


def _norm_mod(xs, g, mod3, sh_idx, sc_idx, cfg):
    tok, d = sum(x.shape[0] for x in xs), xs[0].shape[1]
    tm = cfg.tm // 2
    spec_mod = lambda idx: pl.BlockSpec((None, 1, d), lambda i: (_mod_row(i, cfg, tm) * 6 + idx, 0, 0))
    return pl.pallas_call(
        functools.partial(_norm_mod_body, n_src=len(xs), na=xs[0].shape[0] // tm),
        grid=(tok // tm,),
        in_specs=_row_sources(xs, tm, d, lambda i: (i, 0)) + [
            pl.BlockSpec((1, d), lambda i: (0, 0)), spec_mod(sh_idx), spec_mod(sc_idx)],
        out_specs=pl.BlockSpec((tm, d), lambda i: (i, 0)),
        out_shape=jax.ShapeDtypeStruct((tok, d), BF16),
        compiler_params=_params("arbitrary"),
        name="norm_mod",
    )(*xs, g.reshape(1, d), mod3, mod3)


def _final_norm_body(x_ref, g_ref, o_ref):
    x = x_ref[...]
    o_ref[...] = x * lax.rsqrt(jnp.mean(x * x, axis=-1, keepdims=True) + EPS) * g_ref[...]


def _final_norm(x, g, row0, rows, cfg):
    d = x.shape[1]
    tm = cfg.tm // 2
    off = row0 // tm
    return pl.pallas_call(
        _final_norm_body,
        grid=(rows // tm,),
        in_specs=[pl.BlockSpec((tm, d), lambda i: (i + off, 0)),
                  pl.BlockSpec((1, d), lambda i: (0, 0))],
        out_specs=pl.BlockSpec((tm, d), lambda i: (i, 0)),
        out_shape=jax.ShapeDtypeStruct((rows, d), F32),
        compiler_params=_params("arbitrary"),
        name="final_norm",
    )(x, g.reshape(1, d))


class WCols(NamedTuple):
    w: jax.Array
    layer: int
    col0: int
    act: int
    stride: int


def _wres_body(*refs, n_act, weights, fuse, n_extra, epilogue, kc, n_cast):
    acts = refs[:n_act]
    w_refs = refs[n_act:n_act + len(weights)]
    pos = n_act + len(weights)
    extras = refs[pos:pos + n_extra]
    o_ref = refs[pos + n_extra]
    scrs = refs[pos + n_extra + 1:]
    s = pl.program_id(1)

    @pl.when(s < n_cast)
    def _():
        rows = pl.ds(pl.multiple_of(s * kc, kc), kc)
        for k, w_ref in enumerate(w_refs):
            tn = w_ref.shape[1]
            scrs[k // fuse][rows, (k % fuse) * tn:(k % fuse + 1) * tn] = w_ref[...].astype(BF16)

    @pl.when(s >= n_cast)
    def _():
        prods = [_dot(acts[weights[g * fuse].act][...], scr[...]) for g, scr in enumerate(scrs)]
        epilogue(prods, extras, o_ref, s - n_cast)


def _wres_matmul(name, acts, weights, fuse, extras, epilogue, n_out, out_dtype, tm, tn, tn_out, kc):
    tok = acts[0].shape[0]
    kdim = acts[0].shape[1]
    n_cast = kdim // kc
    assert kdim % kc == 0 and all(a.shape[1] == kdim for a in acts)
    row = lambda s: jnp.maximum(s - n_cast, 0)
    chunk = lambda s: jnp.minimum(s, n_cast - 1)
    in_specs = [pl.BlockSpec((tm, kdim), lambda j, s: (row(s), 0)) for a in acts]
    args = list(acts)
    for wc in weights:
        assert wc.col0 % tn == 0 and wc.w.shape[1] == kdim
        c0, layer, stride = wc.col0 // tn, wc.layer, wc.stride
        last = pl.cdiv(wc.w.shape[2], tn) - 1
        in_specs.append(pl.BlockSpec(
            (None, kc, tn),
            lambda j, s, c0=c0, layer=layer, stride=stride, last=last:
            (layer, chunk(s), jnp.minimum(c0 + j * stride, last))))
        args.append(wc.w)
    assert len(weights) % fuse == 0
    scratch = [pltpu.VMEM((kdim, fuse * tn), BF16)] * (len(weights) // fuse)
    for arr, block, index_map in extras:
        in_specs.append(pl.BlockSpec(block, lambda j, s, index_map=index_map: index_map(j, row(s))))
        args.append(arr)
    return pl.pallas_call(
        functools.partial(_wres_body, n_act=len(acts), weights=tuple(wc._replace(w=None) for wc in weights),
                          fuse=fuse, n_extra=len(extras), epilogue=epilogue, kc=kc, n_cast=n_cast),
        grid=(n_out // tn_out, n_cast + tok // tm),
        in_specs=in_specs,
        out_specs=pl.BlockSpec((tm, tn_out), lambda j, s: (row(s), j)),
        out_shape=jax.ShapeDtypeStruct((tok, n_out), out_dtype),
        scratch_shapes=scratch,
        compiler_params=_params("arbitrary", "arbitrary"),
        name=name,
    )(*args)


def _proj_body(*refs, kc, n_cast, shift):
    if shift:
        a_ref, w_ref, nxt_ref, o_ref, scr = refs
    else:
        a_ref, w_ref, o_ref, scr = refs
    sweep, s = pl.program_id(0), pl.program_id(1)
    slot = sweep % 2

    def cast_chunk():
        cols = pl.ds(pl.multiple_of(jnp.minimum(s, n_cast - 1) * kc, kc), kc)
        w = w_ref[...]
        if shift:
            w = jnp.concatenate([w[shift:], nxt_ref[...]], axis=0)
        scr[slot, :, cols] = w.astype(BF16)

    @pl.when(sweep == 0)
    def _():
        cast_chunk()

    @pl.when(sweep > 0)
    def _():
        cast_chunk()
        o_ref[...] = _dot_nt(a_ref[...], scr[1 - slot]).astype(o_ref.dtype)


def _proj(name, a, w_t, layer, row0, shift, n_out, cfg):
    tok, kdim = a.shape
    tm, tn, kc = cfg.tm, cfg.tn, cfg.kc
    n_cast, ni, nj = kdim // kc, tok // tm, n_out // tn
    assert kdim % kc == 0 and n_cast <= ni and row0 % tn == 0 and n_out % tn == 0
    c0 = row0 // tn
    tile = lambda t: jnp.minimum(t, nj - 1)
    chunk = lambda t, s: jnp.where(t == nj, n_cast - 1, jnp.minimum(s, n_cast - 1))
    row = lambda t, s: jnp.where(t == 0, 0, s)
    in_specs = [pl.BlockSpec((tm, kdim), lambda t, s: (row(t, s), 0)),
                pl.BlockSpec((None, tn, kc), lambda t, s: (layer, c0 + tile(t), chunk(t, s)))]
    args = [a, w_t]
    if shift:
        assert tn % shift == 0 and shift % 8 == 0
        per = tn // shift
        in_specs.append(pl.BlockSpec((None, shift, kc),
                                     lambda t, s: (layer, (c0 + tile(t) + 1) * per, chunk(t, s))))
        args.append(w_t)
    return pl.pallas_call(
        functools.partial(_proj_body, kc=kc, n_cast=n_cast, shift=shift),
        grid=(nj + 1, ni),
        in_specs=in_specs,
        out_specs=pl.BlockSpec((tm, tn), lambda t, s: (row(t, s), jnp.maximum(t - 1, 0))),
        out_shape=jax.ShapeDtypeStruct((tok, n_out), F32),
        scratch_shapes=[pltpu.VMEM((2, tn, kdim), BF16)],
        compiler_params=_params("arbitrary", "arbitrary"),
        name=name,
    )(*args)


def _bg_body(a_ref, w_ref, o_ref):
    o_ref[...] = _dot_nt(a_ref[...], w_ref[...].astype(BF16))


def _bg_proj(h, w_t, layer, row0, nbg, cfg):
    tok, k = h.shape
    tm = cfg.tm
    assert row0 % nbg == 0
    return pl.pallas_call(
        _bg_body,
        grid=(tok // tm,),
        in_specs=[pl.BlockSpec((tm, k), lambda i: (i, 0)),
                  pl.BlockSpec((None, nbg, k), lambda i: (layer, row0 // nbg, 0))],
        out_specs=pl.BlockSpec((tm, nbg), lambda i: (i, 0)),
        out_shape=jax.ShapeDtypeStruct((tok, nbg), F32),
        compiler_params=_params("arbitrary"),
        name="proj_bg",
    )(h, w_t)


def _merge_body(*refs, n_src, na):
    os, (y_ref, wa_ref, wb_ref, ga_ref, gb_ref, out_ref) = refs[:n_src], refs[n_src:]
    pa = _dot(_pick_rows(os, pl.program_id(0), na), wa_ref[...])
    pb = _dot(y_ref[...], wb_ref[...])
    out_ref[...] = (jax.nn.sigmoid(ga_ref[...]) * pa + jax.nn.sigmoid(gb_ref[...]) * pb).astype(out_ref.dtype)


def _merge(os, y, w_pa, w_pb, layer, p2, gate_col0, cfg):
    tok = y.shape[0]
    d = w_pa.shape[2]
    tm, tn = cfg.tm, min(cfg.tn // 2, d)
    ga0 = gate_col0 // tn
    gb0 = (gate_col0 + d) // tn
    return pl.pallas_call(
        functools.partial(_merge_body, n_src=len(os), na=os[0].shape[0] // tm),
        grid=(tok // tm, d // tn),
        in_specs=_row_sources(os, tm, os[0].shape[1], lambda i, j: (i, 0)) + [
            pl.BlockSpec((tm, y.shape[1]), lambda i, j: (i, 0)),
            pl.BlockSpec((None, w_pa.shape[1], tn), lambda i, j: (layer, 0, j)),
            pl.BlockSpec((None, w_pb.shape[1], tn), lambda i, j: (layer, 0, j)),
            pl.BlockSpec((tm, tn), lambda i, j: (i, ga0 + j)),
            pl.BlockSpec((tm, tn), lambda i, j: (i, gb0 + j))],
        out_specs=pl.BlockSpec((tm, tn), lambda i, j: (i, j)),
        out_shape=jax.ShapeDtypeStruct((tok, d), BF16),
        compiler_params=_params("arbitrary", "arbitrary"),
        name="merge",
    )(*os, y, w_pa, w_pb, p2, p2)


def _ffn_up_epilogue(prods, extras, o_ref, i, *, cfg):
    cw_ref, = extras
    tm, tn_out = o_ref.shape
    nsub = len(prods)
    tn = tn_out // nsub
    j = pl.program_id(0)
    is_ctx = i < cfg.tok_ctx // tm
    gw = cfg.grid_w
    lane = lax.broadcasted_iota(jnp.int32, (tm, tn), 1)
    t = lax.broadcasted_iota(jnp.int32, (tm, tn), 0) & (cfg.seq - 1)
    zeros = jnp.zeros((gw, tn), F32)
    for u in range(nsub):
        g, v = prods[u][:, :tn], prods[u][:, tn:]
        cols = slice(u * tn, (u + 1) * tn)
        w = cw_ref[:, cols]
        gm = jnp.where(is_ctx, jnp.where(t == 0, 0.0, pltpu.roll(g, 1, 0)),
                       jnp.concatenate([zeros, g[:tm - gw]], axis=0))
        gp = jnp.where(is_ctx, jnp.where(t == cfg.seq - 1, 0.0, pltpu.roll(g, tm - 1, 0)),
                       jnp.concatenate([g[gw:], zeros], axis=0))
        y = gm * w[0:1] + g * w[1:2] + gp * w[2:3]
        valid = lane + (j * tn_out + u * tn) < cfg.d_ff
        o_ref[:, cols] = jnp.where(valid, _silu(y) * v, 0.0).astype(o_ref.dtype)


def _resid_body(*refs, nk, n_src, na):
    a_ref, w_ref = refs[:2]
    xs, (gt_ref, o_ref) = refs[2:2 + n_src], refs[2 + n_src:]
    prod = lambda: _dot(a_ref[...], w_ref[...])
    finish = lambda acc: _pick_rows(xs, pl.program_id(0), na) + gt_ref[...] * acc
    if nk == 1:
        o_ref[...] = finish(prod())
        return
    k = pl.program_id(2)

    @pl.when(k == 0)
    def _():
        o_ref[...] = prod()

    @pl.when(jnp.logical_and(k > 0, k < nk - 1))
    def _():
        o_ref[...] += prod()

    @pl.when(k == nk - 1)
    def _():
        o_ref[...] = finish(o_ref[...] + prod())


def _resid(a, w, layer, xs, mod3, gt_idx, tk, cfg, name):
    tok, kdim = a.shape
    d = w.shape[2]
    tm, tn = cfg.tm, min(cfg.tn // len(xs), d)
    nk = kdim // tk
    return pl.pallas_call(
        functools.partial(_resid_body, nk=nk, n_src=len(xs), na=xs[0].shape[0] // tm),
        grid=(tok // tm, d // tn, nk),
        in_specs=[pl.BlockSpec((tm, tk), lambda i, j, k: (i, k)),
                  pl.BlockSpec((None, tk, tn), lambda i, j, k: (layer, k, j))]
        + _row_sources(xs, tm, tn, lambda i, j, k: (i, j))
        + [pl.BlockSpec((None, 1, tn), lambda i, j, k: (_mod_row(i, cfg, tm) * 6 + gt_idx, 0, j))],
        out_specs=pl.BlockSpec((tm, tn), lambda i, j, k: (i, j)),
        out_shape=jax.ShapeDtypeStruct((tok, d), F32),
        compiler_params=_params("arbitrary", "arbitrary", "arbitrary"),
        name=name,
    )(a, w, *xs, mod3)


def _cast_pad_body(w_ref, o_ref, *, rows_in):
    tr = o_ref.shape[0]
    row = lax.broadcasted_iota(jnp.int32, o_ref.shape, 0) + pl.program_id(1) * tr
    o_ref[...] = jnp.where(row < rows_in, w_ref[...], 0.0).astype(o_ref.dtype)


def _cast_pad_rows(w, rows_out, tr):
    depth, rows_in, n = w.shape
    last = pl.cdiv(rows_in, tr) - 1
    return pl.pallas_call(
        functools.partial(_cast_pad_body, rows_in=rows_in),
        grid=(depth, rows_out // tr),
        in_specs=[pl.BlockSpec((None, tr, n), lambda l, i: (l, jnp.minimum(i, last), 0))],
        out_specs=pl.BlockSpec((None, tr, n), lambda l, i: (l, i, 0)),
        out_shape=jax.ShapeDtypeStruct((depth, rows_out, n), BF16),
        compiler_params=_params("arbitrary", "arbitrary"),
        name="cast_pad",
    )(w)


def _sc_body(xs_ref, bs_ref, cs_ref, w_ref, o_ref, *, cfg):
    tm, tn = o_ref.shape
    i = pl.program_id(0)
    per = jnp.where(i < cfg.tok_ctx // tm, cfg.seq, cfg.grid_w)
    u = cs_ref[...] * xs_ref[...]
    t = lax.broadcasted_iota(jnp.int32, (tm, tn), 0) & (per - 1)
    um = jnp.where(t == 0, 0.0, pltpu.roll(u, 1, 0))
    up = jnp.where(t == per - 1, 0.0, pltpu.roll(u, tm - 1, 0))
    w = w_ref[...]
    o_ref[...] = (bs_ref[...] * (um * w[0:1] + u * w[1:2] + up * w[2:3])).astype(o_ref.dtype)


def _sc_mixer(p2, cw, cfg):
    tok = p2.shape[0]
    ws = cfg.w_sc
    tm, tn = cfg.tm, min(512, ws)
    nb = ws // tn
    return pl.pallas_call(
        functools.partial(_sc_body, cfg=cfg),
        grid=(tok // tm, nb),
        in_specs=[pl.BlockSpec((tm, tn), lambda i, j: (i, j)),
                  pl.BlockSpec((tm, tn), lambda i, j: (i, nb + j)),
                  pl.BlockSpec((tm, tn), lambda i, j: (i, 2 * nb + j)),
                  pl.BlockSpec((3, tn), lambda i, j: (0, j))],
        out_specs=pl.BlockSpec((tm, tn), lambda i, j: (i, j)),
        out_shape=jax.ShapeDtypeStruct((tok, ws), BF16),
        compiler_params=_params("arbitrary", "arbitrary"),
        name="sc_mixer",
    )(p2, p2, p2, cw)


def _dn_body(*refs, n, per, heads, hb, has_s0, emit_state, n_prev, group):
    it = iter(refs)
    q_ref, k_ref, v_ref, z_ref = next(it), next(it), next(it), next(it)
    wq_ref, wk_ref, wv_ref = next(it), next(it), next(it)
    bgc_ref, bgt_ref, a_row_ref, dtb_row_ref, a_col_ref, dtb_col_ref, og_ref = (next(it) for _ in range(7))
    s0_ref = next(it) if has_s0 else None
    prev_ref = next(it) if n_prev else None
    o_ref = next(it)
    st_ref = next(it) if emit_state else None
    qs, ks, vs, of_s, ob_s, qw_s, n_s, oc_s, eg_s, gcol_s, grow_s = it

    nc = n // CHUNK
    width = hb * LANES
    hblk = pl.program_id(1)
    head_cols = [slice(b * LANES, (b + 1) * LANES) for b in range(hb)]

    tok = lax.broadcasted_iota(jnp.int32, (n, width), 0) & (per - 1)
    first, last = tok == 0, tok == per - 1

    def conv_silu(x_ref, w_ref):
        x, w = x_ref[...], w_ref[...]
        xm = jnp.where(first, 0.0, pltpu.roll(x, 1, 0))
        xp = jnp.where(last, 0.0, pltpu.roll(x, n - 1, 0))
        return _silu(xm * w[0:1] + x * w[1:2] + xp * w[2:3])

    q = conv_silu(q_ref, wq_ref)
    k = conv_silu(k_ref, wk_ref)
    for sl in head_cols:
        qh, kh = q[:, sl], k[:, sl]
        qs[:, sl] = qh * lax.rsqrt(jnp.sum(qh * qh, axis=-1, keepdims=True) + EPS) * (LANES ** -0.5)
        ks[:, sl] = kh * lax.rsqrt(jnp.sum(kh * kh, axis=-1, keepdims=True) + EPS)
    vs[...] = conv_silu(v_ref, wv_ref)

    nbg = bgc_ref.shape[-1]
    lane_bg = lax.broadcasted_iota(jnp.int32, (CHUNK, nbg), 1)
    is_beta = lane_bg < 2 * heads
    neg_a_row, dtb_row = -jnp.exp(a_row_ref[...]), dtb_row_ref[...]
    neg_a_col, dtb_col = -jnp.exp(a_col_ref[...]), dtb_col_ref[...]
    fwd_heads = lax.broadcasted_iota(jnp.int32, (heads, 2 * CHUNK), 1) < CHUNK

    def gates(c, carry):
        bgc = bgc_ref[c]
        gcol_s[c] = jnp.where(is_beta, jax.nn.sigmoid(bgc), neg_a_row * _softplus(bgc + dtb_row))
        gt = neg_a_col * _softplus(bgt_ref[c] + dtb_col)
        grow_s[c] = jnp.where(fwd_heads, gt[2 * heads:3 * heads], gt[3 * heads:])
        return carry

    if nc == group:
        for c in range(nc):
            gates(c, 0)
    else:
        lax.fori_loop(0, nc, gates, 0)

    sub = lax.broadcasted_iota(jnp.int32, (CHUNK, 2 * CHUNK), 0)
    lane = lax.broadcasted_iota(jnp.int32, (CHUNK, 2 * CHUNK), 1)
    fwd = lane < CHUNK
    colx = lane & (CHUNK - 1)
    bwd = jnp.logical_not(fwd)
    vis_f, vis_b = jnp.logical_and(fwd, colx <= sub), jnp.logical_and(bwd, colx >= sub)
    vis = jnp.logical_or(vis_f, vis_b)
    vis_t = jnp.logical_or(jnp.logical_and(fwd, sub <= colx), jnp.logical_and(bwd, sub >= colx))
    strict = jnp.logical_and(vis, colx != sub)
    eye = (sub == colx).astype(F32)
    blk8, blk16, blk32 = ((sub // w) == (colx // w) for w in (8, 16, 32))
    off16 = jnp.logical_and(blk16, jnp.logical_not(blk8))
    off32 = jnp.logical_and(blk32, jnp.logical_not(blk16))
    off64 = jnp.logical_not(blk32)
    fwd_row = lax.broadcasted_iota(jnp.int32, (1, 2 * CHUNK), 1) < CHUNK
    zeros_x = jnp.zeros((CHUNK, 2 * LANES), BF16)

    def bd(y):
        return jnp.concatenate([jnp.where(fwd, y, 0.0).astype(BF16), jnp.where(fwd, 0.0, y).astype(BF16)], axis=0)

    def bd_x(xf, xb):
        return jnp.concatenate([jnp.concatenate([xf, zeros_x], axis=1),
                                jnp.concatenate([zeros_x, xb], axis=1)], axis=0)

    def stage_wave(gi, wave_heads):
        units = []
        for b in wave_heads:
            for j in range(group):
                c = gi * group + j
                r0 = c * CHUNK
                if not isinstance(r0, int):
                    r0 = pl.multiple_of(r0, CHUNK)
                rows = pl.ds(r0, CHUNK)
                qc, kc, vc = qs[rows, head_cols[b]], ks[rows, head_cols[b]], vs[rows, head_cols[b]]
                kb = kc.astype(BF16)
                kb2 = jnp.concatenate([kb, kb], axis=0)
                kk = _dot_nt(kb, kb2)
                qk = _dot_nt(qc.astype(BF16), kb2)
                gall = gcol_s[c]
                hd = hblk * hb + b
                col = lambda g: jnp.sum(jnp.where(lane_bg == g * heads + hd, gall, 0.0), axis=1, keepdims=True)
                beta = (col(0), col(1))
                g_col = jnp.where(fwd, col(2), col(3))
                g_row = grow_s[c, pl.ds(hd, 1), :]
                g_rows = jnp.broadcast_to(g_row, (CHUNK, 2 * CHUNK))
                gc = (jnp.sum(jnp.where(vis_f, g_rows, 0.0), axis=1, keepdims=True),
                      jnp.sum(jnp.where(vis_b, g_rows, 0.0), axis=1, keepdims=True))
                gc_col = jnp.where(fwd, gc[0], gc[1])
                gc_row = jnp.sum(jnp.where(vis_t, g_col, 0.0), axis=0, keepdims=True)
                g_last = (jnp.sum(jnp.where(fwd_row, g_row, 0.0), axis=1, keepdims=True),
                          jnp.sum(jnp.where(fwd_row, 0.0, g_row), axis=1, keepdims=True))
                decay = jnp.where(vis, jnp.exp(jnp.where(vis, gc_col - gc_row, 0.0)), 0.0)
                e_gc = (jnp.exp(gc[0]), jnp.exp(gc[1]))
                units.append(dict(
                    b=b, c=c,
                    a=jnp.where(strict, kk * decay * jnp.where(fwd, beta[0], beta[1]), 0.0),
                    x=[jnp.concatenate([vc * beta[d], kc * (beta[d] * e_gc[d])], axis=1).astype(BF16)
                       for d in range(2)],
                    intra=jnp.where(vis, qk * decay, 0.0).astype(BF16),
                    k_dec=[(kc * jnp.exp(g_last[d] - gc[d])).astype(BF16) for d in range(2)],
                    q_dec=[qc * e_gc[d] for d in range(2)],
                    eg=[jnp.broadcast_to(jnp.exp(g_last[d]), (1, LANES)) for d in range(2)]))
        pmm = lambda x, y_bd: _dot(x.astype(BF16), y_bd)
        a_d = [jnp.where(blk8, u["a"], 0.0) for u in units]
        p2 = [pmm(x, bd(x)) for x in a_d]
        p2_bd = [bd(p) for p in p2]
        t = [eye - x for x in a_d]
        t = [x + pmm(x, p) for x, p in zip(t, p2_bd)]
        p4 = [pmm(p, pb) for p, pb in zip(p2, p2_bd)]
        t = [x + pmm(x, bd(p)) for x, p in zip(t, p4)]
        for off in (off16, off32, off64):
            mm = [pmm(x, bd(jnp.where(off, u["a"], 0.0))) for x, u in zip(t, units)]
            t = [x - pmm(y, bd(x)) for x, y in zip(t, mm)]
        xb = [pmm(x, bd_x(*u["x"])).astype(BF16) for x, u in zip(t, units)]
        xd = [[x[:, :2 * LANES], x[:, 2 * LANES:]] for x in xb]
        iw = [_dot(u["intra"], bd_x(*x)) for u, x in zip(units, xd)]
        kt = [[_dot_tn(u["k_dec"][d], x[d]) for d in range(2)] for u, x in zip(units, xd)]
        for u, iwu, ktu in zip(units, iw, kt):
            b, c = u["b"], u["c"]
            for d in range(2):
                iwd = iwu[:, 2 * LANES * d:2 * LANES * (d + 1)]
                oc_s[b, d, c] = iwd[:, :LANES]
                qw_s[b, d, c, 0:CHUNK, :] = (u["q_dec"][d] - iwd[:, LANES:]).astype(BF16)
                qw_s[b, d, c, CHUNK:, :] = ktu[d][:, LANES:].astype(BF16)
                n_s[b, d, c] = ktu[d][:, :LANES]
                eg_s[b, d, c] = u["eg"][d]

    def stage_group(gi, carry):
        for h0 in range(0, hb, DN_WAVE_HEADS):
            stage_wave(gi, range(h0, min(h0 + DN_WAVE_HEADS, hb)))
        return carry

    if nc == group:
        stage_group(0, 0)
    else:
        lax.fori_loop(0, nc // group, stage_group, 0)

    chains = [(b, d) for b in range(hb) for d in range(2)]
    if has_s0:
        s_init = tuple(s0_ref[d, b] for b, d in chains)
    else:
        s_init = tuple(jnp.zeros((LANES, LANES), F32) for _ in chains)

    def scan_step(t, carry):
        res = []
        for (b, d), s in zip(chains, carry):
            c = t if d == 0 else nc - 1 - t
            res.append(_dot(qw_s[b, d, c], s.astype(BF16)))
        out = []
        for (b, d), s, r in zip(chains, carry, res):
            c = t if d == 0 else nc - 1 - t
            r0 = c * CHUNK
            if not isinstance(r0, int):
                r0 = pl.multiple_of(r0, CHUNK)
            acc = of_s if d == 0 else ob_s
            acc[pl.ds(r0, CHUNK), head_cols[b]] = r[:CHUNK] + oc_s[b, d, c]
            out.append(s * eg_s[b, d, c] - r[CHUNK:] + n_s[b, d, c])
        return tuple(out)

    if nc == group:
        s_fin = s_init
        for t in range(nc):
            s_fin = scan_step(t, s_fin)
    else:
        s_fin = lax.fori_loop(0, nc, scan_step, s_init)

    o = of_s[...] + ob_s[...]
    z = _silu(z_ref[...])
    og = og_ref[...]
    for sl in head_cols:
        oh = o[:, sl]
        y = oh * lax.rsqrt(jnp.mean(oh * oh, axis=-1, keepdims=True) + EPS) * og
        o_ref[:, sl] = (y * z[:, sl]).astype(o_ref.dtype)
    if emit_state:
        if n_prev:
            st_ref[0:n_prev] = prev_ref[...]
        for (b, d), s in zip(chains, s_fin):
            st_ref[n_prev, d, b] = s


def _deltanet(p1, bg, conv_qkv, a_log, dt_bias, onorm_g, s0, cfg, *, latent, prev_states=None):
    hh, hb = cfg.heads, (cfg.dn_heads_lat if latent else cfg.dn_heads)
    tok, nbg = bg.shape
    n = cfg.dec_seq if latent else cfg.seq
    nseq = cfg.dec_batch if latent else cfg.batch
    blk0 = cfg.tok_ctx // n if latent else 0
    per = cfg.grid_w if latent else cfg.seq
    nc = n // CHUNK
    group = min(4, nc)
    width = hb * LANES
    nhb = hh // hb
    assert nbg == 4 * hh
    bgc = bg.reshape(tok // CHUNK, CHUNK, nbg)
    bgt = jnp.tile(bgc.transpose(0, 2, 1), (1, 1, 2))
    a_row = jnp.concatenate([jnp.zeros((2 * hh,), F32), a_log.reshape(-1)]).reshape(1, nbg)
    dtb_row = jnp.concatenate([jnp.zeros((2 * hh,), F32), dt_bias.reshape(-1)]).reshape(1, nbg)

    col = lambda g: pl.BlockSpec((n, width), lambda s, h: (blk0 + s, g * nhb + h))
    cw = lambda g: pl.BlockSpec((3, width), lambda s, h: (0, g * nhb + h))
    whole = lambda shape: pl.BlockSpec(shape, lambda s, h: (0,) * len(shape))
    in_specs = [col(0), col(1), col(2), col(3), cw(0), cw(1), cw(2),
                pl.BlockSpec((nc, CHUNK, nbg), lambda s, h: (blk0 + s, 0, 0)),
                pl.BlockSpec((nc, nbg, 2 * CHUNK), lambda s, h: (blk0 + s, 0, 0)),
                whole((1, nbg)), whole((1, nbg)), whole((nbg, 1)), whole((nbg, 1)), whole((1, LANES))]
    args = [p1, p1, p1, p1, conv_qkv, conv_qkv, conv_qkv, bgc, bgt, a_row, dtb_row,
            a_row.reshape(nbg, 1), dtb_row.reshape(nbg, 1), onorm_g.reshape(1, LANES)]
    st_spec = pl.BlockSpec((None, 2, hb, LANES, LANES), lambda s, h: (s, 0, h, 0, 0))
    if latent:
        in_specs.append(st_spec)
        args.append(s0)
    n_prev = 0 if prev_states is None else prev_states.shape[1]
    layers_spec = lambda nl: pl.BlockSpec((None, nl, 2, hb, LANES, LANES), lambda s, h: (s, 0, 0, h, 0, 0))
    if n_prev:
        in_specs.append(layers_spec(n_prev))
        args.append(prev_states)
    out_specs = [pl.BlockSpec((n, width), lambda s, h: (s, h))]
    out_shape = [jax.ShapeDtypeStruct((nseq * n, hh * LANES), BF16)]
    if not latent:
        out_specs.append(layers_spec(n_prev + 1))
        out_shape.append(jax.ShapeDtypeStruct((nseq, n_prev + 1, 2, hh, LANES, LANES), F32))
    scratch = [pltpu.VMEM((n, width), F32)] * 5 + [
        pltpu.VMEM((hb, 2, nc, CHUNK + LANES, LANES), BF16), pltpu.VMEM((hb, 2, nc, LANES, LANES), F32),
        pltpu.VMEM((hb, 2, nc, CHUNK, LANES), F32), pltpu.VMEM((hb, 2, nc, 1, LANES), F32),
        pltpu.VMEM((nc, CHUNK, nbg), F32), pltpu.VMEM((nc, hh, 2 * CHUNK), F32)]
    return pl.pallas_call(
        functools.partial(_dn_body, n=n, per=per, heads=hh, hb=hb, has_s0=latent, emit_state=not latent,
                          n_prev=n_prev, group=group),
        grid=(nseq, nhb),
        in_specs=in_specs,
        out_specs=out_specs,
        out_shape=out_shape,
        scratch_shapes=scratch,
        compiler_params=_params("arbitrary", "arbitrary"),
        name="deltanet_lat" if latent else "deltanet_ctx",
    )(*args)


def _forward(cfg, x_prompt, x_sample, state_dn, c, c_ctx, norm1_g, norm2_g, w_ada, b_ada, w_in,
             conv_qkv, a_log, dt_bias, onorm_g, conv_sc, w_pa, w_pb, w_o, w_up, conv_ff,
             w_down, final_g):
    d, hh = cfg.d, cfg.heads
    w1_n = 4 * cfg.w_hd
    nbg = 4 * hh
    xs = (x_prompt.reshape(cfg.tok_ctx, d), x_sample.reshape(-1, d))
    cvec = jnp.zeros((MOD_ROWS, d), F32).at[0].set(c_ctx).at[1:1 + cfg.dec_batch].set(c)
    mod = _ada(cvec, w_ada, b_ada)
    w_in_t = jnp.swapaxes(w_in, 1, 2)
    w_pa_b, w_pb_b, w_o_b = w_pa.astype(BF16), w_pb.astype(BF16), w_o.astype(BF16)
    w_down_b = _cast_pad_rows(w_down, cfg.ff_pad, CAST_PAD_ROWS)
    states = None
    n2 = 3 * cfg.w_sc + 2 * d
    ff_tn, ff_sub = cfg.ff_tn, FFN_SUBTILES
    ff_w = ff_tn // ff_sub
    assert cfg.tm == cfg.dec_seq and cfg.tm % cfg.seq == 0 and cfg.d_ff % ff_w == 0 and cfg.ff_pad % ff_tn == 0
    ff_weights = [WCols(w_up, 0, half * cfg.d_ff + u * ff_w, 0, ff_sub)
                  for u in range(ff_sub) for half in range(2)]
    for l in range(cfg.depth):
        mod3 = mod[l].reshape(MOD_ROWS * 6, 1, d)
        h = _norm_mod(xs, norm1_g[l], mod3, 0, 1, cfg)
        p1 = _proj("proj_qkvz", h, w_in_t, l, 0, 0, w1_n, cfg)
        bg = _bg_proj(h, w_in_t, l, w1_n, nbg, cfg)
        p2 = _proj("proj_sc_gates", h, w_in_t, l, w1_n, nbg, n2, cfg)
        o_ctx, states = _deltanet(p1, bg, conv_qkv[l], a_log[l], dt_bias[l], onorm_g[l], None, cfg,
                                  latent=False, prev_states=states)
        o_lat, = _deltanet(p1, bg, conv_qkv[l], a_log[l], dt_bias[l], onorm_g[l], state_dn[:, l], cfg, latent=True)
        y_sc = _sc_mixer(p2, conv_sc[l], cfg)
        merged = _merge((o_ctx, o_lat), y_sc, w_pa_b, w_pb_b, l, p2, 3 * cfg.w_sc, cfg)
        xs = (_resid(merged, w_o_b, l, xs, mod3, 2, d, cfg, "out_proj"),)
        h = _norm_mod(xs, norm2_g[l], mod3, 3, 4, cfg)
        f = _wres_matmul(
            "ffn_up", [h], [wc._replace(layer=l) for wc in ff_weights], 2,
            [(conv_ff, (None, 3, ff_tn), lambda j, i, l=l: (l, 0, j))],
            functools.partial(_ffn_up_epilogue, cfg=cfg), cfg.ff_pad, BF16, cfg.tm, ff_w, ff_tn, cfg.kc)
        xs = (_resid(f, w_down_b, l, xs, mod3, 5, cfg.ff_tk, cfg, "ffn_down"),)
    x, = xs
    y_prompt = _final_norm(x, final_g, 0, cfg.tok_ctx, cfg).reshape(x_prompt.shape)
    y_sample = _final_norm(x, final_g, cfg.tok_ctx, cfg.tok - cfg.tok_ctx, cfg).reshape(x_sample.shape)
    return y_prompt, y_sample, states


def kernel(x_prompt, x_sample, state_dn, c, c_ctx, norm1_g, norm2_g, w_ada, b_ada, w_in, conv_qkv, a_log, dt_bias, onorm_g, conv_sc, w_pa, w_pb, w_o, w_up, conv_ff, w_down, final_g):
    return _forward(CFG, x_prompt, x_sample, state_dn, c, c_ctx, norm1_g, norm2_g, w_ada, b_ada, w_in,
                    conv_qkv, a_log, dt_bias, onorm_g, conv_sc, w_pa, w_pb, w_o, w_up, conv_ff,
                    w_down, final_g)
```
